```python
import numpy as np
import jax
import jax.numpy as jnp
from jax import lax

D_MODEL = 1024
BATCH = 4
SEQ = 8192
DEPTH = 4

POOL_WINDOWS = (2, 4, 8, 16)
N_POOL_GROUPS = 4
POOL_WIDTH = D_MODEL // 2
POOL_GROUP = POOL_WIDTH // N_POOL_GROUPS
HEAD_DIM = 64
N_HEADS = D_MODEL // HEAD_DIM
N_KV = 2
HEADS_PER_KV = N_HEADS // N_KV
ROPE_DIM = HEAD_DIM // 4
ROPE_THETA = 500000.0
CMP_LEN = 32
CMP_STRIDE = 16
CMP_HIDDEN = 4 * HEAD_DIM
SEL_LEN = 64
SEL_TOP = 16
WINDOW = 512
Q_BLOCK = 128
FORCED_SCORE = 1e9
Q_WIDTH = N_HEADS * HEAD_DIM
KV_WIDTH = N_KV * HEAD_DIM
IN_WIDTH = POOL_WIDTH + Q_WIDTH + 6 * KV_WIDTH + 3 * N_HEADS + 2 * D_MODEL
N_EXPERTS = 32
TOP_K = 4
D_EXPERT = D_MODEL
SWIGLU_LIMIT = 7.0
SWIGLU_ALPHA = 1.702
MOE_BLOCK = 256
EPS = 1e-6
NEG = -1e30

kernel_name = 'hybrid_pool_nsa_moe_trunk'


def rms_norm(x, gain):
    xf = x.astype(jnp.float32)
    y = xf * lax.rsqrt(jnp.mean(xf * xf, axis=-1, keepdims=True) + EPS)
    return (y * gain.astype(jnp.float32)).astype(x.dtype)


def partial_rope(x, pos):
    half = ROPE_DIM // 2
    inv_freq = ROPE_THETA ** (-jnp.arange(half, dtype=jnp.float32) / half)
    ang = pos.astype(jnp.float32)[:, None] * inv_freq[None, :]
    shape = (pos.shape[0],) + (1,) * (x.ndim - 3) + (half,)
    cos = jnp.cos(ang).reshape(shape)
    sin = jnp.sin(ang).reshape(shape)
    xr = x[..., :ROPE_DIM].astype(jnp.float32)
    x1, x2 = xr[..., :half], xr[..., half:]
    rot = jnp.concatenate([x1 * cos - x2 * sin, x2 * cos + x1 * sin], axis=-1)
    return jnp.concatenate([rot.astype(x.dtype), x[..., ROPE_DIM:]], axis=-1)


def pool_mixer(u, w_pool, pool_scale):
    b, s, _ = u.shape
    uf = u.astype(jnp.float32).reshape(b, s, N_POOL_GROUPS, POOL_GROUP)
    cs = jnp.pad(jnp.cumsum(uf, axis=1), ((0, 0), (1, 0), (0, 0), (0, 0)))
    t = jnp.arange(s)[:, None]
    win = jnp.asarray(POOL_WINDOWS, jnp.int32)[None, :]
    lo = jnp.maximum(t + 1 - win, 0)
    total = cs[:, 1:] - cs[:, lo, jnp.arange(N_POOL_GROUPS)]
    mean = total / (t + 1 - lo).astype(jnp.float32)[None, :, :, None]
    y = jnp.einsum('bsgc,gce->bsge', mean - uf, w_pool.astype(jnp.float32))
    y = y.reshape(b, s, POOL_WIDTH) * pool_scale.astype(jnp.float32)
    return y.astype(u.dtype)


def masked_softmax(scores, mask):
    return jax.nn.softmax(jnp.where(mask, scores, NEG), axis=-1)


def nsa_mixer(q, k_cmp, v_cmp, k_slc, v_slc, k_win, v_win, gate_logits,
              q_norm, k_norm, cmp_pos, cmp_w1, cmp_b1, cmp_w2):
    b, s = q.shape[0], q.shape[1]
    pos = jnp.arange(s, dtype=jnp.int32)
    q = partial_rope(rms_norm(q, q_norm), pos)
    k_slc = partial_rope(rms_norm(k_slc, k_norm[1]), pos)
    k_win = partial_rope(rms_norm(k_win, k_norm[2]), pos)

    n_cmp = (s - CMP_LEN) // CMP_STRIDE + 1
    cmp_idx = np.arange(n_cmp)[:, None] * CMP_STRIDE + np.arange(CMP_LEN)[None, :]
    cmp_end = jnp.asarray(cmp_idx[:, -1], jnp.int32)

    def compress(t, i):
        blocks = t[:, cmp_idx] + cmp_pos[i][:, None, :]
        flat = jnp.swapaxes(blocks, 2, 3).reshape(b, n_cmp, N_KV, CMP_LEN * HEAD_DIM)
        hid = jax.nn.silu(flat @ cmp_w1[i] + cmp_b1[i])
        return hid @ cmp_w2[i]

    kc = partial_rope(rms_norm(compress(k_cmp, 0), k_norm[0]), cmp_end)
    vc = compress(v_cmp, 1)

    n_sel = s // SEL_LEN
    n_top = min(SEL_TOP, n_sel)
    sel_start = np.arange(n_sel) * SEL_LEN
    overlap = jnp.asarray(((cmp_idx[:, :1] < sel_start[None, :] + SEL_LEN)
                           & (cmp_idx[:, -1:] >= sel_start[None, :])).astype(np.float32))
    ks_blk = k_slc.reshape(b, n_sel, SEL_LEN, N_KV, HEAD_DIM).transpose(0, 3, 1, 2, 4)
    vs_blk = v_slc.reshape(b, n_sel, SEL_LEN, N_KV, HEAD_DIM).transpose(0, 3, 1, 2, 4)

    kw_pad = jnp.pad(k_win, ((0, 0), (WINDOW, 0), (0, 0), (0, 0)))
    vw_pad = jnp.pad(v_win, ((0, 0), (WINDOW, 0), (0, 0), (0, 0)))
    span = WINDOW + Q_BLOCK

    n_q = s // Q_BLOCK
    q_blk = jnp.moveaxis(q.reshape(b, n_q, Q_BLOCK, N_KV, HEADS_PER_KV, HEAD_DIM), 1, 0)
    g_blk = jnp.moveaxis(gate_logits.reshape(b, n_q, Q_BLOCK, 3, N_KV, HEADS_PER_KV), 1, 0)
    b_ix = jnp.arange(b)[:, None, None, None]
    g_ix = jnp.arange(N_KV)[None, :, None, None]
    sel_ids = jnp.arange(n_sel)
    scale = HEAD_DIM ** -0.5

    def block(args):
        qi, gi, blk = args
        q0 = blk * Q_BLOCK
        t = q0 + jnp.arange(Q_BLOCK)
        s_c = jnp.einsum('bqghd,bngd->bgqhn', qi, kc).astype(jnp.float32) * scale
        m_c = (cmp_end[None, :] <= t[:, None])[None, None, :, None, :]
        p_c = jnp.where(m_c, masked_softmax(s_c, m_c), 0.0)
        o_c = jnp.einsum('bgqhn,bngd->bqghd', p_c.astype(vc.dtype), vc)
        imp = jnp.einsum('bgqhn,ns->bgqs', p_c, overlap)
        cur = t // SEL_LEN
        forced = (sel_ids[None, :] == cur[:, None]) | (sel_ids[None, :] == 0)
        causal_blk = sel_ids[None, :] <= cur[:, None]
        imp = jnp.where(forced, FORCED_SCORE, jnp.where(causal_blk, imp, -1.0))
        _, sel = lax.top_k(imp, n_top)
        ks_g = ks_blk[b_ix, g_ix, sel].reshape(b, N_KV, Q_BLOCK, n_top * SEL_LEN, HEAD_DIM)
        vs_g = vs_blk[b_ix, g_ix, sel].reshape(b, N_KV, Q_BLOCK, n_top * SEL_LEN, HEAD_DIM)
        kp = (sel[..., None] * SEL_LEN + jnp.arange(SEL_LEN)).reshape(b, N_KV, Q_BLOCK, n_top * SEL_LEN)
        s_s = jnp.einsum('bqghd,bgqnd->bgqhn', qi, ks_g).astype(jnp.float32) * scale
        m_s = (kp <= t[None, None, :, None])[:, :, :, None, :]
        p_s = masked_softmax(s_s, m_s)
        o_s = jnp.einsum('bgqhn,bgqnd->bqghd', p_s.astype(vs_g.dtype), vs_g)
        kw = lax.dynamic_slice_in_dim(kw_pad, q0, span, axis=1)
        vw = lax.dynamic_slice_in_dim(vw_pad, q0, span, axis=1)
        kpos = q0 - WINDOW + jnp.arange(span)
        diff = t[:, None] - kpos[None, :]
        m_w = ((diff >= 0) & (diff < WINDOW) & (kpos[None, :] >= 0))[None, None, :, None, :]
        s_w = jnp.einsum('bqghd,bngd->bgqhn', qi, kw).astype(jnp.float32) * scale
        p_w = masked_softmax(s_w, m_w)
        o_w = jnp.einsum('bgqhn,bngd->bqghd', p_w.astype(vw.dtype), vw)
        g = jax.nn.sigmoid(gi.astype(jnp.float32))[..., None]
        out = (g[:, :, 0] * o_c.astype(jnp.float32) + g[:, :, 1] * o_s.astype(jnp.float32)
               + g[:, :, 2] * o_w.astype(jnp.float32))
        return out.astype(qi.dtype)

    out = lax.map(block, (q_blk, g_blk, jnp.arange(n_q, dtype=jnp.int32)))
    return jnp.moveaxis(out, 0, 1).reshape(b, s, N_HEADS * HEAD_DIM)


def moe_ffn(h, router_w, router_b, w_gate_up, b_gate_up, w_down, b_down):
    n, d = h.shape
    logits = (h @ router_w + router_b).astype(jnp.float32)
    top_v, top_e = lax.top_k(logits, TOP_K)
    gate = jax.nn.softmax(top_v, axis=-1)
    nk = n * TOP_K
    flat_e = top_e.reshape(-1)
    flat_tok = jnp.repeat(jnp.arange(n, dtype=jnp.int32), TOP_K)
    flat_g = gate.reshape(-1)
    order = jnp.argsort(flat_e)
    sorted_e = flat_e[order]
    counts = jnp.bincount(flat_e, length=N_EXPERTS)
    starts = jnp.cumsum(counts) - counts
    padded = (counts + MOE_BLOCK - 1) // MOE_BLOCK * MOE_BLOCK
    pad_end = jnp.cumsum(padded)
    pad_start = pad_end - padded
    dest = pad_start[sorted_e] + jnp.arange(nk) - starts[sorted_e]
    n_blocks = -(-nk // MOE_BLOCK) + N_EXPERTS
    n_rows = n_blocks * MOE_BLOCK
    row_tok = jnp.zeros((n_rows,), jnp.int32).at[dest].set(flat_tok[order])
    row_w = jnp.zeros((n_rows,), jnp.float32).at[dest].set(flat_g[order])
    blk_e = jnp.minimum(jnp.searchsorted(pad_end, jnp.arange(n_blocks) * MOE_BLOCK, side='right'),
                        N_EXPERTS - 1)

    def expert_block(args):
        toks, e = args
        xb = h[toks]
        gu = xb @ w_gate_up[e] + b_gate_up[e]
        g_lin, u_lin = gu[:, 0::2], gu[:, 1::2]
        g_lin = jnp.minimum(g_lin, SWIGLU_LIMIT)
        u_lin = jnp.clip(u_lin, -SWIGLU_LIMIT, SWIGLU_LIMIT)
        act = (u_lin + 1) * g_lin * jax.nn.sigmoid(g_lin * SWIGLU_ALPHA)
        return act @ w_down[e] + b_down[e]

    y = lax.map(expert_block, (row_tok.reshape(n_blocks, MOE_BLOCK), blk_e))
    y = y.reshape(n_rows, d).astype(jnp.float32) * row_w[:, None]
    return jax.ops.segment_sum(y, row_tok, num_segments=n).astype(h.dtype)


def hybrid_layer(x, c, norm1, norm2, ada_w, ada_b, w_in, w_pool, pool_scale, q_norm, k_norm,
                 cmp_pos, cmp_w1, cmp_b1, cmp_w2, w_proj_a, w_proj_b, w_out,
                 router_w, router_b, w_gate_up, b_gate_up, w_down, b_down):
    b, s, d = x.shape
    mod = jax.nn.silu(c) @ ada_w + ada_b
    shift1, scale1, gate1, shift2, scale2, gate2 = [m[:, None, :] for m in jnp.split(mod, 6, axis=-1)]

    h = rms_norm(x, norm1) * (1 + scale1) + shift1
    z = h @ w_in
    offsets = np.cumsum([POOL_WIDTH, Q_WIDTH] + [KV_WIDTH] * 6 + [3 * N_HEADS])
    u_pool, q, kcm, vcm, ksl, vsl, kwn, vwn, nsa_g, merge_g = jnp.split(
        z, [int(o) for o in offsets], axis=-1)

    def kv(t):
        return t.reshape(b, s, N_KV, HEAD_DIM)

    y_a = pool_mixer(u_pool, w_pool, pool_scale) @ w_proj_a
    y_b = nsa_mixer(q.reshape(b, s, N_KV, HEADS_PER_KV, HEAD_DIM), kv(kcm), kv(vcm), kv(ksl), kv(vsl),
                    kv(kwn), kv(vwn), nsa_g.reshape(b, s, 3, N_KV, HEADS_PER_KV),
                    q_norm, k_norm, cmp_pos, cmp_w1, cmp_b1, cmp_w2) @ w_proj_b
    g_a, g_b = jnp.split(jax.nn.sigmoid(merge_g), 2, axis=-1)
    x = x + gate1 * ((g_a * y_a + g_b * y_b) @ w_out)

    h2 = rms_norm(x, norm2) * (1 + scale2) + shift2
    y_ffn = moe_ffn(h2.reshape(b * s, d), router_w, router_b, w_gate_up, b_gate_up, w_down, b_down)
    return x + gate2 * y_ffn.reshape(b, s, d)


def setup_inputs(seed: int = 0) -> dict:
    key = jax.random.key(seed)
    keys = list(jax.random.split(key, 32))

    def nrm(shape, scale):
        return jax.random.normal(keys.pop(), shape, jnp.float32) * scale

    D, E, F = D_MODEL, N_EXPERTS, D_EXPERT
    return {
        'x': nrm((BATCH, SEQ, D), 1.0),
        'c': nrm((BATCH, D), 1.0),
        'norm1': 1.0 + nrm((DEPTH, D), 0.02),
        'norm2': 1.0 + nrm((DEPTH, D), 0.02),
        'ada_w': nrm((DEPTH, D, 6 * D), 0.5 * D ** -0.5),
        'ada_b': nrm((DEPTH, 6 * D), 0.01),
        'w_in': nrm((DEPTH, D, IN_WIDTH), D ** -0.5),
        'w_pool': nrm((DEPTH, N_POOL_GROUPS, POOL_GROUP, POOL_GROUP), POOL_GROUP ** -0.5),
        'pool_scale': 1.0 + nrm((DEPTH, POOL_WIDTH), 0.02),
        'q_norm': 1.0 + nrm((DEPTH, HEAD_DIM), 0.02),
        'k_norm': 1.0 + nrm((DEPTH, 3, HEAD_DIM), 0.02),
        'cmp_pos': nrm((DEPTH, 2, CMP_LEN, HEAD_DIM), 0.02),
        'cmp_w1': nrm((DEPTH, 2, CMP_LEN * HEAD_DIM, CMP_HIDDEN), (CMP_LEN * HEAD_DIM) ** -0.5),
        'cmp_b1': nrm((DEPTH, 2, CMP_HIDDEN), 0.01),
        'cmp_w2': nrm((DEPTH, 2, CMP_HIDDEN, HEAD_DIM), CMP_HIDDEN ** -0.5),
        'w_proj_a': nrm((DEPTH, POOL_WIDTH, D), POOL_WIDTH ** -0.5),
        'w_proj_b': nrm((DEPTH, Q_WIDTH, D), Q_WIDTH ** -0.5),
        'w_out': nrm((DEPTH, D, D), D ** -0.5),
        'router_w': nrm((DEPTH, D, E), D ** -0.5),
        'router_b': nrm((DEPTH, E), 0.01),
        'w_gate_up': nrm((DEPTH, E, D, 2 * F), D ** -0.5),
        'b_gate_up': nrm((DEPTH, E, 2 * F), 0.01),
        'w_down': nrm((DEPTH, E, F, D), F ** -0.5),
        'b_down': nrm((DEPTH, E, D), 0.01),
    }


def reference(x, c, norm1, norm2, ada_w, ada_b, w_in, w_pool, pool_scale, q_norm, k_norm,
              cmp_pos, cmp_w1, cmp_b1, cmp_w2, w_proj_a, w_proj_b, w_out,
              router_w, router_b, w_gate_up, b_gate_up, w_down, b_down):
    for l in range(DEPTH):
        x = hybrid_layer(x, c, norm1[l], norm2[l], ada_w[l], ada_b[l], w_in[l], w_pool[l],
                         pool_scale[l], q_norm[l], k_norm[l], cmp_pos[l], cmp_w1[l], cmp_b1[l],
                         cmp_w2[l], w_proj_a[l], w_proj_b[l], w_out[l], router_w[l], router_b[l],
                         w_gate_up[l], b_gate_up[l], w_down[l], b_down[l])
    return x
```

```python
import functools

import numpy as np
import jax
import jax.numpy as jnp
from jax import lax
from jax.experimental import pallas as pl
from jax.experimental.pallas import tpu as pltpu

F32 = jnp.float32
BF16 = jnp.bfloat16
I32 = jnp.int32

POOL_WINDOWS = (2, 4, 8, 16)
POOL_GROUP = 128
POOL_WIDTH = 512
HEAD_DIM = 64
N_KV = 2
HEADS_PER_KV = 8
ROPE_DIM = 16
ROPE_HALF = 8
ROPE_THETA = 500000.0
CMP_LEN = 32
CMP_STRIDE = 16
CMP_HIDDEN = 256
SEL_LEN = 64
SEL_TOP = 16
WINDOW = 512
FORCED_SCORE = 1e9
N_EXPERTS = 32
TOP_K = 4
SWIGLU_LIMIT = 7.0
SWIGLU_ALPHA = 1.702
EPS = 1e-6
NEG = -1e30

LANES = 128
SUBLANES = 8
VMEM_LIMIT = 56 * 1024 * 1024

ROW_TILE = 256
POOL_HALO = 16
Q_TILE = 128
KV_TILE = 512
ROUTE_TILE = 512
EXPERT_ROWS = 512
MOVE_TILE = 128


def _cparams(sem):
    return pltpu.CompilerParams(dimension_semantics=sem, vmem_limit_bytes=VMEM_LIMIT)


def _bdot(a, b):
    return jnp.dot(a, b, preferred_element_type=F32)


def _split(a):
    hi = a.astype(BF16)
    lo = (a - hi.astype(F32)).astype(BF16)
    return hi, lo


def _sigmoid(x):
    return 1.0 / (1.0 + jnp.exp(-x))


def _mod_kernel(c_ref, w_ref, b_ref, o_ref):
    c = c_ref[...]
    a = c * _sigmoid(c)
    a_hi, a_lo = _split(a)
    w_hi, w_lo = _split(w_ref[...])
    o_ref[...] = _bdot(a_hi, w_hi) + _bdot(a_lo, w_hi) + _bdot(a_hi, w_lo) + b_ref[...]


def _modulation(c_pad, ada_w, ada_b):
    rows, d = c_pad.shape
    n = ada_w.shape[1]
    tn = 512
    return pl.pallas_call(
        _mod_kernel,
        out_shape=jax.ShapeDtypeStruct((rows, n), F32),
        grid=(n // tn,),
        in_specs=[pl.BlockSpec((rows, d), lambda i: (0, 0)),
                  pl.BlockSpec((d, tn), lambda i: (0, i)),
                  pl.BlockSpec((1, tn), lambda i: (0, i))],
        out_specs=pl.BlockSpec((rows, tn), lambda i: (0, i)),
        compiler_params=_cparams(("arbitrary",)),
        name="modulation",
    )(c_pad, ada_w, ada_b.reshape(1, n))


def _inproj_kernel(x_ref, g_ref, sc_ref, sh_ref, *refs):
    n_out = len(refs) // 2
    x = x_ref[...]
    ms = jnp.mean(x * x, axis=-1, keepdims=True)
    h = x * lax.rsqrt(ms + EPS) * g_ref[...]
    h = h * (1.0 + sc_ref[0]) + sh_ref[0]
    hb = h.astype(BF16)
    for w_ref, o_ref in zip(refs[:n_out], refs[n_out:]):
        o_ref[...] = _bdot(hb, w_ref[...])


def _in_projection(x2, gain, scale, shift, weights, seq):
    n, d = x2.shape
    tm = ROW_TILE
    tiles_per_b = seq // tm
    in_specs = [pl.BlockSpec((tm, d), lambda i: (i, 0)),
                pl.BlockSpec((1, d), lambda i: (0, 0)),
                pl.BlockSpec((1, 1, d), lambda i: (i // tiles_per_b, 0, 0)),
                pl.BlockSpec((1, 1, d), lambda i: (i // tiles_per_b, 0, 0))]
    in_specs += [pl.BlockSpec(w.shape, lambda i: (0, 0)) for w in weights]
    out_shape = [jax.ShapeDtypeStruct((n, w.shape[1]), F32) for w in weights]
    out_specs = [pl.BlockSpec((tm, w.shape[1]), lambda i: (i, 0)) for w in weights]
    return pl.pallas_call(
        _inproj_kernel,
        out_shape=out_shape,
        grid=(n // tm,),
        in_specs=in_specs,
        out_specs=out_specs,
        compiler_params=_cparams(("arbitrary",)),
        name="in_projection",
    )(x2, gain.reshape(1, d), scale[:, None, :], shift[:, None, :], *weights)


def _pool_kernel(u_ref, wp_ref, ps_ref, wa_ref, o_ref, ext_ref, *, ts):
    i = pl.program_id(1)

    @pl.when(i == 0)
    def _():
        ext_ref[0:POOL_HALO, :] = jnp.zeros((POOL_HALO, POOL_WIDTH), F32)

    @pl.when(i > 0)
    def _():
        ext_ref[0:POOL_HALO, :] = ext_ref[ts:ts + POOL_HALO, :]

    u = u_ref[...]
    ext_ref[POOL_HALO:POOL_HALO + ts, :] = u
    t = i * ts + lax.broadcasted_iota(I32, (ts, 1), 0)
    outs = []
    for gi, win in enumerate(POOL_WINDOWS):
        lo, hi = gi * POOL_GROUP, (gi + 1) * POOL_GROUP
        ug = u[:, lo:hi]
        total = ug
        for jj in range(1, win):
            total = total + ext_ref[POOL_HALO - jj:POOL_HALO - jj + ts, lo:hi]
        cnt = jnp.minimum(t + 1, win).astype(F32)
        dlt = total / cnt - ug
        outs.append(_bdot(dlt.astype(BF16), wp_ref[gi]))
    y = jnp.concatenate(outs, axis=-1) * ps_ref[...]
    o_ref[...] = _bdot(y.astype(BF16), wa_ref[...])


def _pool_mixer(u, w_pool, pool_scale, w_proj_a, batch, seq):
    n, d_out = u.shape[0], w_proj_a.shape[1]
    ts = ROW_TILE
    nt = seq // ts
    return pl.pallas_call(
        functools.partial(_pool_kernel, ts=ts),
        out_shape=jax.ShapeDtypeStruct((n, d_out), F32),
        grid=(batch, nt),
        in_specs=[pl.BlockSpec((ts, POOL_WIDTH), lambda b, i: (b * nt + i, 0)),
                  pl.BlockSpec(w_pool.shape, lambda b, i: (0, 0, 0)),
                  pl.BlockSpec((1, POOL_WIDTH), lambda b, i: (0, 0)),
                  pl.BlockSpec(w_proj_a.shape, lambda b, i: (0, 0))],
        out_specs=pl.BlockSpec((ts, d_out), lambda b, i: (b * nt + i, 0)),
        scratch_shapes=[pltpu.VMEM((ts + POOL_HALO, POOL_WIDTH), F32)],
        compiler_params=_cparams(("arbitrary", "arbitrary")),
        name="pool_mixer",
    )(u, w_pool, pool_scale.reshape(1, POOL_WIDTH), w_proj_a)


def _rope_tables(pos):
    inv_freq = np.float32(ROPE_THETA) ** (-(np.arange(ROPE_HALF, dtype=np.float32)) / np.float32(ROPE_HALF))
    ang = pos.astype(np.float32)[:, None] * inv_freq.astype(np.float32)[None, :]
    cos, sin = np.cos(ang), np.sin(ang)
    c = np.ones((pos.shape[0], HEAD_DIM), np.float32)
    s = np.zeros((pos.shape[0], HEAD_DIM), np.float32)
    c[:, :ROPE_HALF] = cos
    c[:, ROPE_HALF:ROPE_DIM] = cos
    s[:, :ROPE_HALF] = -sin
    s[:, ROPE_HALF:ROPE_DIM] = sin
    return c, s


def _norm_rope_pair(x, gain, cos, sin):
    rows = x.shape[0]
    lane = lax.broadcasted_iota(I32, (rows, LANES), 1)
    first = lane < HEAD_DIM
    x2 = x * x
    s0 = jnp.sum(jnp.where(first, x2, 0.0), axis=-1, keepdims=True)
    s1 = jnp.sum(jnp.where(first, 0.0, x2), axis=-1, keepdims=True)
    r = lax.rsqrt(jnp.where(first, s0, s1) * (1.0 / HEAD_DIM) + EPS)
    y = x * r * gain
    low = (lane & (HEAD_DIM - 1)) < ROPE_HALF
    sw = jnp.where(low, pltpu.roll(y, LANES - ROPE_HALF, 1), pltpu.roll(y, ROPE_HALF, 1))
    return y * cos + sw * sin


def _kprep_kernel(kv_ref, gain_ref, cos_ref, sin_ref, ks_ref, vs_ref, kw_ref, vw_ref):
    cos, sin = cos_ref[...], sin_ref[...]
    ks = _norm_rope_pair(kv_ref[:, 0:LANES], gain_ref[0:1, :], cos, sin).T.astype(BF16)
    kw = _norm_rope_pair(kv_ref[:, 2 * LANES:3 * LANES], gain_ref[1:2, :], cos, sin).T.astype(BF16)
    for gi in range(N_KV):
        ks_ref[0, gi] = ks[gi * HEAD_DIM:(gi + 1) * HEAD_DIM, :]
        kw_ref[0, gi] = kw[gi * HEAD_DIM:(gi + 1) * HEAD_DIM, :]
    vs_ref[0] = kv_ref[:, LANES:2 * LANES].astype(BF16)
    vw_ref[0] = kv_ref[:, 3 * LANES:4 * LANES].astype(BF16)


def _kv_prep(kv4, gains, cos2, sin2, batch, seq):
    tm = ROW_TILE
    nt = seq // tm
    kt_shape = jax.ShapeDtypeStruct((batch, N_KV, HEAD_DIM, seq), BF16)
    v_shape = jax.ShapeDtypeStruct((batch, seq, LANES), BF16)
    kt_spec = pl.BlockSpec((1, N_KV, HEAD_DIM, tm), lambda b, i: (b, 0, 0, i))
    v_spec = pl.BlockSpec((1, tm, LANES), lambda b, i: (b, i, 0))
    return pl.pallas_call(
        _kprep_kernel,
        out_shape=[kt_shape, v_shape, kt_shape, v_shape],
        grid=(batch, nt),
        in_specs=[pl.BlockSpec((tm, 4 * LANES), lambda b, i: (b * nt + i, 0)),
                  pl.BlockSpec((2, LANES), lambda b, i: (0, 0)),
                  pl.BlockSpec((tm, LANES), lambda b, i: (i, 0)),
                  pl.BlockSpec((tm, LANES), lambda b, i: (i, 0))],
        out_specs=[kt_spec, v_spec, kt_spec, v_spec],
        compiler_params=_cparams(("arbitrary", "arbitrary")),
        name="kv_prep",
    )(kv4, gains, cos2, sin2)


def _compress_kernel(k_ref, v_ref, pos_ref, wa_ref, wb_ref, b1_ref, w2_ref, gain_ref, cos_ref, sin_ref,
                     kc_ref, vc_ref):
    nch = k_ref.shape[0]

    def mlp(x, idx):
        xa = (x + pos_ref[idx, 0:1, :]).astype(BF16)
        xb = (x + pos_ref[idx, 1:2, :]).astype(BF16)
        a = _bdot(xa, wa_ref[idx])
        b = _bdot(xb, wb_ref[idx])
        b_next = jnp.concatenate([b[1:nch, :], jnp.zeros((1, b.shape[1]), F32)], axis=0)
        pre = a + b_next + b1_ref[idx]
        hid = pre * _sigmoid(pre)
        return _bdot(hid.astype(BF16), w2_ref[idx])

    kc = _norm_rope_pair(mlp(k_ref[...], 0), gain_ref[...], cos_ref[...], sin_ref[...]).T.astype(BF16)
    for gi in range(N_KV):
        kc_ref[0, gi] = kc[gi * HEAD_DIM:(gi + 1) * HEAD_DIM, :]
    vc_ref[0] = mlp(v_ref[...], 1).astype(BF16)


def _compress(kcm, vcm, pos2, wa, wb, b1, w2, gain, cos_c, sin_c, batch, seq):
    nch = seq // CMP_STRIDE
    width = CMP_STRIDE * LANES
    full = lambda a: pl.BlockSpec(a.shape, lambda b: (0,) * a.ndim)
    return pl.pallas_call(
        _compress_kernel,
        out_shape=[jax.ShapeDtypeStruct((batch, N_KV, HEAD_DIM, nch), BF16),
                   jax.ShapeDtypeStruct((batch, nch, LANES), BF16)],
        grid=(batch,),
        in_specs=[pl.BlockSpec((nch, width), lambda b: (b, 0)),
                  pl.BlockSpec((nch, width), lambda b: (b, 0)),
                  full(pos2), full(wa), full(wb), full(b1), full(w2), full(gain), full(cos_c), full(sin_c)],
        out_specs=[pl.BlockSpec((1, N_KV, HEAD_DIM, nch), lambda b: (b, 0, 0, 0)),
                   pl.BlockSpec((1, nch, LANES), lambda b: (b, 0, 0))],
        compiler_params=_cparams(("arbitrary",)),
        name="compress",
    )(kcm.reshape(batch * nch, width), vcm.reshape(batch * nch, width), pos2, wa, wb, b1, w2, gain, cos_c, sin_c)


def _attn_kernel(q_ref, gl_ref, ks_ref, vs_ref, kw_ref, vw_ref, kc_ref, vc_ref,
                 cos_ref, sin_ref, qg_ref, psw_ref, ovl_ref, eneg_ref, place_ref,
                 o_ref, m_scr, l_scr, acc_scr, *, seq):
    H, QB, TK = HEADS_PER_KV, Q_TILE, KV_TILE
    R = H * QB
    g = pl.program_id(1)
    q0 = pl.program_id(2) * QB
    ncp = seq // CMP_STRIDE
    nsel = seq // SEL_LEN
    n_top = min(SEL_TOP, nsel)

    xq = q_ref[...]
    rows = jnp.concatenate([xq[:, h * HEAD_DIM:(h + 1) * HEAD_DIM] for h in range(H)], axis=0)
    ms = jnp.mean(rows * rows, axis=-1, keepdims=True)
    y = rows * lax.rsqrt(ms + EPS) * qg_ref[...]
    cos = jnp.concatenate([cos_ref[...]] * H, axis=0)
    sin = jnp.concatenate([sin_ref[...]] * H, axis=0)
    y_hi, y_lo = _split(y)
    sw = _bdot(y_hi, psw_ref[...]) + _bdot(y_lo, psw_ref[...])
    qb = ((y * cos + sw * sin) * (HEAD_DIM ** -0.5)).astype(BF16)

    def row_pos(width):
        rr = lax.broadcasted_iota(I32, (R, width), 0)
        return q0 + (rr & (QB - 1))

    s_c = _bdot(qb, kc_ref[0, 0])
    nn = lax.broadcasted_iota(I32, (R, ncp), 1)
    valid = (nn * CMP_STRIDE + (CMP_LEN - 1)) <= row_pos(ncp)
    s_c = jnp.where(valid, s_c, NEG)
    p = jnp.where(valid, jnp.exp(s_c - jnp.max(s_c, axis=-1, keepdims=True)), 0.0)
    l = jnp.sum(p, axis=-1, keepdims=True)
    p = p * jnp.where(l > 0.0, 1.0 / l, 0.0)
    o_c = _bdot(p.astype(BF16), vc_ref[0])

    ps = p[0:QB]
    for h in range(1, H):
        ps = ps + p[h * QB:(h + 1) * QB]
    ps_hi, ps_lo = _split(ps)
    imp = _bdot(ps_hi, ovl_ref[...]) + _bdot(ps_lo, ovl_ref[...])
    sidx = lax.broadcasted_iota(I32, (QB, nsel), 1)
    cur = (q0 + lax.broadcasted_iota(I32, (QB, nsel), 0)) // SEL_LEN
    causal = sidx <= cur
    forced = (sidx == cur) | (sidx == 0)
    v = jnp.where(forced, FORCED_SCORE, jnp.where(causal, imp, -1.0))
    sel = jnp.zeros((QB, nsel), jnp.bool_)
    for _ in range(n_top):
        mx = jnp.max(v, axis=-1, keepdims=True)
        first = jnp.min(jnp.where(v == mx, sidx, nsel), axis=-1, keepdims=True)
        pick = sidx == first
        sel = sel | pick
        v = jnp.where(pick, -3e38, v)
    notsel = jnp.where(sel & causal, 0.0, 1.0).astype(BF16)
    q_aug = jnp.concatenate([jnp.concatenate([notsel] * H, axis=0), qb], axis=1)

    m_scr[...] = jnp.full((R, 1), NEG, F32)
    l_scr[...] = jnp.zeros((R, 1), F32)
    acc_scr[...] = jnp.zeros((R, LANES), F32)

    def sel_tile(kt, causal_tile):
        off = pl.multiple_of(kt * TK, TK)
        k_aug = jnp.concatenate([eneg_ref[:, pl.ds(off, TK)], ks_ref[0, 0, :, pl.ds(off, TK)]], axis=0)
        s = _bdot(q_aug, k_aug)
        if causal_tile:
            kpos = off + lax.broadcasted_iota(I32, (R, TK), 1)
            s = jnp.where(kpos <= row_pos(TK), s, NEG)
        m_old = m_scr[...]
        m_new = jnp.maximum(m_old, jnp.max(s, axis=-1, keepdims=True))
        alpha = jnp.exp(m_old - m_new)
        pt = jnp.exp(s - m_new)
        l_scr[...] = alpha * l_scr[...] + jnp.sum(pt, axis=-1, keepdims=True)
        acc_scr[...] = alpha * acc_scr[...] + _bdot(pt.astype(BF16), vs_ref[0, pl.ds(off, TK), :])
        m_scr[...] = m_new

    n_tiles = (q0 + QB - 1) // TK + 1

    def body(kt, carry):
        sel_tile(kt, False)
        return carry

    lax.fori_loop(0, n_tiles - 1, body, 0)
    sel_tile(n_tiles - 1, True)
    o_s = acc_scr[...] / l_scr[...]

    span = WINDOW + QB
    base = pl.multiple_of(jnp.maximum(q0 - WINDOW, 0), QB)
    s_w = _bdot(qb, kw_ref[0, 0, :, pl.ds(base, span)])
    dist = row_pos(span) - (base + lax.broadcasted_iota(I32, (R, span), 1))
    inwin = (dist >= 0) & (dist < WINDOW)
    s_w = jnp.where(inwin, s_w, NEG)
    pw = jnp.exp(s_w - jnp.max(s_w, axis=-1, keepdims=True))
    pw = pw / jnp.sum(pw, axis=-1, keepdims=True)
    o_w = _bdot(pw.astype(BF16), vw_ref[0, pl.ds(base, span), :])

    gs = _sigmoid(gl_ref[...])

    def gate_col(c):
        return jnp.concatenate([gs[:, c * H + h:c * H + h + 1] for h in range(H)], axis=0)

    out = gate_col(0) * o_c + gate_col(1) * o_s + gate_col(2) * o_w
    out = jnp.where(g == 0, out[:, 0:HEAD_DIM], out[:, HEAD_DIM:2 * HEAD_DIM]).astype(BF16)
    res = _bdot(out[0:QB], place_ref[0])
    for h in range(1, H):
        res = res + _bdot(out[h * QB:(h + 1) * QB], place_ref[h])
    o_ref[...] = res.astype(BF16)


def _attention(q, gl, ks, vs, kw, vw, kc, vc, cos_q, sin_q, qgain, pswap, ovl, eneg, place, batch, seq):
    n = q.shape[0]
    QB = Q_TILE
    nq = seq // QB
    R = HEADS_PER_KV * QB
    gw = HEADS_PER_KV * HEAD_DIM
    ncp = seq // CMP_STRIDE
    const2 = lambda a: pl.BlockSpec(a.shape, lambda b, g, j: (0,) * a.ndim)
    return pl.pallas_call(
        functools.partial(_attn_kernel, seq=seq),
        out_shape=jax.ShapeDtypeStruct((n, N_KV * gw), BF16),
        grid=(batch, N_KV, nq),
        in_specs=[pl.BlockSpec((QB, gw), lambda b, g, j: (b * nq + j, g)),
                  pl.BlockSpec((QB, LANES), lambda b, g, j: (b * nq + j, g)),
                  pl.BlockSpec((1, 1, HEAD_DIM, seq), lambda b, g, j: (b, g, 0, 0)),
                  pl.BlockSpec((1, seq, LANES), lambda b, g, j: (b, 0, 0)),
                  pl.BlockSpec((1, 1, HEAD_DIM, seq), lambda b, g, j: (b, g, 0, 0)),
                  pl.BlockSpec((1, seq, LANES), lambda b, g, j: (b, 0, 0)),
                  pl.BlockSpec((1, 1, HEAD_DIM, ncp), lambda b, g, j: (b, g, 0, 0)),
                  pl.BlockSpec((1, ncp, LANES), lambda b, g, j: (b, 0, 0)),
                  pl.BlockSpec((QB, HEAD_DIM), lambda b, g, j: (j, 0)),
                  pl.BlockSpec((QB, HEAD_DIM), lambda b, g, j: (j, 0)),
                  const2(qgain), const2(pswap), const2(ovl), const2(eneg), const2(place)],
        out_specs=pl.BlockSpec((QB, gw), lambda b, g, j: (b * nq + j, g)),
        scratch_shapes=[pltpu.VMEM((R, 1), F32), pltpu.VMEM((R, 1), F32), pltpu.VMEM((R, LANES), F32)],
        compiler_params=_cparams(("arbitrary", "arbitrary", "arbitrary")),
        name="sparse_attention",
    )(q, gl, ks, vs, kw, vw, kc, vc, cos_q, sin_q, qgain, pswap, ovl, eneg, place)


def _merge_kernel(x_ref, att_ref, ya_ref, mg_ref, g1_ref, wb_ref, wo_ref, o_ref):
    d = x_ref.shape[1]
    y_b = _bdot(att_ref[...], wb_ref[...])
    mg = mg_ref[...]
    merged = _sigmoid(mg[:, 0:d]) * ya_ref[...] + _sigmoid(mg[:, d:2 * d]) * y_b
    o_ref[...] = x_ref[...] + g1_ref[0] * _bdot(merged.astype(BF16), wo_ref[...])


def _merge(x2, att, y_a, mg, gate1, w_proj_b, w_out, seq):
    n, d = x2.shape
    tm = ROW_TILE
    tiles_per_b = seq // tm
    return pl.pallas_call(
        _merge_kernel,
        out_shape=jax.ShapeDtypeStruct((n, d), F32),
        grid=(n // tm,),
        in_specs=[pl.BlockSpec((tm, d), lambda i: (i, 0)),
                  pl.BlockSpec((tm, d), lambda i: (i, 0)),
                  pl.BlockSpec((tm, d), lambda i: (i, 0)),
                  pl.BlockSpec((tm, 2 * d), lambda i: (i, 0)),
                  pl.BlockSpec((1, 1, d), lambda i: (i // tiles_per_b, 0, 0)),
                  pl.BlockSpec(w_proj_b.shape, lambda i: (0, 0)),
                  pl.BlockSpec(w_out.shape, lambda i: (0, 0))],
        out_specs=pl.BlockSpec((tm, d), lambda i: (i, 0)),
        compiler_params=_cparams(("arbitrary",)),
        name="merge_out_projection",
    )(x2, att, y_a, mg, gate1[:, None, :], w_proj_b, w_out)


def _router_kernel(x_ref, g_ref, sc_ref, sh_ref, wh_ref, wl_ref, rb_ref, tri_ref,
                   h_ref, e_ref, gt_ref, pos_ref, cnt_ref, carry_ref):
    i = pl.program_id(0)
    tm, d = x_ref.shape

    @pl.when(i == 0)
    def _():
        carry_ref[...] = jnp.zeros((N_EXPERTS, 1), F32)

    x = x_ref[...]
    ms = jnp.mean(x * x, axis=-1, keepdims=True)
    h = x * lax.rsqrt(ms + EPS) * g_ref[...]
    h = h * (1.0 + sc_ref[0]) + sh_ref[0]
    for s in range(d // LANES):
        h_ref[pl.ds(s, tm, stride=SUBLANES), :] = h[:, s * LANES:(s + 1) * LANES]

    h_hi, h_lo = _split(h)
    nt = (((1,), (1,)), ((), ()))
    logits = (lax.dot_general(wh_ref[...], h_hi, nt, preferred_element_type=F32)
              + lax.dot_general(wh_ref[...], h_lo, nt, preferred_element_type=F32)
              + lax.dot_general(wl_ref[...], h_hi, nt, preferred_element_type=F32)
              + rb_ref[...])
    eidx = lax.broadcasted_iota(I32, (N_EXPERTS, tm), 0)
    onehot = jnp.zeros((N_EXPERTS, tm), F32)
    picks, vals, ids = [], [], []
    for _ in range(TOP_K):
        mx = jnp.max(logits, axis=0, keepdims=True)
        first = jnp.min(jnp.where(logits == mx, eidx, N_EXPERTS), axis=0, keepdims=True)
        pick = eidx == first
        picks.append(pick)
        vals.append(mx)
        ids.append(first)
        onehot = jnp.where(pick, 1.0, onehot)
        logits = jnp.where(pick, -3e38, logits)
    ex = [jnp.exp(vk - vals[0]) for vk in vals]
    den = ex[0] + ex[1] + ex[2] + ex[3]
    before = _bdot(onehot.astype(BF16), tri_ref[...]) + carry_ref[...]
    pad = SUBLANES - TOP_K
    pos = [jnp.sum(jnp.where(pk, before, 0.0), axis=0, keepdims=True) for pk in picks]
    e_ref[...] = jnp.concatenate(ids + [jnp.zeros((pad, tm), I32)], axis=0)
    gt_ref[...] = jnp.concatenate([e / den for e in ex] + [jnp.zeros((pad, tm), F32)], axis=0)
    pos_ref[...] = jnp.concatenate(pos + [jnp.zeros((pad, tm), F32)], axis=0).astype(I32)
    carry_ref[...] = carry_ref[...] + jnp.sum(onehot, axis=1, keepdims=True)
    cnt_ref[...] = carry_ref[...]


def _router(x2, gain, scale, shift, wr_hi, wr_lo, rb, tri, seq):
    n, d = x2.shape
    tm = ROUTE_TILE
    tiles_per_b = seq // tm
    nsub = d // LANES
    return pl.pallas_call(
        _router_kernel,
        out_shape=[jax.ShapeDtypeStruct((n * nsub, LANES), F32),
                   jax.ShapeDtypeStruct((SUBLANES, n), I32),
                   jax.ShapeDtypeStruct((SUBLANES, n), F32),
                   jax.ShapeDtypeStruct((SUBLANES, n), I32),
                   jax.ShapeDtypeStruct((N_EXPERTS, 1), F32)],
        grid=(n // tm,),
        in_specs=[pl.BlockSpec((tm, d), lambda i: (i, 0)),
                  pl.BlockSpec((1, d), lambda i: (0, 0)),
                  pl.BlockSpec((1, 1, d), lambda i: (i // tiles_per_b, 0, 0)),
                  pl.BlockSpec((1, 1, d), lambda i: (i // tiles_per_b, 0, 0)),
                  pl.BlockSpec(wr_hi.shape, lambda i: (0, 0)),
                  pl.BlockSpec(wr_lo.shape, lambda i: (0, 0)),
                  pl.BlockSpec((N_EXPERTS, 1), lambda i: (0, 0)),
                  pl.BlockSpec((tm, tm), lambda i: (0, 0))],
        out_specs=[pl.BlockSpec((tm * nsub, LANES), lambda i: (i, 0)),
                   pl.BlockSpec((SUBLANES, tm), lambda i: (0, i)),
                   pl.BlockSpec((SUBLANES, tm), lambda i: (0, i)),
                   pl.BlockSpec((SUBLANES, tm), lambda i: (0, i)),
                   pl.BlockSpec((N_EXPERTS, 1), lambda i: (0, 0))],
        scratch_shapes=[pltpu.VMEM((N_EXPERTS, 1), F32)],
        compiler_params=_cparams(("arbitrary",)),
        name="router",
    )(x2, gain.reshape(1, d), scale[:, None, :], shift[:, None, :], wr_hi, wr_lo, rb.reshape(N_EXPERTS, 1), tri)


def _dispatch_kernel(dest_ref, fill_ref, h_ref, xs_ref, zero_ref, sem, *, nsub):
    i = pl.program_id(0)
    tt = MOVE_TILE

    def row_copy(n, k):
        dst = dest_ref[0, 0, k * tt + n]
        return pltpu.make_async_copy(h_ref.at[pl.ds(n * nsub, nsub), :],
                                     xs_ref.at[pl.ds(dst * nsub, nsub), :], sem)

    def issue(n, carry):
        for k in range(TOP_K):
            row_copy(n, k).start()
        return carry

    def drain(n, carry):
        for k in range(TOP_K):
            row_copy(n, k).wait()
        return carry

    lax.fori_loop(0, tt, issue, 0)

    @pl.when(i == 0)
    def _():
        zero_ref[...] = jnp.zeros(zero_ref.shape, F32)

        def fill_copy(r):
            return pltpu.make_async_copy(zero_ref, xs_ref.at[pl.ds(r * nsub, nsub), :], sem)

        def per_expert(e, carry):
            lo, hi = fill_ref[2 * e], fill_ref[2 * e + 1]
            lax.fori_loop(lo, hi, lambda r, c: (fill_copy(r).start(), c)[1], 0)
            lax.fori_loop(lo, hi, lambda r, c: (fill_copy(r).wait(), c)[1], 0)
            return carry

        lax.fori_loop(0, fill_ref.shape[0] // 2, per_expert, 0)

    lax.fori_loop(0, tt, drain, 0)


def _dispatch(h_rows, dest_tiles, fill, n_rows, nsub):
    n_tok = h_rows.shape[0] // nsub
    tt = MOVE_TILE
    grid_spec = pltpu.PrefetchScalarGridSpec(
        num_scalar_prefetch=0,
        grid=(n_tok // tt,),
        in_specs=[pl.BlockSpec((1, 1, TOP_K * tt), lambda i: (i, 0, 0), memory_space=pltpu.SMEM),
                  pl.BlockSpec(memory_space=pltpu.SMEM),
                  pl.BlockSpec((tt * nsub, LANES), lambda i: (i, 0))],
        out_specs=pl.BlockSpec(memory_space=pl.ANY),
        scratch_shapes=[pltpu.VMEM((nsub, LANES), F32), pltpu.SemaphoreType.DMA(())],
    )
    return pl.pallas_call(
        functools.partial(_dispatch_kernel, nsub=nsub),
        out_shape=jax.ShapeDtypeStruct((n_rows * nsub, LANES), F32),
        grid_spec=grid_spec,
        compiler_params=pltpu.CompilerParams(dimension_semantics=("arbitrary",), vmem_limit_bytes=VMEM_LIMIT,
                                             has_side_effects=True),
        name="dispatch",
    )(dest_tiles, fill, h_rows)


def _expert_kernel(be_ref, nu_ref, xs_ref, wgu_ref, bgu_ref, wd_ref, bd_ref, ys_ref, *, nsub):
    i = pl.program_id(0)
    rows = EXPERT_ROWS
    f = wd_ref.shape[1]

    @pl.when(i < nu_ref[0])
    def _():
        x = jnp.concatenate([xs_ref[pl.ds(s, rows, stride=nsub), :] for s in range(nsub)], axis=-1)
        gu = _bdot(x.astype(BF16), wgu_ref[0]) + bgu_ref[0]
        g_lin = jnp.minimum(gu[:, 0:f], SWIGLU_LIMIT)
        u_lin = jnp.clip(gu[:, f:2 * f], -SWIGLU_LIMIT, SWIGLU_LIMIT)
        act = (u_lin + 1.0) * g_lin * _sigmoid(g_lin * SWIGLU_ALPHA)
        y = _bdot(act.astype(BF16), wd_ref[0]) + bd_ref[0]
        for s in range(nsub):
            ys_ref[pl.ds(s, rows, stride=nsub), :] = y[:, s * LANES:(s + 1) * LANES]

    @pl.when(i >= nu_ref[0])
    def _():
        ys_ref[...] = jnp.zeros(ys_ref.shape, F32)


def _experts(blk_e, n_used, xs, w_gu, b_gu, w_d, b_d, nsub):
    rows = EXPERT_ROWS
    n_blocks = xs.shape[0] // (rows * nsub)
    d, f2 = w_gu.shape[1], w_gu.shape[2]
    f = w_d.shape[1]
    row_map = lambda i, be, nu: (jnp.minimum(i, nu[0] - 1), 0)
    grid_spec = pltpu.PrefetchScalarGridSpec(
        num_scalar_prefetch=2,
        grid=(n_blocks,),
        in_specs=[pl.BlockSpec((rows * nsub, LANES), row_map),
                  pl.BlockSpec((1, d, f2), lambda i, be, nu: (be[i], 0, 0)),
                  pl.BlockSpec((1, 1, f2), lambda i, be, nu: (be[i], 0, 0)),
                  pl.BlockSpec((1, f, d), lambda i, be, nu: (be[i], 0, 0)),
                  pl.BlockSpec((1, 1, d), lambda i, be, nu: (be[i], 0, 0))],
        out_specs=pl.BlockSpec((rows * nsub, LANES), lambda i, be, nu: (i, 0)),
    )
    return pl.pallas_call(
        functools.partial(_expert_kernel, nsub=nsub),
        out_shape=jax.ShapeDtypeStruct(xs.shape, F32),
        grid_spec=grid_spec,
        compiler_params=_cparams(("arbitrary",)),
        name="experts",
    )(blk_e, n_used, xs, w_gu, b_gu, w_d, b_d)


def _combine_kernel(dest_ref, ys_ref, x_ref, gt_ref, g2_ref, o_ref, buf_ref, sem, *, nsub):
    tt = MOVE_TILE

    def row_copy(n, k):
        src = dest_ref[0, 0, k * tt + n]
        return pltpu.make_async_copy(ys_ref.at[pl.ds(src * nsub, nsub), :],
                                     buf_ref.at[k, pl.ds(n * nsub, nsub), :], sem)

    def issue(n, carry):
        for k in range(TOP_K):
            row_copy(n, k).start()
        return carry

    def drain(n, carry):
        for k in range(TOP_K):
            row_copy(n, k).wait()
        return carry

    lax.fori_loop(0, tt, issue, 0)
    lax.fori_loop(0, tt, drain, 0)
    gt = gt_ref[...]
    acc = None
    for k in range(TOP_K):
        yk = jnp.concatenate([buf_ref[k, pl.ds(s, tt, stride=nsub), :] for s in range(nsub)], axis=-1)
        term = gt[:, k:k + 1] * yk
        acc = term if acc is None else acc + term
    o_ref[...] = x_ref[...] + g2_ref[0] * acc


def _combine(dest_tiles, ys, x2, gates_t, gate2, seq, nsub):
    n, d = x2.shape
    tt = MOVE_TILE
    tiles_per_b = seq // tt
    grid_spec = pltpu.PrefetchScalarGridSpec(
        num_scalar_prefetch=0,
        grid=(n // tt,),
        in_specs=[pl.BlockSpec((1, 1, TOP_K * tt), lambda i: (i, 0, 0), memory_space=pltpu.SMEM),
                  pl.BlockSpec(memory_space=pl.ANY),
                  pl.BlockSpec((tt, d), lambda i: (i, 0)),
                  pl.BlockSpec((tt, SUBLANES), lambda i: (i, 0)),
                  pl.BlockSpec((1, 1, d), lambda i: (i // tiles_per_b, 0, 0))],
        out_specs=pl.BlockSpec((tt, d), lambda i: (i, 0)),
        scratch_shapes=[pltpu.VMEM((TOP_K, tt * nsub, LANES), F32), pltpu.SemaphoreType.DMA(())],
    )
    return pl.pallas_call(
        functools.partial(_combine_kernel, nsub=nsub),
        out_shape=jax.ShapeDtypeStruct((n, d), F32),
        grid_spec=grid_spec,
        compiler_params=_cparams(("arbitrary",)),
        name="combine",
    )(dest_tiles, ys, x2, gates_t, gate2[:, None, :])


def _attention_constants(seq):
    nsel = seq // SEL_LEN
    ncp = seq // CMP_STRIDE
    pswap = np.zeros((HEAD_DIM, HEAD_DIM), np.float32)
    for dd in range(ROPE_HALF):
        pswap[dd + ROPE_HALF, dd] = 1.0
        pswap[dd, dd + ROPE_HALF] = 1.0
    cmp_start = np.arange(ncp) * CMP_STRIDE
    sel_start = np.arange(nsel) * SEL_LEN
    ovl = ((cmp_start[:, None] < sel_start[None, :] + SEL_LEN)
           & (cmp_start[:, None] + CMP_LEN - 1 >= sel_start[None, :])).astype(np.float32)
    ovl[ncp - 1, :] = 0.0
    eneg = np.where(np.arange(seq)[None, :] // SEL_LEN == np.arange(nsel)[:, None], NEG, 0.0).astype(np.float32)
    gw = HEADS_PER_KV * HEAD_DIM
    place = np.zeros((HEADS_PER_KV, HEAD_DIM, gw), np.float32)
    for h in range(HEADS_PER_KV):
        place[h, np.arange(HEAD_DIM), h * HEAD_DIM + np.arange(HEAD_DIM)] = 1.0
    return (jnp.asarray(pswap, BF16), jnp.asarray(ovl, BF16), jnp.asarray(eneg, BF16), jnp.asarray(place, BF16))


def _layer(x2, mod, consts, p, batch, seq):
    n, d = x2.shape
    shift1, scale1, gate1, shift2, scale2, gate2 = [mod[:, k * d:(k + 1) * d] for k in range(6)]
    (cos_q, sin_q, cos2, sin2, cos_c, sin_c, pswap, ovl, eneg, place, tri) = consts

    u, q, kcm, vcm, kv4, gl, mg = _in_projection(x2, p["norm1"], scale1, shift1, p["w_in_parts"], seq)
    y_a = _pool_mixer(u, p["w_pool"], p["pool_scale"], p["w_proj_a"], batch, seq)
    ks, vs, kw, vw = _kv_prep(kv4, p["k_gain2"], cos2, sin2, batch, seq)
    kc, vc = _compress(kcm, vcm, p["cmp_pos2"], p["cmp_wa"], p["cmp_wb"], p["cmp_b1"], p["cmp_w2"],
                       p["kc_gain"], cos_c, sin_c, batch, seq)
    att = _attention(q, gl, ks, vs, kw, vw, kc, vc, cos_q, sin_q, p["q_gain"], pswap, ovl, eneg, place,
                     batch, seq)
    x2 = _merge(x2, att, y_a, mg, gate1, p["w_proj_b"], p["w_out"], seq)

    nsub = d // LANES
    h_rows, top_e, gates, pos, counts = _router(x2, p["norm2"], scale2, shift2, p["wr_hi"], p["wr_lo"],
                                                p["router_b"], tri, seq)
    rows = EXPERT_ROWS
    n_blocks = -(-n * TOP_K // rows) + N_EXPERTS
    cnt = counts[:, 0].astype(I32)
    padded = (cnt + rows - 1) // rows * rows
    pad_end = jnp.cumsum(padded)
    pad_start = pad_end - padded
    dest = pad_start[top_e[:TOP_K]] + pos[:TOP_K]
    tt = MOVE_TILE
    dest_tiles = dest.reshape(TOP_K, n // tt, tt).transpose(1, 0, 2).reshape(n // tt, 1, TOP_K * tt)
    fill_lo = jnp.concatenate([pad_start + cnt, pad_end[-1:]])
    fill_hi = jnp.concatenate([pad_end, jnp.full((1,), n_blocks * rows, I32)])
    fill = jnp.stack([fill_lo, fill_hi], axis=1).reshape(-1).astype(I32)
    blk_e = jnp.minimum(jnp.searchsorted(pad_end, jnp.arange(n_blocks, dtype=I32) * rows, side="right"),
                        N_EXPERTS - 1).astype(I32)
    n_used = (pad_end[-1:] // rows).astype(I32)
    xs = _dispatch(h_rows, dest_tiles, fill, n_blocks * rows, nsub)
    ys = _experts(blk_e, n_used, xs, p["w_gu"], p["b_gu"], p["w_d"], p["b_d"], nsub)
    gates_t = gates.T
    return _combine(dest_tiles, ys, x2, gates_t, gate2, seq, nsub)


def _prep_layer(l, norm1, norm2, w_in, w_pool, pool_scale, q_norm, k_norm, cmp_pos, cmp_w1, cmp_b1, cmp_w2,
                w_proj_a, w_proj_b, w_out, router_w, router_b, w_gate_up, b_gate_up, w_down, b_down):
    d = w_in.shape[1]
    w = w_in[l]
    o_q = POOL_WIDTH
    o_kv = o_q + N_KV * HEADS_PER_KV * HEAD_DIM
    o_g = o_kv + 6 * LANES
    n_gate = 3 * N_KV * HEADS_PER_KV
    o_m = o_g + n_gate
    wg = w[:, o_g:o_m].reshape(d, 3, N_KV, HEADS_PER_KV).transpose(0, 2, 1, 3).reshape(d, N_KV, 3 * HEADS_PER_KV)
    wg = jnp.pad(wg, ((0, 0), (0, 0), (0, LANES - 3 * HEADS_PER_KV))).reshape(d, N_KV * LANES)
    parts = [w[:, 0:o_q], w[:, o_q:o_kv], w[:, o_kv:o_kv + LANES], w[:, o_kv + LANES:o_kv + 2 * LANES],
             w[:, o_kv + 2 * LANES:o_g], wg, w[:, o_m:]]
    w_in_parts = [a.astype(BF16) for a in parts]

    half = CMP_LEN // 2
    w1 = cmp_w1[l].reshape(2, CMP_LEN, HEAD_DIM, CMP_HIDDEN)

    def chunk_weight(wh):
        z = jnp.zeros_like(wh)
        g0 = jnp.stack([wh, z], axis=2)
        g1 = jnp.stack([z, wh], axis=2)
        return jnp.concatenate([g0, g1], axis=-1).reshape(2, half * LANES, N_KV * CMP_HIDDEN)

    cmp_wa = chunk_weight(w1[:, :half]).astype(BF16)
    cmp_wb = chunk_weight(w1[:, half:]).astype(BF16)
    pos = cmp_pos[l]
    pos_t = jnp.broadcast_to(pos[:, :, None, :], (2, CMP_LEN, N_KV, HEAD_DIM))
    cmp_pos2 = jnp.stack([pos_t[:, :half].reshape(2, half * LANES), pos_t[:, half:].reshape(2, half * LANES)], axis=1)
    b1 = jnp.tile(cmp_b1[l], (1, N_KV))[:, None, :]
    w2 = cmp_w2[l]
    z2 = jnp.zeros_like(w2)
    cmp_w2b = jnp.concatenate([jnp.concatenate([w2, z2], axis=-1), jnp.concatenate([z2, w2], axis=-1)],
                              axis=1).astype(BF16)
    wr = router_w[l].T
    wr_hi = wr.astype(BF16)
    wr_lo = (wr - wr_hi.astype(F32)).astype(BF16)
    wgu = w_gate_up[l]
    w_gu = jnp.concatenate([wgu[:, :, 0::2], wgu[:, :, 1::2]], axis=-1).astype(BF16)
    bgu = b_gate_up[l]
    b_gu = jnp.concatenate([bgu[:, 0::2], bgu[:, 1::2]], axis=-1)[:, None, :]
    return dict(
        norm1=norm1[l], norm2=norm2[l], w_in_parts=w_in_parts,
        w_pool=w_pool[l].astype(BF16), pool_scale=pool_scale[l], w_proj_a=w_proj_a[l].astype(BF16),
        k_gain2=jnp.stack([jnp.tile(k_norm[l, 1], N_KV), jnp.tile(k_norm[l, 2], N_KV)], axis=0),
        kc_gain=jnp.tile(k_norm[l, 0], N_KV)[None, :], q_gain=q_norm[l][None, :],
        cmp_pos2=cmp_pos2, cmp_wa=cmp_wa, cmp_wb=cmp_wb, cmp_b1=b1, cmp_w2=cmp_w2b,
        w_proj_b=w_proj_b[l].astype(BF16), w_out=w_out[l].astype(BF16),
        wr_hi=wr_hi, wr_lo=wr_lo, router_b=router_b[l],
        w_gu=w_gu, b_gu=b_gu, w_d=w_down[l].astype(BF16), b_d=b_down[l][:, None, :],
    )


def kernel(x, c, norm1, norm2, ada_w, ada_b, w_in, w_pool, pool_scale, q_norm, k_norm, cmp_pos, cmp_w1, cmp_b1,
           cmp_w2, w_proj_a, w_proj_b, w_out, router_w, router_b, w_gate_up, b_gate_up, w_down, b_down):
    batch, seq, d = x.shape
    depth = norm1.shape[0]
    assert seq % KV_TILE == 0 and seq % ROUTE_TILE == 0 and seq >= WINDOW + Q_TILE and d % LANES == 0
    assert (batch * seq * TOP_K) % EXPERT_ROWS == 0

    pos = np.arange(seq)
    cq, sq = _rope_tables(pos)
    cc, sc = _rope_tables(np.arange(seq // CMP_STRIDE) * CMP_STRIDE + CMP_LEN - 1)
    tile2 = lambda a: jnp.asarray(np.concatenate([a, a], axis=1))
    tri = jnp.asarray(np.triu(np.ones((ROUTE_TILE, ROUTE_TILE), np.float32), 1), BF16)
    consts = (jnp.asarray(cq), jnp.asarray(sq), tile2(cq), tile2(sq), tile2(cc), tile2(sc),
              *_attention_constants(seq), tri)

    c_pad = jnp.pad(c, ((0, SUBLANES - batch % SUBLANES if batch % SUBLANES else 0), (0, 0)))
    x2 = x.reshape(batch * seq, d)
    for l in range(depth):
        mod = _modulation(c_pad, ada_w[l], ada_b[l])[:batch]
        p = _prep_layer(l, norm1, norm2, w_in, w_pool, pool_scale, q_norm, k_norm, cmp_pos, cmp_w1, cmp_b1, cmp_w2,
                        w_proj_a, w_proj_b, w_out, router_w, router_b, w_gate_up, b_gate_up, w_down, b_down)
        x2 = _layer(x2, mod, consts, p, batch, seq)
    return x2.reshape(batch, seq, d)
```

```python
import functools

import numpy as np
import jax
import jax.numpy as jnp
from jax import lax
from jax.experimental import pallas as pl
from jax.experimental.pallas import tpu as pltpu

F32 = jnp.float32
BF16 = jnp.bfloat16
I32 = jnp.int32

POOL_WINDOWS = (2, 4, 8, 16)
POOL_GROUP = 128
POOL_WIDTH = 512
HEAD_DIM = 64
N_KV = 2
HEADS_PER_KV = 8
ROPE_DIM = 16
ROPE_HALF = 8
ROPE_THETA = 500000.0
CMP_LEN = 32
CMP_STRIDE = 16
CMP_HIDDEN = 256
SEL_LEN = 64
SEL_TOP = 16
WINDOW = 512
FORCED_SCORE = 1e9
N_EXPERTS = 32
TOP_K = 4
SWIGLU_LIMIT = 7.0
SWIGLU_ALPHA = 1.702
EPS = 1e-6
NEG = -1e30
QK_SCALE = HEAD_DIM ** -0.5 * 1.4426950408889634

LANES = 128
SUBLANES = 8
VMEM_LIMIT = 56 * 1024 * 1024

ROW_TILE = 256
POOL_HALO = 16
Q_TILE = 128
KV_TILE = 512
ROUTE_TILE = 512
EXPERT_ROWS = 512
MOVE_TILE = 128


def _cparams(sem):
    return pltpu.CompilerParams(dimension_semantics=sem, vmem_limit_bytes=VMEM_LIMIT)


def _bdot(a, b):
    return jnp.dot(a, b, preferred_element_type=F32)


def _split(a):
    hi = a.astype(BF16)
    lo = (a - hi.astype(F32)).astype(BF16)
    return hi, lo


def _sigmoid(x):
    return 1.0 / (1.0 + jnp.exp(-x))


def _mod_kernel(c_ref, w_ref, b_ref, o_ref):
    c = c_ref[...]
    a = c * _sigmoid(c)
    a_hi, a_lo = _split(a)
    w_hi, w_lo = _split(w_ref[...])
    o_ref[...] = _bdot(a_hi, w_hi) + _bdot(a_lo, w_hi) + _bdot(a_hi, w_lo) + b_ref[...]


def _modulation(c_pad, ada_w, ada_b):
    rows, d = c_pad.shape
    n = ada_w.shape[1]
    tn = 512
    return pl.pallas_call(
        _mod_kernel,
        out_shape=jax.ShapeDtypeStruct((rows, n), F32),
        grid=(n // tn,),
        in_specs=[pl.BlockSpec((rows, d), lambda i: (0, 0)),
                  pl.BlockSpec((d, tn), lambda i: (0, i)),
                  pl.BlockSpec((1, tn), lambda i: (0, i))],
        out_specs=pl.BlockSpec((rows, tn), lambda i: (0, i)),
        compiler_params=_cparams(("arbitrary",)),
        name="modulation",
    )(c_pad, ada_w, ada_b.reshape(1, n))


def _inproj_kernel(x_ref, g_ref, sc_ref, sh_ref, *refs):
    n_out = len(refs) // 2
    x = x_ref[...]
    ms = jnp.mean(x * x, axis=-1, keepdims=True)
    h = x * lax.rsqrt(ms + EPS) * g_ref[...]
    h = h * (1.0 + sc_ref[0]) + sh_ref[0]
    hb = h.astype(BF16)
    for w_ref, o_ref in zip(refs[:n_out], refs[n_out:]):
        o_ref[...] = _bdot(hb, w_ref[...])


def _in_projection(x2, gain, scale, shift, weights, seq):
    n, d = x2.shape
    tm = ROW_TILE
    tiles_per_b = seq // tm
    in_specs = [pl.BlockSpec((tm, d), lambda i: (i, 0)),
                pl.BlockSpec((1, d), lambda i: (0, 0)),
                pl.BlockSpec((1, 1, d), lambda i: (i // tiles_per_b, 0, 0)),
                pl.BlockSpec((1, 1, d), lambda i: (i // tiles_per_b, 0, 0))]
    in_specs += [pl.BlockSpec(w.shape, lambda i: (0, 0)) for w in weights]
    out_shape = [jax.ShapeDtypeStruct((n, w.shape[1]), F32) for w in weights]
    out_specs = [pl.BlockSpec((tm, w.shape[1]), lambda i: (i, 0)) for w in weights]
    return pl.pallas_call(
        _inproj_kernel,
        out_shape=out_shape,
        grid=(n // tm,),
        in_specs=in_specs,
        out_specs=out_specs,
        compiler_params=_cparams(("arbitrary",)),
        name="in_projection",
    )(x2, gain.reshape(1, d), scale[:, None, :], shift[:, None, :], *weights)


def _pool_kernel(u_ref, wp_ref, ps_ref, wa_ref, o_ref, ext_ref, *, ts):
    i = pl.program_id(1)

    @pl.when(i == 0)
    def _():
        ext_ref[0:POOL_HALO, :] = jnp.zeros((POOL_HALO, POOL_WIDTH), F32)

    @pl.when(i > 0)
    def _():
        ext_ref[0:POOL_HALO, :] = ext_ref[ts:ts + POOL_HALO, :]

    u = u_ref[...]
    ext_ref[POOL_HALO:POOL_HALO + ts, :] = u
    t = i * ts + lax.broadcasted_iota(I32, (ts, 1), 0)
    outs = []
    for gi, win in enumerate(POOL_WINDOWS):
        lo, hi = gi * POOL_GROUP, (gi + 1) * POOL_GROUP
        ug = u[:, lo:hi]
        total = ug
        for jj in range(1, win):
            total = total + ext_ref[POOL_HALO - jj:POOL_HALO - jj + ts, lo:hi]
        cnt = jnp.minimum(t + 1, win).astype(F32)
        dlt = total / cnt - ug
        outs.append(_bdot(dlt.astype(BF16), wp_ref[gi]))
    y = jnp.concatenate(outs, axis=-1) * ps_ref[...]
    o_ref[...] = _bdot(y.astype(BF16), wa_ref[...])


def _pool_mixer(u, w_pool, pool_scale, w_proj_a, batch, seq):
    n, d_out = u.shape[0], w_proj_a.shape[1]
    ts = ROW_TILE
    nt = seq // ts
    return pl.pallas_call(
        functools.partial(_pool_kernel, ts=ts),
        out_shape=jax.ShapeDtypeStruct((n, d_out), F32),
        grid=(batch, nt),
        in_specs=[pl.BlockSpec((ts, POOL_WIDTH), lambda b, i: (b * nt + i, 0)),
                  pl.BlockSpec(w_pool.shape, lambda b, i: (0, 0, 0)),
                  pl.BlockSpec((1, POOL_WIDTH), lambda b, i: (0, 0)),
                  pl.BlockSpec(w_proj_a.shape, lambda b, i: (0, 0))],
        out_specs=pl.BlockSpec((ts, d_out), lambda b, i: (b * nt + i, 0)),
        scratch_shapes=[pltpu.VMEM((ts + POOL_HALO, POOL_WIDTH), F32)],
        compiler_params=_cparams(("arbitrary", "arbitrary")),
        name="pool_mixer",
    )(u, w_pool, pool_scale.reshape(1, POOL_WIDTH), w_proj_a)


def _rope_tables(pos):
    inv_freq = np.float32(ROPE_THETA) ** (-(np.arange(ROPE_HALF, dtype=np.float32)) / np.float32(ROPE_HALF))
    ang = pos.astype(np.float32)[:, None] * inv_freq.astype(np.float32)[None, :]
    cos, sin = np.cos(ang), np.sin(ang)
    c = np.ones((pos.shape[0], HEAD_DIM), np.float32)
    s = np.zeros((pos.shape[0], HEAD_DIM), np.float32)
    c[:, :ROPE_HALF] = cos
    c[:, ROPE_HALF:ROPE_DIM] = cos
    s[:, :ROPE_HALF] = -sin
    s[:, ROPE_HALF:ROPE_DIM] = sin
    return c, s


def _norm_rope_pair(x, gain, cos, sin):
    rows = x.shape[0]
    lane = lax.broadcasted_iota(I32, (rows, LANES), 1)
    first = lane < HEAD_DIM
    x2 = x * x
    s0 = jnp.sum(jnp.where(first, x2, 0.0), axis=-1, keepdims=True)
    s1 = jnp.sum(jnp.where(first, 0.0, x2), axis=-1, keepdims=True)
    r = lax.rsqrt(jnp.where(first, s0, s1) * (1.0 / HEAD_DIM) + EPS)
    y = x * r * gain
    low = (lane & (HEAD_DIM - 1)) < ROPE_HALF
    sw = jnp.where(low, pltpu.roll(y, LANES - ROPE_HALF, 1), pltpu.roll(y, ROPE_HALF, 1))
    return y * cos + sw * sin


def _values_with_ones(v, gi):
    lane = lax.broadcasted_iota(I32, v.shape, 1)
    vg = v if gi == 0 else pltpu.roll(v, HEAD_DIM, 1)
    return jnp.where(lane < HEAD_DIM, vg, 1.0).astype(BF16)


def _kprep_kernel(kv_ref, gain_ref, cos_ref, sin_ref, ks_ref, vs_ref, kw_ref, vw_ref):
    cos, sin = cos_ref[...], sin_ref[...]
    ks = _norm_rope_pair(kv_ref[:, 0:LANES], gain_ref[0:1, :], cos, sin).T.astype(BF16)
    kw = _norm_rope_pair(kv_ref[:, 2 * LANES:3 * LANES], gain_ref[1:2, :], cos, sin).T.astype(BF16)
    vs = kv_ref[:, LANES:2 * LANES]
    vw = kv_ref[:, 3 * LANES:4 * LANES]
    for gi in range(N_KV):
        ks_ref[0, gi] = ks[gi * HEAD_DIM:(gi + 1) * HEAD_DIM, :]
        kw_ref[0, gi] = kw[gi * HEAD_DIM:(gi + 1) * HEAD_DIM, :]
        vs_ref[0, gi] = _values_with_ones(vs, gi)
        vw_ref[0, gi] = _values_with_ones(vw, gi)


def _kv_prep(kv4, gains, cos2, sin2, batch, seq):
    tm = ROW_TILE
    nt = seq // tm
    kt_shape = jax.ShapeDtypeStruct((batch, N_KV, HEAD_DIM, seq), BF16)
    v_shape = jax.ShapeDtypeStruct((batch, N_KV, seq, LANES), BF16)
    kt_spec = pl.BlockSpec((1, N_KV, HEAD_DIM, tm), lambda b, i: (b, 0, 0, i))
    v_spec = pl.BlockSpec((1, N_KV, tm, LANES), lambda b, i: (b, 0, i, 0))
    return pl.pallas_call(
        _kprep_kernel,
        out_shape=[kt_shape, v_shape, kt_shape, v_shape],
        grid=(batch, nt),
        in_specs=[pl.BlockSpec((tm, 4 * LANES), lambda b, i: (b * nt + i, 0)),
                  pl.BlockSpec((2, LANES), lambda b, i: (0, 0)),
                  pl.BlockSpec((tm, LANES), lambda b, i: (i, 0)),
                  pl.BlockSpec((tm, LANES), lambda b, i: (i, 0))],
        out_specs=[kt_spec, v_spec, kt_spec, v_spec],
        compiler_params=_cparams(("arbitrary", "arbitrary")),
        name="kv_prep",
    )(kv4, gains, cos2, sin2)


def _compress_kernel(k_ref, v_ref, pos_ref, wa_ref, wb_ref, b1_ref, w2_ref, gain_ref, cos_ref, sin_ref,
                     kc_ref, vc_ref):
    nch = k_ref.shape[0]

    def mlp(x, idx):
        xa = (x + pos_ref[idx, 0:1, :]).astype(BF16)
        xb = (x + pos_ref[idx, 1:2, :]).astype(BF16)
        a = _bdot(xa, wa_ref[idx])
        b = _bdot(xb, wb_ref[idx])
        b_next = jnp.concatenate([b[1:nch, :], jnp.zeros((1, b.shape[1]), F32)], axis=0)
        pre = a + b_next + b1_ref[idx]
        hid = pre * _sigmoid(pre)
        return _bdot(hid.astype(BF16), w2_ref[idx])

    kc = _norm_rope_pair(mlp(k_ref[...], 0), gain_ref[...], cos_ref[...], sin_ref[...]).T.astype(BF16)
    vc = mlp(v_ref[...], 1)
    for gi in range(N_KV):
        kc_ref[0, gi] = kc[gi * HEAD_DIM:(gi + 1) * HEAD_DIM, :]
        vc_ref[0, gi] = _values_with_ones(vc, gi)


def _compress(kcm, vcm, pos2, wa, wb, b1, w2, gain, cos_c, sin_c, batch, seq):
    nch = seq // CMP_STRIDE
    width = CMP_STRIDE * LANES
    full = lambda a: pl.BlockSpec(a.shape, lambda b: (0,) * a.ndim)
    return pl.pallas_call(
        _compress_kernel,
        out_shape=[jax.ShapeDtypeStruct((batch, N_KV, HEAD_DIM, nch), BF16),
                   jax.ShapeDtypeStruct((batch, N_KV, nch, LANES), BF16)],
        grid=(batch,),
        in_specs=[pl.BlockSpec((nch, width), lambda b: (b, 0)),
                  pl.BlockSpec((nch, width), lambda b: (b, 0)),
                  full(pos2), full(wa), full(wb), full(b1), full(w2), full(gain), full(cos_c), full(sin_c)],
        out_specs=[pl.BlockSpec((1, N_KV, HEAD_DIM, nch), lambda b: (b, 0, 0, 0)),
                   pl.BlockSpec((1, N_KV, nch, LANES), lambda b: (b, 0, 0, 0))],
        compiler_params=_cparams(("arbitrary",)),
        name="compress",
    )(kcm.reshape(batch * nch, width), vcm.reshape(batch * nch, width), pos2, wa, wb, b1, w2, gain, cos_c, sin_c)


def _attn_kernel(q_ref, gl_ref, ks_ref, vs_ref, kw_ref, vw_ref, kc_ref, vc_ref,
                 cos_ref, sin_ref, qg_ref, psw_ref, ovl_ref, eneg_ref, place_ref,
                 o_ref, qa_scr, m_scr, acc_scr, *, seq):
    H, QB, TK = HEADS_PER_KV, Q_TILE, KV_TILE
    q0 = pl.program_id(2) * QB
    ncp = seq // CMP_STRIDE
    nsel = seq // SEL_LEN
    n_top = min(SEL_TOP, nsel)
    nt_dims = (((1,), (1,)), ((), ()))

    def row_sum(a):
        return a[:, HEAD_DIM:HEAD_DIM + 1]

    cos, sin, psw = cos_ref[...], sin_ref[...], psw_ref[...]
    ys = []
    for h in range(H):
        x = q_ref[:, h * HEAD_DIM:(h + 1) * HEAD_DIM]
        ms = jnp.mean(x * x, axis=-1, keepdims=True)
        ys.append(x * lax.rsqrt(ms + EPS) * qg_ref[...])
    sws = []
    for h in range(H):
        y_hi, y_lo = _split(ys[h])
        sws.append(_bdot(y_hi, psw) + _bdot(y_lo, psw))
    qh = [((ys[h] * cos + sws[h] * sin) * QK_SCALE).astype(BF16) for h in range(H)]
    qpos = q0 + lax.broadcasted_iota(I32, (QB, 1), 0)

    def one_shot_branch(k_t, v_aug, mask, zero_masked):
        probs, accs = [], []
        ahead = 3
        scores = [_bdot(qh[h], k_t) for h in range(ahead)]
        for h in range(H):
            s = jnp.where(mask, scores[h], NEG)
            if h + ahead < H:
                scores.append(_bdot(qh[h + ahead], k_t))
            p = jnp.exp2(s - jnp.max(s, axis=-1, keepdims=True))
            if zero_masked:
                p = jnp.where(mask, p, 0.0)
            probs.append(p)
            accs.append(_bdot(p.astype(BF16), v_aug))
        return probs, accs

    gs = _sigmoid(gl_ref[...])
    valid = (lax.broadcasted_iota(I32, (QB, ncp), 1) * CMP_STRIDE + (CMP_LEN - 1)) <= qpos
    probs_c, accs_c = one_shot_branch(kc_ref[0, 0], vc_ref[0, 0], valid, True)
    out, ps = [], None
    for h in range(H):
        l = row_sum(accs_c[h])
        inv = jnp.where(l > 0.0, 1.0 / l, 0.0)
        out.append(accs_c[h] * (gs[:, h:h + 1] * inv))
        ps = probs_c[h] * inv if ps is None else ps + probs_c[h] * inv

    ps_hi, ps_lo = _split(ps)
    imp = (lax.dot_general(ovl_ref[...], ps_hi, nt_dims, preferred_element_type=F32)
           + lax.dot_general(ovl_ref[...], ps_lo, nt_dims, preferred_element_type=F32))
    sidx = lax.broadcasted_iota(I32, (nsel, QB), 0)
    cur = (q0 + lax.broadcasted_iota(I32, (nsel, QB), 1)) // SEL_LEN
    causal = sidx <= cur
    forced = (sidx == cur) | (sidx == 0)
    v = jnp.where(forced, FORCED_SCORE, jnp.where(causal, imp, -1.0))
    sel = jnp.zeros((nsel, QB), jnp.bool_)
    for _ in range(n_top):
        mx = jnp.max(v, axis=0, keepdims=True)
        first = jnp.min(jnp.where(v == mx, sidx, nsel), axis=0, keepdims=True)
        pick = sidx == first
        sel = sel | pick
        v = jnp.where(pick, -3e38, v)
    notsel = jnp.where(sel & causal, 0.0, 1.0).T.astype(BF16)
    for h in range(H):
        qa_scr[h] = jnp.concatenate([notsel, qh[h]], axis=1)

    m_scr[...] = jnp.full(m_scr.shape, NEG, F32)
    acc_scr[...] = jnp.zeros(acc_scr.shape, F32)

    def sel_tile(kt, causal_tile):
        off = pl.multiple_of(kt * TK, TK)
        k_aug = jnp.concatenate([eneg_ref[:, pl.ds(off, TK)], ks_ref[0, 0, :, pl.ds(off, TK)]], axis=0)
        v_t = vs_ref[0, 0, pl.ds(off, TK), :]
        if causal_tile:
            keep = (off + lax.broadcasted_iota(I32, (QB, TK), 1)) <= qpos
        ahead = 4
        scores = [_bdot(qa_scr[h], k_aug) for h in range(ahead)]
        for h in range(H):
            s = scores[h]
            if h + ahead < H:
                scores.append(_bdot(qa_scr[h + ahead], k_aug))
            if causal_tile:
                s = jnp.where(keep, s, NEG)
            m_old = m_scr[h]
            m_new = jnp.maximum(m_old, jnp.max(s, axis=-1, keepdims=True))
            alpha = jnp.exp2(m_old - m_new)
            pt = jnp.concatenate([jnp.exp2(s[:, c * LANES:(c + 1) * LANES] - m_new).astype(BF16)
                                  for c in range(TK // LANES)], axis=1)
            acc_scr[h] = alpha * acc_scr[h] + _bdot(pt, v_t)
            m_scr[h] = m_new

    n_tiles = (q0 + QB - 1) // TK + 1

    def body(kt, carry):
        sel_tile(kt, False)
        return carry

    lax.fori_loop(0, n_tiles - 1, body, 0)
    sel_tile(n_tiles - 1, True)

    span = WINDOW + QB
    base = pl.multiple_of(jnp.maximum(q0 - WINDOW, 0), QB)
    kw = kw_ref[0, 0, :, pl.ds(base, span)]
    vw = vw_ref[0, 0, pl.ds(base, span), :]
    dist = qpos - (base + lax.broadcasted_iota(I32, (QB, span), 1))
    inwin = (dist >= 0) & (dist < WINDOW)
    _, accs_w = one_shot_branch(kw, vw, inwin, False)
    heads = []
    for h in range(H):
        a_s, a_w = acc_scr[h], accs_w[h]
        o_h = (out[h] + a_s * (gs[:, H + h:H + h + 1] / row_sum(a_s))
               + a_w * (gs[:, 2 * H + h:2 * H + h + 1] / row_sum(a_w)))
        heads.append(o_h.astype(BF16))
    res = _bdot(heads[0], place_ref[0])
    for h in range(1, H):
        res = res + _bdot(heads[h], place_ref[h])
    o_ref[...] = res.astype(BF16)


def _attention(q, gl, ks, vs, kw, vw, kc, vc, cos_q, sin_q, qgain, pswap, ovl, eneg, place, batch, seq):
    n = q.shape[0]
    QB = Q_TILE
    nq = seq // QB
    H = HEADS_PER_KV
    gw = HEADS_PER_KV * HEAD_DIM
    ncp = seq // CMP_STRIDE
    nsel = seq // SEL_LEN
    const2 = lambda a: pl.BlockSpec(a.shape, lambda b, g, j: (0,) * a.ndim)
    return pl.pallas_call(
        functools.partial(_attn_kernel, seq=seq),
        out_shape=jax.ShapeDtypeStruct((n, N_KV * gw), BF16),
        grid=(batch, N_KV, nq),
        in_specs=[pl.BlockSpec((QB, gw), lambda b, g, j: (b * nq + j, g)),
                  pl.BlockSpec((QB, LANES), lambda b, g, j: (b * nq + j, g)),
                  pl.BlockSpec((1, 1, HEAD_DIM, seq), lambda b, g, j: (b, g, 0, 0)),
                  pl.BlockSpec((1, 1, seq, LANES), lambda b, g, j: (b, g, 0, 0)),
                  pl.BlockSpec((1, 1, HEAD_DIM, seq), lambda b, g, j: (b, g, 0, 0)),
                  pl.BlockSpec((1, 1, seq, LANES), lambda b, g, j: (b, g, 0, 0)),
                  pl.BlockSpec((1, 1, HEAD_DIM, ncp), lambda b, g, j: (b, g, 0, 0)),
                  pl.BlockSpec((1, 1, ncp, LANES), lambda b, g, j: (b, g, 0, 0)),
                  pl.BlockSpec((QB, HEAD_DIM), lambda b, g, j: (j, 0)),
                  pl.BlockSpec((QB, HEAD_DIM), lambda b, g, j: (j, 0)),
                  const2(qgain), const2(pswap), const2(ovl), const2(eneg), const2(place)],
        out_specs=pl.BlockSpec((QB, gw), lambda b, g, j: (b * nq + j, g)),
        scratch_shapes=[pltpu.VMEM((H, QB, nsel + HEAD_DIM), BF16), pltpu.VMEM((H, QB, LANES), F32),
                        pltpu.VMEM((H, QB, LANES), F32)],
        compiler_params=_cparams(("arbitrary", "arbitrary", "arbitrary")),
        name="sparse_attention",
    )(q, gl, ks, vs, kw, vw, kc, vc, cos_q, sin_q, qgain, pswap, ovl, eneg, place)


def _merge_kernel(x_ref, att_ref, ya_ref, mg_ref, g1_ref, wb_ref, wo_ref, o_ref):
    d = x_ref.shape[1]
    y_b = _bdot(att_ref[...], wb_ref[...])
    mg = mg_ref[...]
    merged = _sigmoid(mg[:, 0:d]) * ya_ref[...] + _sigmoid(mg[:, d:2 * d]) * y_b
    o_ref[...] = x_ref[...] + g1_ref[0] * _bdot(merged.astype(BF16), wo_ref[...])


def _merge(x2, att, y_a, mg, gate1, w_proj_b, w_out, seq):
    n, d = x2.shape
    tm = ROW_TILE
    tiles_per_b = seq // tm
    return pl.pallas_call(
        _merge_kernel,
        out_shape=jax.ShapeDtypeStruct((n, d), F32),
        grid=(n // tm,),
        in_specs=[pl.BlockSpec((tm, d), lambda i: (i, 0)),
                  pl.BlockSpec((tm, d), lambda i: (i, 0)),
                  pl.BlockSpec((tm, d), lambda i: (i, 0)),
                  pl.BlockSpec((tm, 2 * d), lambda i: (i, 0)),
                  pl.BlockSpec((1, 1, d), lambda i: (i // tiles_per_b, 0, 0)),
                  pl.BlockSpec(w_proj_b.shape, lambda i: (0, 0)),
                  pl.BlockSpec(w_out.shape, lambda i: (0, 0))],
        out_specs=pl.BlockSpec((tm, d), lambda i: (i, 0)),
        compiler_params=_cparams(("arbitrary",)),
        name="merge_out_projection",
    )(x2, att, y_a, mg, gate1[:, None, :], w_proj_b, w_out)


def _router_kernel(x_ref, g_ref, sc_ref, sh_ref, wh_ref, wl_ref, rb_ref, tri_ref,
                   h_ref, e_ref, gt_ref, pos_ref, cnt_ref, carry_ref):
    i = pl.program_id(0)
    tm, d = x_ref.shape

    @pl.when(i == 0)
    def _():
        carry_ref[...] = jnp.zeros((N_EXPERTS, 1), F32)

    x = x_ref[...]
    ms = jnp.mean(x * x, axis=-1, keepdims=True)
    h = x * lax.rsqrt(ms + EPS) * g_ref[...]
    h = h * (1.0 + sc_ref[0]) + sh_ref[0]
    for s in range(d // LANES):
        h_ref[pl.ds(s, tm, stride=SUBLANES), :] = h[:, s * LANES:(s + 1) * LANES]

    h_hi, h_lo = _split(h)
    nt = (((1,), (1,)), ((), ()))
    logits = (lax.dot_general(wh_ref[...], h_hi, nt, preferred_element_type=F32)
              + lax.dot_general(wh_ref[...], h_lo, nt, preferred_element_type=F32)
              + lax.dot_general(wl_ref[...], h_hi, nt, preferred_element_type=F32)
              + rb_ref[...])
    eidx = lax.broadcasted_iota(I32, (N_EXPERTS, tm), 0)
    onehot = jnp.zeros((N_EXPERTS, tm), F32)
    picks, vals, ids = [], [], []
    for _ in range(TOP_K):
        mx = jnp.max(logits, axis=0, keepdims=True)
        first = jnp.min(jnp.where(logits == mx, eidx, N_EXPERTS), axis=0, keepdims=True)
        pick = eidx == first
        picks.append(pick)
        vals.append(mx)
        ids.append(first)
        onehot = jnp.where(pick, 1.0, onehot)
        logits = jnp.where(pick, -3e38, logits)
    ex = [jnp.exp(vk - vals[0]) for vk in vals]
    den = ex[0] + ex[1] + ex[2] + ex[3]
    before = _bdot(onehot.astype(BF16), tri_ref[...]) + carry_ref[...]
    pad = SUBLANES - TOP_K
    pos = [jnp.sum(jnp.where(pk, before, 0.0), axis=0, keepdims=True) for pk in picks]
    e_ref[...] = jnp.concatenate(ids + [jnp.zeros((pad, tm), I32)], axis=0)
    gt_ref[...] = jnp.concatenate([e / den for e in ex] + [jnp.zeros((pad, tm), F32)], axis=0)
    pos_ref[...] = jnp.concatenate(pos + [jnp.zeros((pad, tm), F32)], axis=0).astype(I32)
    carry_ref[...] = carry_ref[...] + jnp.sum(onehot, axis=1, keepdims=True)
    cnt_ref[...] = carry_ref[...]


def _router(x2, gain, scale, shift, wr_hi, wr_lo, rb, tri, seq):
    n, d = x2.shape
    tm = ROUTE_TILE
    tiles_per_b = seq // tm
    nsub = d // LANES
    return pl.pallas_call(
        _router_kernel,
        out_shape=[jax.ShapeDtypeStruct((n * nsub, LANES), F32),
                   jax.ShapeDtypeStruct((SUBLANES, n), I32),
                   jax.ShapeDtypeStruct((SUBLANES, n), F32),
                   jax.ShapeDtypeStruct((SUBLANES, n), I32),
                   jax.ShapeDtypeStruct((N_EXPERTS, 1), F32)],
        grid=(n // tm,),
        in_specs=[pl.BlockSpec((tm, d), lambda i: (i, 0)),
                  pl.BlockSpec((1, d), lambda i: (0, 0)),
                  pl.BlockSpec((1, 1, d), lambda i: (i // tiles_per_b, 0, 0)),
                  pl.BlockSpec((1, 1, d), lambda i: (i // tiles_per_b, 0, 0)),
                  pl.BlockSpec(wr_hi.shape, lambda i: (0, 0)),
                  pl.BlockSpec(wr_lo.shape, lambda i: (0, 0)),
                  pl.BlockSpec((N_EXPERTS, 1), lambda i: (0, 0)),
                  pl.BlockSpec((tm, tm), lambda i: (0, 0))],
        out_specs=[pl.BlockSpec((tm * nsub, LANES), lambda i: (i, 0)),
                   pl.BlockSpec((SUBLANES, tm), lambda i: (0, i)),
                   pl.BlockSpec((SUBLANES, tm), lambda i: (0, i)),
                   pl.BlockSpec((SUBLANES, tm), lambda i: (0, i)),
                   pl.BlockSpec((N_EXPERTS, 1), lambda i: (0, 0))],
        scratch_shapes=[pltpu.VMEM((N_EXPERTS, 1), F32)],
        compiler_params=_cparams(("arbitrary",)),
        name="router",
    )(x2, gain.reshape(1, d), scale[:, None, :], shift[:, None, :], wr_hi, wr_lo, rb.reshape(N_EXPERTS, 1), tri)


def _dispatch_kernel(dest_ref, fill_ref, h_ref, xs_ref, zero_ref, sem, *, nsub):
    i = pl.program_id(0)
    tt = MOVE_TILE

    def row_copy(n, k):
        dst = dest_ref[0, 0, k * tt + n]
        return pltpu.make_async_copy(h_ref.at[pl.ds(n * nsub, nsub), :],
                                     xs_ref.at[pl.ds(dst * nsub, nsub), :], sem)

    def issue(n, carry):
        for k in range(TOP_K):
            row_copy(n, k).start()
        return carry

    def drain(n, carry):
        for k in range(TOP_K):
            row_copy(n, k).wait()
        return carry

    lax.fori_loop(0, tt, issue, 0)

    @pl.when(i == 0)
    def _():
        zero_ref[...] = jnp.zeros(zero_ref.shape, F32)

        def fill_copy(r):
            return pltpu.make_async_copy(zero_ref, xs_ref.at[pl.ds(r * nsub, nsub), :], sem)

        def per_expert(e, carry):
            lo, hi = fill_ref[2 * e], fill_ref[2 * e + 1]
            lax.fori_loop(lo, hi, lambda r, c: (fill_copy(r).start(), c)[1], 0)
            lax.fori_loop(lo, hi, lambda r, c: (fill_copy(r).wait(), c)[1], 0)
            return carry

        lax.fori_loop(0, fill_ref.shape[0] // 2, per_expert, 0)

    lax.fori_loop(0, tt, drain, 0)


def _dispatch(h_rows, dest_tiles, fill, n_rows, nsub):
    n_tok = h_rows.shape[0] // nsub
    tt = MOVE_TILE
    grid_spec = pltpu.PrefetchScalarGridSpec(
        num_scalar_prefetch=0,
        grid=(n_tok // tt,),
        in_specs=[pl.BlockSpec((1, 1, TOP_K * tt), lambda i: (i, 0, 0), memory_space=pltpu.SMEM),
                  pl.BlockSpec(memory_space=pltpu.SMEM),
                  pl.BlockSpec((tt * nsub, LANES), lambda i: (i, 0))],
        out_specs=pl.BlockSpec(memory_space=pl.ANY),
        scratch_shapes=[pltpu.VMEM((nsub, LANES), F32), pltpu.SemaphoreType.DMA(())],
    )
    return pl.pallas_call(
        functools.partial(_dispatch_kernel, nsub=nsub),
        out_shape=jax.ShapeDtypeStruct((n_rows * nsub, LANES), F32),
        grid_spec=grid_spec,
        compiler_params=pltpu.CompilerParams(dimension_semantics=("arbitrary",), vmem_limit_bytes=VMEM_LIMIT,
                                             has_side_effects=True),
        name="dispatch",
    )(dest_tiles, fill, h_rows)


PAIR = 2 * LANES


def _pair_permutation():
    p = np.zeros((PAIR, PAIR), np.float32)
    p[2 * np.arange(LANES), np.arange(LANES)] = 1.0
    p[2 * np.arange(LANES) + 1, LANES + np.arange(LANES)] = 1.0
    return p


def _regroup_kernel(w_ref, p_ref, o_ref):
    w = w_ref[0].astype(BF16)
    for j in range(w.shape[1] // PAIR):
        o_ref[0, :, j * PAIR:(j + 1) * PAIR] = _bdot(w[:, j * PAIR:(j + 1) * PAIR], p_ref[...]).astype(BF16)


def _regroup_gate_up(w_gate_up):
    depth, n_e, d, f2 = w_gate_up.shape
    tm = 512
    perm = jnp.asarray(_pair_permutation(), BF16)
    out = pl.pallas_call(
        _regroup_kernel,
        out_shape=jax.ShapeDtypeStruct((depth * n_e, d, f2), BF16),
        grid=(depth * n_e, d // tm),
        in_specs=[pl.BlockSpec((1, tm, f2), lambda e, i: (e, i, 0)),
                  pl.BlockSpec((PAIR, PAIR), lambda e, i: (0, 0))],
        out_specs=pl.BlockSpec((1, tm, f2), lambda e, i: (e, i, 0)),
        compiler_params=_cparams(("arbitrary", "arbitrary")),
        name="regroup_gate_up",
    )(w_gate_up.reshape(depth * n_e, d, f2), perm)
    return out.reshape(depth, n_e, d, f2)


def _regroup_bias(b):
    lead = b.shape[:-1]
    return b.reshape(*lead, -1, LANES, 2).swapaxes(-1, -2).reshape(*lead, -1)


def _expert_kernel(be_ref, nu_ref, xs_ref, wgu_ref, bgu_ref, wd_ref, bd_ref, ys_ref, *, nsub):
    i = pl.program_id(0)
    rows = EXPERT_ROWS
    f = wd_ref.shape[1]

    @pl.when(i < nu_ref[0])
    def _():
        x = jnp.concatenate([xs_ref[pl.ds(s, rows, stride=nsub), :] for s in range(nsub)], axis=-1)
        gu = _bdot(x.astype(BF16), wgu_ref[0]) + bgu_ref[0]
        acts = []
        for j in range(f // LANES):
            g_lin = jnp.minimum(gu[:, j * PAIR:j * PAIR + LANES], SWIGLU_LIMIT)
            u_lin = jnp.clip(gu[:, j * PAIR + LANES:(j + 1) * PAIR], -SWIGLU_LIMIT, SWIGLU_LIMIT)
            acts.append(((u_lin + 1.0) * g_lin * _sigmoid(g_lin * SWIGLU_ALPHA)).astype(BF16))
        act = jnp.concatenate(acts, axis=-1)
        y = _bdot(act, wd_ref[0]) + bd_ref[0]
        for s in range(nsub):
            ys_ref[pl.ds(s, rows, stride=nsub), :] = y[:, s * LANES:(s + 1) * LANES]

    @pl.when(i >= nu_ref[0])
    def _():
        ys_ref[...] = jnp.zeros(ys_ref.shape, F32)


def _experts(blk_e, n_used, xs, w_gu, b_gu, w_d, b_d, nsub):
    rows = EXPERT_ROWS
    n_blocks = xs.shape[0] // (rows * nsub)
    d, f2 = w_gu.shape[1], w_gu.shape[2]
    f = w_d.shape[1]
    row_map = lambda i, be, nu: (jnp.minimum(i, nu[0] - 1), 0)
    grid_spec = pltpu.PrefetchScalarGridSpec(
        num_scalar_prefetch=2,
        grid=(n_blocks,),
        in_specs=[pl.BlockSpec((rows * nsub, LANES), row_map),
                  pl.BlockSpec((1, d, f2), lambda i, be, nu: (be[i], 0, 0)),
                  pl.BlockSpec((1, 1, f2), lambda i, be, nu: (be[i], 0, 0)),
                  pl.BlockSpec((1, f, d), lambda i, be, nu: (be[i], 0, 0)),
                  pl.BlockSpec((1, 1, d), lambda i, be, nu: (be[i], 0, 0))],
        out_specs=pl.BlockSpec((rows * nsub, LANES), lambda i, be, nu: (i, 0)),
    )
    return pl.pallas_call(
        functools.partial(_expert_kernel, nsub=nsub),
        out_shape=jax.ShapeDtypeStruct(xs.shape, F32),
        grid_spec=grid_spec,
        compiler_params=_cparams(("arbitrary",)),
        name="experts",
    )(blk_e, n_used, xs, w_gu, b_gu, w_d, b_d)


def _combine_kernel(dest_ref, ys_ref, x_ref, gt_ref, g2_ref, o_ref, buf_ref, sem, *, nsub):
    tt = MOVE_TILE

    def row_copy(n, k):
        src = dest_ref[0, 0, k * tt + n]
        return pltpu.make_async_copy(ys_ref.at[pl.ds(src * nsub, nsub), :],
                                     buf_ref.at[k, pl.ds(n * nsub, nsub), :], sem)

    def issue(n, carry):
        for k in range(TOP_K):
            row_copy(n, k).start()
        return carry

    def drain(n, carry):
        for k in range(TOP_K):
            row_copy(n, k).wait()
        return carry

    lax.fori_loop(0, tt, issue, 0)
    lax.fori_loop(0, tt, drain, 0)
    gt = gt_ref[...]
    acc = None
    for k in range(TOP_K):
        yk = jnp.concatenate([buf_ref[k, pl.ds(s, tt, stride=nsub), :] for s in range(nsub)], axis=-1)
        term = gt[:, k:k + 1] * yk
        acc = term if acc is None else acc + term
    o_ref[...] = x_ref[...] + g2_ref[0] * acc


def _combine(dest_tiles, ys, x2, gates_t, gate2, seq, nsub):
    n, d = x2.shape
    tt = MOVE_TILE
    tiles_per_b = seq // tt
    grid_spec = pltpu.PrefetchScalarGridSpec(
        num_scalar_prefetch=0,
        grid=(n // tt,),
        in_specs=[pl.BlockSpec((1, 1, TOP_K * tt), lambda i: (i, 0, 0), memory_space=pltpu.SMEM),
                  pl.BlockSpec(memory_space=pl.ANY),
                  pl.BlockSpec((tt, d), lambda i: (i, 0)),
                  pl.BlockSpec((tt, SUBLANES), lambda i: (i, 0)),
                  pl.BlockSpec((1, 1, d), lambda i: (i // tiles_per_b, 0, 0))],
        out_specs=pl.BlockSpec((tt, d), lambda i: (i, 0)),
        scratch_shapes=[pltpu.VMEM((TOP_K, tt * nsub, LANES), F32), pltpu.SemaphoreType.DMA(())],
    )
    return pl.pallas_call(
        functools.partial(_combine_kernel, nsub=nsub),
        out_shape=jax.ShapeDtypeStruct((n, d), F32),
        grid_spec=grid_spec,
        compiler_params=_cparams(("arbitrary",)),
        name="combine",
    )(dest_tiles, ys, x2, gates_t, gate2[:, None, :])


def _attention_constants(seq):
    nsel = seq // SEL_LEN
    ncp = seq // CMP_STRIDE
    pswap = np.zeros((HEAD_DIM, HEAD_DIM), np.float32)
    for dd in range(ROPE_HALF):
        pswap[dd + ROPE_HALF, dd] = 1.0
        pswap[dd, dd + ROPE_HALF] = 1.0
    cmp_start = np.arange(ncp) * CMP_STRIDE
    sel_start = np.arange(nsel) * SEL_LEN
    ovl = ((cmp_start[:, None] < sel_start[None, :] + SEL_LEN)
           & (cmp_start[:, None] + CMP_LEN - 1 >= sel_start[None, :])).astype(np.float32)
    ovl[ncp - 1, :] = 0.0
    eneg = np.where(np.arange(seq)[None, :] // SEL_LEN == np.arange(nsel)[:, None], NEG, 0.0).astype(np.float32)
    gw = HEADS_PER_KV * HEAD_DIM
    place = np.zeros((HEADS_PER_KV, LANES, gw), np.float32)
    for h in range(HEADS_PER_KV):
        place[h, np.arange(HEAD_DIM), h * HEAD_DIM + np.arange(HEAD_DIM)] = 1.0
    return (jnp.asarray(pswap, BF16), jnp.asarray(ovl.T, BF16), jnp.asarray(eneg, BF16), jnp.asarray(place, BF16))


def _layer(x2, mod, consts, p, batch, seq):
    n, d = x2.shape
    shift1, scale1, gate1, shift2, scale2, gate2 = [mod[:, k * d:(k + 1) * d] for k in range(6)]
    (cos_q, sin_q, cos2, sin2, cos_c, sin_c, pswap, ovl, eneg, place, tri) = consts

    u, q, kcm, vcm, kv4, gl, mg = _in_projection(x2, p["norm1"], scale1, shift1, p["w_in_parts"], seq)
    y_a = _pool_mixer(u, p["w_pool"], p["pool_scale"], p["w_proj_a"], batch, seq)
    ks, vs, kw, vw = _kv_prep(kv4, p["k_gain2"], cos2, sin2, batch, seq)
    kc, vc = _compress(kcm, vcm, p["cmp_pos2"], p["cmp_wa"], p["cmp_wb"], p["cmp_b1"], p["cmp_w2"],
                       p["kc_gain"], cos_c, sin_c, batch, seq)
    att = _attention(q, gl, ks, vs, kw, vw, kc, vc, cos_q, sin_q, p["q_gain"], pswap, ovl, eneg, place,
                     batch, seq)
    x2 = _merge(x2, att, y_a, mg, gate1, p["w_proj_b"], p["w_out"], seq)

    nsub = d // LANES
    h_rows, top_e, gates, pos, counts = _router(x2, p["norm2"], scale2, shift2, p["wr_hi"], p["wr_lo"],
                                                p["router_b"], tri, seq)
    rows = EXPERT_ROWS
    n_blocks = -(-n * TOP_K // rows) + N_EXPERTS
    cnt = counts[:, 0].astype(I32)
    padded = (cnt + rows - 1) // rows * rows
    pad_end = jnp.cumsum(padded)
    pad_start = pad_end - padded
    e_ids = jnp.arange(N_EXPERTS, dtype=I32)
    dest = jnp.sum(jnp.where(top_e[:TOP_K, :, None] == e_ids, pad_start, 0), axis=-1) + pos[:TOP_K]
    tt = MOVE_TILE
    dest_tiles = dest.reshape(TOP_K, n // tt, tt).transpose(1, 0, 2).reshape(n // tt, 1, TOP_K * tt)
    fill_lo = jnp.concatenate([pad_start + cnt, pad_end[-1:]])
    fill_hi = jnp.concatenate([pad_end, jnp.full((1,), n_blocks * rows, I32)])
    fill = jnp.stack([fill_lo, fill_hi], axis=1).reshape(-1).astype(I32)
    blk_first = jnp.arange(n_blocks, dtype=I32)[:, None] * rows
    blk_e = jnp.minimum(jnp.sum((pad_end[None, :] <= blk_first).astype(I32), axis=1), N_EXPERTS - 1)
    n_used = (pad_end[-1:] // rows).astype(I32)
    xs = _dispatch(h_rows, dest_tiles, fill, n_blocks * rows, nsub)
    ys = _experts(blk_e, n_used, xs, p["w_gu"], p["b_gu"], p["w_d"], p["b_d"], nsub)
    gates_t = gates.T
    return _combine(dest_tiles, ys, x2, gates_t, gate2, seq, nsub)


def _prep_layer(l, norm1, norm2, w_in, w_pool, pool_scale, q_norm, k_norm, cmp_pos, cmp_w1, cmp_b1, cmp_w2,
                w_proj_a, w_proj_b, w_out, router_w, router_b, w_gate_up, b_gate_up, w_down, b_down):
    d = w_in.shape[1]
    w = w_in[l]
    o_q = POOL_WIDTH
    o_kv = o_q + N_KV * HEADS_PER_KV * HEAD_DIM
    o_g = o_kv + 6 * LANES
    n_gate = 3 * N_KV * HEADS_PER_KV
    o_m = o_g + n_gate
    wg = w[:, o_g:o_m].reshape(d, 3, N_KV, HEADS_PER_KV).transpose(0, 2, 1, 3).reshape(d, N_KV, 3 * HEADS_PER_KV)
    wg = jnp.pad(wg, ((0, 0), (0, 0), (0, LANES - 3 * HEADS_PER_KV))).reshape(d, N_KV * LANES)
    parts = [w[:, 0:o_q], w[:, o_q:o_kv], w[:, o_kv:o_kv + LANES], w[:, o_kv + LANES:o_kv + 2 * LANES],
             w[:, o_kv + 2 * LANES:o_g], wg, w[:, o_m:]]
    w_in_parts = [a.astype(BF16) for a in parts]

    half = CMP_LEN // 2
    w1 = cmp_w1[l].reshape(2, CMP_LEN, HEAD_DIM, CMP_HIDDEN)

    def chunk_weight(wh):
        z = jnp.zeros_like(wh)
        g0 = jnp.stack([wh, z], axis=2)
        g1 = jnp.stack([z, wh], axis=2)
        return jnp.concatenate([g0, g1], axis=-1).reshape(2, half * LANES, N_KV * CMP_HIDDEN)

    cmp_wa = chunk_weight(w1[:, :half]).astype(BF16)
    cmp_wb = chunk_weight(w1[:, half:]).astype(BF16)
    pos = cmp_pos[l]
    pos_t = jnp.broadcast_to(pos[:, :, None, :], (2, CMP_LEN, N_KV, HEAD_DIM))
    cmp_pos2 = jnp.stack([pos_t[:, :half].reshape(2, half * LANES), pos_t[:, half:].reshape(2, half * LANES)], axis=1)
    b1 = jnp.tile(cmp_b1[l], (1, N_KV))[:, None, :]
    w2 = cmp_w2[l]
    z2 = jnp.zeros_like(w2)
    cmp_w2b = jnp.concatenate([jnp.concatenate([w2, z2], axis=-1), jnp.concatenate([z2, w2], axis=-1)],
                              axis=1).astype(BF16)
    wr = router_w[l].T
    wr_hi = wr.astype(BF16)
    wr_lo = (wr - wr_hi.astype(F32)).astype(BF16)
    w_gu = w_gate_up[l]
    b_gu = _regroup_bias(b_gate_up[l])[:, None, :]
    return dict(
        norm1=norm1[l], norm2=norm2[l], w_in_parts=w_in_parts,
        w_pool=w_pool[l].astype(BF16), pool_scale=pool_scale[l], w_proj_a=w_proj_a[l].astype(BF16),
        k_gain2=jnp.stack([jnp.tile(k_norm[l, 1], N_KV), jnp.tile(k_norm[l, 2], N_KV)], axis=0),
        kc_gain=jnp.tile(k_norm[l, 0], N_KV)[None, :], q_gain=q_norm[l][None, :],
        cmp_pos2=cmp_pos2, cmp_wa=cmp_wa, cmp_wb=cmp_wb, cmp_b1=b1, cmp_w2=cmp_w2b,
        w_proj_b=w_proj_b[l].astype(BF16), w_out=w_out[l].astype(BF16),
        wr_hi=wr_hi, wr_lo=wr_lo, router_b=router_b[l],
        w_gu=w_gu, b_gu=b_gu, w_d=w_down[l].astype(BF16), b_d=b_down[l][:, None, :],
    )


def kernel(x, c, norm1, norm2, ada_w, ada_b, w_in, w_pool, pool_scale, q_norm, k_norm, cmp_pos, cmp_w1, cmp_b1,
           cmp_w2, w_proj_a, w_proj_b, w_out, router_w, router_b, w_gate_up, b_gate_up, w_down, b_down):
    batch, seq, d = x.shape
    depth = norm1.shape[0]
    assert seq % KV_TILE == 0 and seq % ROUTE_TILE == 0 and seq >= WINDOW + Q_TILE and d % LANES == 0
    assert (batch * seq * TOP_K) % EXPERT_ROWS == 0

    pos = np.arange(seq)
    cq, sq = _rope_tables(pos)
    cc, sc = _rope_tables(np.arange(seq // CMP_STRIDE) * CMP_STRIDE + CMP_LEN - 1)
    tile2 = lambda a: jnp.asarray(np.concatenate([a, a], axis=1))
    tri = jnp.asarray(np.triu(np.ones((ROUTE_TILE, ROUTE_TILE), np.float32), 1), BF16)
    consts = (jnp.asarray(cq), jnp.asarray(sq), tile2(cq), tile2(sq), tile2(cc), tile2(sc),
              *_attention_constants(seq), tri)

    c_pad = jnp.pad(c, ((0, SUBLANES - batch % SUBLANES if batch % SUBLANES else 0), (0, 0)))
    x2 = x.reshape(batch * seq, d)
    w_gate_up = _regroup_gate_up(w_gate_up)
    for l in range(depth):
        mod = _modulation(c_pad, ada_w[l], ada_b[l])[:batch]
        p = _prep_layer(l, norm1, norm2, w_in, w_pool, pool_scale, q_norm, k_norm, cmp_pos, cmp_w1, cmp_b1, cmp_w2,
                        w_proj_a, w_proj_b, w_out, router_w, router_b, w_gate_up, b_gate_up, w_down, b_down)
        x2 = _layer(x2, mod, consts, p, batch, seq)
    return x2.reshape(batch, seq, d)
```

```python
import functools

import numpy as np
import jax
import jax.numpy as jnp
from jax import lax
from jax.experimental import pallas as pl
from jax.experimental.pallas import tpu as pltpu

F32 = jnp.float32
BF16 = jnp.bfloat16
I32 = jnp.int32

POOL_WINDOWS = (2, 4, 8, 16)
POOL_GROUP = 128
POOL_WIDTH = 512
HEAD_DIM = 64
N_KV = 2
HEADS_PER_KV = 8
ROPE_DIM = 16
ROPE_HALF = 8
ROPE_THETA = 500000.0
CMP_LEN = 32
CMP_STRIDE = 16
CMP_HIDDEN = 256
SEL_LEN = 64
SEL_TOP = 16
WINDOW = 512
FORCED_SCORE = 1e9
N_EXPERTS = 32
TOP_K = 4
SWIGLU_LIMIT = 7.0
SWIGLU_ALPHA = 1.702
EPS = 1e-6
NEG = -1e30
QK_SCALE = HEAD_DIM ** -0.5 * 1.4426950408889634

LANES = 128
SUBLANES = 8
VMEM_LIMIT = 56 * 1024 * 1024

ROW_TILE = 256
POOL_HALO = 16
Q_TILE = 256
KV_TILE = 512
ROUTE_TILE = 512
EXPERT_ROWS = 512
MOVE_TILE = 128


def _cparams(sem):
    return pltpu.CompilerParams(dimension_semantics=sem, vmem_limit_bytes=VMEM_LIMIT)


def _bdot(a, b):
    return jnp.dot(a, b, preferred_element_type=F32)


def _split(a):
    hi = a.astype(BF16)
    lo = (a - hi.astype(F32)).astype(BF16)
    return hi, lo


def _sigmoid(x):
    return 1.0 / (1.0 + jnp.exp(-x))


def _mod_kernel(c_ref, w_ref, b_ref, o_ref):
    c = c_ref[...]
    a = c * _sigmoid(c)
    a_hi, a_lo = _split(a)
    w_hi, w_lo = _split(w_ref[...])
    o_ref[...] = _bdot(a_hi, w_hi) + _bdot(a_lo, w_hi) + _bdot(a_hi, w_lo) + b_ref[...]


def _modulation(c_pad, ada_w, ada_b):
    rows, d = c_pad.shape
    n = ada_w.shape[1]
    tn = 512
    return pl.pallas_call(
        _mod_kernel,
        out_shape=jax.ShapeDtypeStruct((rows, n), F32),
        grid=(n // tn,),
        in_specs=[pl.BlockSpec((rows, d), lambda i: (0, 0)),
                  pl.BlockSpec((d, tn), lambda i: (0, i)),
                  pl.BlockSpec((1, tn), lambda i: (0, i))],
        out_specs=pl.BlockSpec((rows, tn), lambda i: (0, i)),
        compiler_params=_cparams(("arbitrary",)),
        name="modulation",
    )(c_pad, ada_w, ada_b.reshape(1, n))


def _inproj_kernel(x_ref, g_ref, sc_ref, sh_ref, qg_ref, cos_ref, sin_ref, *refs, q_index):
    n_out = len(refs) // 2
    x = x_ref[...]
    ms = jnp.mean(x * x, axis=-1, keepdims=True)
    h = x * lax.rsqrt(ms + EPS) * g_ref[...]
    h = h * (1.0 + sc_ref[0]) + sh_ref[0]
    hb = h.astype(BF16)
    for idx, (w_ref, o_ref) in enumerate(zip(refs[:n_out], refs[n_out:])):
        y = _bdot(hb, w_ref[...])
        if idx == q_index:
            cos, sin = cos_ref[...], sin_ref[...]
            for j in range(y.shape[1] // LANES):
                q2 = _norm_rope_pair(y[:, j * LANES:(j + 1) * LANES], qg_ref[...], cos, sin)
                o_ref[:, j * LANES:(j + 1) * LANES] = (q2 * QK_SCALE).astype(BF16)
        else:
            o_ref[...] = y


def _in_projection(x2, gain, scale, shift, q_gain2, cos2, sin2, weights, seq, q_index):
    n, d = x2.shape
    tm = ROW_TILE
    tiles_per_b = seq // tm
    in_specs = [pl.BlockSpec((tm, d), lambda i: (i, 0)),
                pl.BlockSpec((1, d), lambda i: (0, 0)),
                pl.BlockSpec((1, 1, d), lambda i: (i // tiles_per_b, 0, 0)),
                pl.BlockSpec((1, 1, d), lambda i: (i // tiles_per_b, 0, 0)),
                pl.BlockSpec((1, LANES), lambda i: (0, 0)),
                pl.BlockSpec((tm, LANES), lambda i: (i % tiles_per_b, 0)),
                pl.BlockSpec((tm, LANES), lambda i: (i % tiles_per_b, 0))]
    in_specs += [pl.BlockSpec(w.shape, lambda i: (0, 0)) for w in weights]
    out_shape = [jax.ShapeDtypeStruct((n, w.shape[1]), BF16 if k == q_index else F32)
                 for k, w in enumerate(weights)]
    out_specs = [pl.BlockSpec((tm, w.shape[1]), lambda i: (i, 0)) for w in weights]
    return pl.pallas_call(
        functools.partial(_inproj_kernel, q_index=q_index),
        out_shape=out_shape,
        grid=(n // tm,),
        in_specs=in_specs,
        out_specs=out_specs,
        compiler_params=_cparams(("arbitrary",)),
        name="in_projection",
    )(x2, gain.reshape(1, d), scale[:, None, :], shift[:, None, :], q_gain2, cos2, sin2, *weights)


def _pool_kernel(u_ref, wp_ref, ps_ref, wa_ref, o_ref, ext_ref, *, ts):
    i = pl.program_id(1)

    @pl.when(i == 0)
    def _():
        ext_ref[0:POOL_HALO, :] = jnp.zeros((POOL_HALO, POOL_WIDTH), F32)

    @pl.when(i > 0)
    def _():
        ext_ref[0:POOL_HALO, :] = ext_ref[ts:ts + POOL_HALO, :]

    u = u_ref[...]
    ext_ref[POOL_HALO:POOL_HALO + ts, :] = u
    t = i * ts + lax.broadcasted_iota(I32, (ts, 1), 0)
    outs = []
    for gi, win in enumerate(POOL_WINDOWS):
        lo, hi = gi * POOL_GROUP, (gi + 1) * POOL_GROUP
        ug = u[:, lo:hi]
        total = ug
        for jj in range(1, win):
            total = total + ext_ref[POOL_HALO - jj:POOL_HALO - jj + ts, lo:hi]
        cnt = jnp.minimum(t + 1, win).astype(F32)
        dlt = total / cnt - ug
        outs.append(_bdot(dlt.astype(BF16), wp_ref[gi]))
    y = jnp.concatenate(outs, axis=-1) * ps_ref[...]
    o_ref[...] = _bdot(y.astype(BF16), wa_ref[...])


def _pool_mixer(u, w_pool, pool_scale, w_proj_a, batch, seq):
    n, d_out = u.shape[0], w_proj_a.shape[1]
    ts = ROW_TILE
    nt = seq // ts
    return pl.pallas_call(
        functools.partial(_pool_kernel, ts=ts),
        out_shape=jax.ShapeDtypeStruct((n, d_out), F32),
        grid=(batch, nt),
        in_specs=[pl.BlockSpec((ts, POOL_WIDTH), lambda b, i: (b * nt + i, 0)),
                  pl.BlockSpec(w_pool.shape, lambda b, i: (0, 0, 0)),
                  pl.BlockSpec((1, POOL_WIDTH), lambda b, i: (0, 0)),
                  pl.BlockSpec(w_proj_a.shape, lambda b, i: (0, 0))],
        out_specs=pl.BlockSpec((ts, d_out), lambda b, i: (b * nt + i, 0)),
        scratch_shapes=[pltpu.VMEM((ts + POOL_HALO, POOL_WIDTH), F32)],
        compiler_params=_cparams(("arbitrary", "arbitrary")),
        name="pool_mixer",
    )(u, w_pool, pool_scale.reshape(1, POOL_WIDTH), w_proj_a)


def _rope_tables(pos):
    inv_freq = np.float32(ROPE_THETA) ** (-(np.arange(ROPE_HALF, dtype=np.float32)) / np.float32(ROPE_HALF))
    ang = pos.astype(np.float32)[:, None] * inv_freq.astype(np.float32)[None, :]
    cos, sin = np.cos(ang), np.sin(ang)
    c = np.ones((pos.shape[0], HEAD_DIM), np.float32)
    s = np.zeros((pos.shape[0], HEAD_DIM), np.float32)
    c[:, :ROPE_HALF] = cos
    c[:, ROPE_HALF:ROPE_DIM] = cos
    s[:, :ROPE_HALF] = -sin
    s[:, ROPE_HALF:ROPE_DIM] = sin
    return c, s


def _norm_rope_pair(x, gain, cos, sin):
    rows = x.shape[0]
    lane = lax.broadcasted_iota(I32, (rows, LANES), 1)
    first = lane < HEAD_DIM
    x2 = x * x
    s0 = jnp.sum(jnp.where(first, x2, 0.0), axis=-1, keepdims=True)
    s1 = jnp.sum(jnp.where(first, 0.0, x2), axis=-1, keepdims=True)
    r = lax.rsqrt(jnp.where(first, s0, s1) * (1.0 / HEAD_DIM) + EPS)
    y = x * r * gain
    low = (lane & (HEAD_DIM - 1)) < ROPE_HALF
    sw = jnp.where(low, pltpu.roll(y, LANES - ROPE_HALF, 1), pltpu.roll(y, ROPE_HALF, 1))
    return y * cos + sw * sin


def _values_with_ones(v, gi):
    lane = lax.broadcasted_iota(I32, v.shape, 1)
    vg = v if gi == 0 else pltpu.roll(v, HEAD_DIM, 1)
    return jnp.where(lane < HEAD_DIM, vg, 1.0).astype(BF16)


def _kprep_kernel(kv_ref, gain_ref, cos_ref, sin_ref, ks_ref, vs_ref, kw_ref, vw_ref):
    cos, sin = cos_ref[...], sin_ref[...]
    ks = _norm_rope_pair(kv_ref[:, 0:LANES], gain_ref[0:1, :], cos, sin).T.astype(BF16)
    kw = _norm_rope_pair(kv_ref[:, 2 * LANES:3 * LANES], gain_ref[1:2, :], cos, sin).T.astype(BF16)
    vs = kv_ref[:, LANES:2 * LANES]
    vw = kv_ref[:, 3 * LANES:4 * LANES]
    for gi in range(N_KV):
        ks_ref[0, gi] = ks[gi * HEAD_DIM:(gi + 1) * HEAD_DIM, :]
        kw_ref[0, gi] = kw[gi * HEAD_DIM:(gi + 1) * HEAD_DIM, :]
        vs_ref[0, gi] = _values_with_ones(vs, gi)
        vw_ref[0, gi] = _values_with_ones(vw, gi)


def _kv_prep(kv4, gains, cos2, sin2, batch, seq):
    tm = ROW_TILE
    nt = seq // tm
    kt_shape = jax.ShapeDtypeStruct((batch, N_KV, HEAD_DIM, seq), BF16)
    v_shape = jax.ShapeDtypeStruct((batch, N_KV, seq, LANES), BF16)
    kt_spec = pl.BlockSpec((1, N_KV, HEAD_DIM, tm), lambda b, i: (b, 0, 0, i))
    v_spec = pl.BlockSpec((1, N_KV, tm, LANES), lambda b, i: (b, 0, i, 0))
    return pl.pallas_call(
        _kprep_kernel,
        out_shape=[kt_shape, v_shape, kt_shape, v_shape],
        grid=(batch, nt),
        in_specs=[pl.BlockSpec((tm, 4 * LANES), lambda b, i: (b * nt + i, 0)),
                  pl.BlockSpec((2, LANES), lambda b, i: (0, 0)),
                  pl.BlockSpec((tm, LANES), lambda b, i: (i, 0)),
                  pl.BlockSpec((tm, LANES), lambda b, i: (i, 0))],
        out_specs=[kt_spec, v_spec, kt_spec, v_spec],
        compiler_params=_cparams(("arbitrary", "arbitrary")),
        name="kv_prep",
    )(kv4, gains, cos2, sin2)


def _compress_kernel(k_ref, v_ref, pos_ref, wa_ref, wb_ref, b1_ref, w2_ref, gain_ref, cos_ref, sin_ref,
                     kc_ref, vc_ref):
    nch = k_ref.shape[0]

    def mlp(x, idx):
        xa = (x + pos_ref[idx, 0:1, :]).astype(BF16)
        xb = (x + pos_ref[idx, 1:2, :]).astype(BF16)
        a = _bdot(xa, wa_ref[idx])
        b = _bdot(xb, wb_ref[idx])
        b_next = jnp.concatenate([b[1:nch, :], jnp.zeros((1, b.shape[1]), F32)], axis=0)
        pre = a + b_next + b1_ref[idx]
        hid = pre * _sigmoid(pre)
        return _bdot(hid.astype(BF16), w2_ref[idx])

    kc = _norm_rope_pair(mlp(k_ref[...], 0), gain_ref[...], cos_ref[...], sin_ref[...]).T.astype(BF16)
    vc = mlp(v_ref[...], 1)
    for gi in range(N_KV):
        kc_ref[0, gi] = kc[gi * HEAD_DIM:(gi + 1) * HEAD_DIM, :]
        vc_ref[0, gi] = _values_with_ones(vc, gi)


def _compress(kcm, vcm, pos2, wa, wb, b1, w2, gain, cos_c, sin_c, batch, seq):
    nch = seq // CMP_STRIDE
    width = CMP_STRIDE * LANES
    full = lambda a: pl.BlockSpec(a.shape, lambda b: (0,) * a.ndim)
    return pl.pallas_call(
        _compress_kernel,
        out_shape=[jax.ShapeDtypeStruct((batch, N_KV, HEAD_DIM, nch), BF16),
                   jax.ShapeDtypeStruct((batch, N_KV, nch, LANES), BF16)],
        grid=(batch,),
        in_specs=[pl.BlockSpec((nch, width), lambda b: (b, 0)),
                  pl.BlockSpec((nch, width), lambda b: (b, 0)),
                  full(pos2), full(wa), full(wb), full(b1), full(w2), full(gain), full(cos_c), full(sin_c)],
        out_specs=[pl.BlockSpec((1, N_KV, HEAD_DIM, nch), lambda b: (b, 0, 0, 0)),
                   pl.BlockSpec((1, N_KV, nch, LANES), lambda b: (b, 0, 0, 0))],
        compiler_params=_cparams(("arbitrary",)),
        name="compress",
    )(kcm.reshape(batch * nch, width), vcm.reshape(batch * nch, width), pos2, wa, wb, b1, w2, gain, cos_c, sin_c)


def _attn_kernel(q_ref, gl_ref, ks_ref, vs_ref, kw_ref, vw_ref, kc_ref, vc_ref, ovl_ref, eneg_ref,
                 o_ref, qa_scr, pre_scr, m_scr, acc_scr, *, seq):
    H, QB, TK = HEADS_PER_KV, Q_TILE, KV_TILE
    q0 = pl.program_id(2) * QB
    ncp = seq // CMP_STRIDE
    nsel = seq // SEL_LEN
    n_top = min(SEL_TOP, nsel)
    nt_dims = (((1,), (1,)), ((), ()))

    def row_sum(a):
        return a[:, HEAD_DIM:HEAD_DIM + 1]

    def pair_keys(k_t, lead=None):
        z = jnp.zeros_like(k_t)
        head = [] if lead is None else [lead]
        return (jnp.concatenate(head + [k_t, z], axis=0), jnp.concatenate(head + [z, k_t], axis=0))

    q_pairs = [q_ref[:, j * LANES:(j + 1) * LANES] for j in range(H // 2)]
    qpos = q0 + lax.broadcasted_iota(I32, (QB, 1), 0)

    def one_shot_branch(k_t, v_aug, mask, zero_masked, between=None):
        k_var = pair_keys(k_t)
        score = lambda h: _bdot(q_pairs[h // 2], k_var[h % 2])
        probs, accs = [], []
        ahead = 3
        scores = [score(h) for h in range(ahead)]
        for h in range(H):
            s = jnp.where(mask, scores[h], NEG)
            if h + ahead < H:
                scores.append(score(h + ahead))
            p = jnp.exp2(s - jnp.max(s, axis=-1, keepdims=True))
            if zero_masked:
                p = jnp.where(mask, p, 0.0)
            probs.append(p)
            accs.append(_bdot(p.astype(BF16), v_aug))
            if between is not None:
                between(h, probs, accs)
        return accs

    gs = _sigmoid(gl_ref[...])
    valid = (lax.broadcasted_iota(I32, (QB, ncp), 1) * CMP_STRIDE + (CMP_LEN - 1)) <= qpos
    imp_state = {"ps": None}

    def finish_cmp_head(h, probs, accs):
        l = row_sum(accs[h])
        inv = jnp.where(l > 0.0, 1.0 / l, 0.0)
        pre_scr[h] = accs[h] * (gs[:, h:h + 1] * inv)
        term = probs[h] * inv
        imp_state["ps"] = term if imp_state["ps"] is None else imp_state["ps"] + term
        probs[h] = None

    def after_cmp_head(h, probs, accs):
        if h > 0:
            finish_cmp_head(h - 1, probs, accs)
        if h == H - 1:
            finish_cmp_head(h, probs, accs)

    one_shot_branch(kc_ref[0, 0], vc_ref[0, 0], valid, True, between=after_cmp_head)
    ps = imp_state["ps"]

    ps_hi, ps_lo = _split(ps)
    imp = (lax.dot_general(ovl_ref[...], ps_hi, nt_dims, preferred_element_type=F32)
           + lax.dot_general(ovl_ref[...], ps_lo, nt_dims, preferred_element_type=F32))
    sidx = lax.broadcasted_iota(I32, (nsel, QB), 0)
    cur = (q0 + lax.broadcasted_iota(I32, (nsel, QB), 1)) // SEL_LEN
    causal = sidx <= cur
    forced = (sidx == cur) | (sidx == 0)
    pick_state = {"v": jnp.where(forced, FORCED_SCORE, jnp.where(causal, imp, -1.0)),
                  "sel": jnp.zeros((nsel, QB), jnp.bool_)}

    def pick_next():
        v = pick_state["v"]
        mx = jnp.max(v, axis=0, keepdims=True)
        first = jnp.min(jnp.where(v == mx, sidx, nsel), axis=0, keepdims=True)
        pick = sidx == first
        pick_state["sel"] = pick_state["sel"] | pick
        pick_state["v"] = jnp.where(pick, -3e38, v)

    def picks_after_head(h, probs, accs):
        probs[h] = None
        for _ in range(n_top // H + (1 if h < n_top % H else 0)):
            pick_next()

    span = WINDOW + QB
    base = pl.multiple_of(jnp.maximum(q0 - WINDOW, 0), LANES)
    dist = qpos - (base + lax.broadcasted_iota(I32, (QB, span), 1))
    inwin = (dist >= 0) & (dist < WINDOW)
    accs_w = one_shot_branch(kw_ref[0, 0, :, pl.ds(base, span)], vw_ref[0, 0, pl.ds(base, span), :],
                             inwin, False, between=picks_after_head)
    for h in range(H):
        pre_scr[h] = pre_scr[h] + accs_w[h] * (gs[:, 2 * H + h:2 * H + h + 1] / row_sum(accs_w[h]))

    notsel = jnp.where(pick_state["sel"] & causal, 0.0, 1.0).T.astype(BF16)
    for j in range(H // 2):
        qa_scr[j] = jnp.concatenate([notsel, q_pairs[j]], axis=1)

    m_scr[...] = jnp.full(m_scr.shape, NEG, F32)
    acc_scr[...] = jnp.zeros(acc_scr.shape, F32)

    def sel_tile(kt, causal_tile):
        off = pl.multiple_of(kt * TK, TK)
        k_var = pair_keys(ks_ref[0, 0, :, pl.ds(off, TK)], lead=eneg_ref[:, pl.ds(off, TK)])
        score = lambda h: _bdot(qa_scr[h // 2], k_var[h % 2])
        v_t = vs_ref[0, 0, pl.ds(off, TK), :]
        if causal_tile:
            keep = (off + lax.broadcasted_iota(I32, (QB, TK), 1)) <= qpos
        ahead = 4
        scores = [score(h) for h in range(ahead)]
        for h in range(H):
            s = scores[h]
            if h + ahead < H:
                scores.append(score(h + ahead))
            if causal_tile:
                s = jnp.where(keep, s, NEG)
            m_old = m_scr[h]
            m_new = jnp.maximum(m_old, jnp.max(s, axis=-1, keepdims=True))
            alpha = jnp.exp2(m_old - m_new)
            pt = jnp.concatenate([jnp.exp2(s[:, c * LANES:(c + 1) * LANES] - m_new).astype(BF16)
                                  for c in range(TK // LANES)], axis=1)
            acc_scr[h] = alpha * acc_scr[h] + _bdot(pt, v_t)
            m_scr[h] = m_new

    n_tiles = (q0 + QB - 1) // TK + 1

    def body(kt, carry):
        sel_tile(kt, False)
        return carry

    lax.fori_loop(0, n_tiles - 1, body, 0)
    sel_tile(n_tiles - 1, True)

    def head_out(h):
        a_s = acc_scr[h]
        return pre_scr[h] + a_s * (gs[:, H + h:H + h + 1] / row_sum(a_s))

    low = lax.broadcasted_iota(I32, (QB, LANES), 1) < HEAD_DIM
    for j in range(H // 2):
        slab = jnp.where(low, head_out(2 * j), pltpu.roll(head_out(2 * j + 1), HEAD_DIM, 1))
        o_ref[:, j * LANES:(j + 1) * LANES] = slab.astype(BF16)


def _attention(q, gl, ks, vs, kw, vw, kc, vc, ovl, eneg, batch, seq):
    n = q.shape[0]
    QB = Q_TILE
    nq = seq // QB
    H = HEADS_PER_KV
    gw = HEADS_PER_KV * HEAD_DIM
    ncp = seq // CMP_STRIDE
    nsel = seq // SEL_LEN
    const2 = lambda a: pl.BlockSpec(a.shape, lambda b, g, j: (0,) * a.ndim)
    return pl.pallas_call(
        functools.partial(_attn_kernel, seq=seq),
        out_shape=jax.ShapeDtypeStruct((n, N_KV * gw), BF16),
        grid=(batch, N_KV, nq),
        in_specs=[pl.BlockSpec((QB, gw), lambda b, g, j: (b * nq + j, g)),
                  pl.BlockSpec((QB, LANES), lambda b, g, j: (b * nq + j, g)),
                  pl.BlockSpec((1, 1, HEAD_DIM, seq), lambda b, g, j: (b, g, 0, 0)),
                  pl.BlockSpec((1, 1, seq, LANES), lambda b, g, j: (b, g, 0, 0)),
                  pl.BlockSpec((1, 1, HEAD_DIM, seq), lambda b, g, j: (b, g, 0, 0)),
                  pl.BlockSpec((1, 1, seq, LANES), lambda b, g, j: (b, g, 0, 0)),
                  pl.BlockSpec((1, 1, HEAD_DIM, ncp), lambda b, g, j: (b, g, 0, 0)),
                  pl.BlockSpec((1, 1, ncp, LANES), lambda b, g, j: (b, g, 0, 0)),
                  const2(ovl), const2(eneg)],
        out_specs=pl.BlockSpec((QB, gw), lambda b, g, j: (b * nq + j, g)),
        scratch_shapes=[pltpu.VMEM((H // 2, QB, nsel + LANES), BF16), pltpu.VMEM((H, QB, LANES), F32),
                        pltpu.VMEM((H, QB, LANES), F32), pltpu.VMEM((H, QB, LANES), F32)],
        compiler_params=_cparams(("arbitrary", "arbitrary", "arbitrary")),
        name="sparse_attention",
    )(q, gl, ks, vs, kw, vw, kc, vc, ovl, eneg)


def _merge_kernel(x_ref, att_ref, ya_ref, mg_ref, g1_ref, wb_ref, wo_ref, o_ref):
    d = x_ref.shape[1]
    y_b = _bdot(att_ref[...], wb_ref[...])
    mg = mg_ref[...]
    merged = _sigmoid(mg[:, 0:d]) * ya_ref[...] + _sigmoid(mg[:, d:2 * d]) * y_b
    o_ref[...] = x_ref[...] + g1_ref[0] * _bdot(merged.astype(BF16), wo_ref[...])


def _merge(x2, att, y_a, mg, gate1, w_proj_b, w_out, seq):
    n, d = x2.shape
    tm = ROW_TILE
    tiles_per_b = seq // tm
    return pl.pallas_call(
        _merge_kernel,
        out_shape=jax.ShapeDtypeStruct((n, d), F32),
        grid=(n // tm,),
        in_specs=[pl.BlockSpec((tm, d), lambda i: (i, 0)),
                  pl.BlockSpec((tm, d), lambda i: (i, 0)),
                  pl.BlockSpec((tm, d), lambda i: (i, 0)),
                  pl.BlockSpec((tm, 2 * d), lambda i: (i, 0)),
                  pl.BlockSpec((1, 1, d), lambda i: (i // tiles_per_b, 0, 0)),
                  pl.BlockSpec(w_proj_b.shape, lambda i: (0, 0)),
                  pl.BlockSpec(w_out.shape, lambda i: (0, 0))],
        out_specs=pl.BlockSpec((tm, d), lambda i: (i, 0)),
        compiler_params=_cparams(("arbitrary",)),
        name="merge_out_projection",
    )(x2, att, y_a, mg, gate1[:, None, :], w_proj_b, w_out)


def _router_kernel(x_ref, g_ref, sc_ref, sh_ref, wh_ref, wl_ref, rb_ref, tri_ref,
                   h_ref, e_ref, gt_ref, pos_ref, cnt_ref, carry_ref):
    i = pl.program_id(0)
    tm, d = x_ref.shape

    @pl.when(i == 0)
    def _():
        carry_ref[...] = jnp.zeros((N_EXPERTS, 1), F32)

    x = x_ref[...]
    ms = jnp.mean(x * x, axis=-1, keepdims=True)
    h = x * lax.rsqrt(ms + EPS) * g_ref[...]
    h = h * (1.0 + sc_ref[0]) + sh_ref[0]
    for s in range(d // LANES):
        h_ref[pl.ds(s, tm, stride=SUBLANES), :] = h[:, s * LANES:(s + 1) * LANES]

    h_hi, h_lo = _split(h)
    nt = (((1,), (1,)), ((), ()))
    logits = (lax.dot_general(wh_ref[...], h_hi, nt, preferred_element_type=F32)
              + lax.dot_general(wh_ref[...], h_lo, nt, preferred_element_type=F32)
              + lax.dot_general(wl_ref[...], h_hi, nt, preferred_element_type=F32)
              + rb_ref[...])
    eidx = lax.broadcasted_iota(I32, (N_EXPERTS, tm), 0)
    onehot = jnp.zeros((N_EXPERTS, tm), F32)
    picks, vals, ids = [], [], []
    for _ in range(TOP_K):
        mx = jnp.max(logits, axis=0, keepdims=True)
        first = jnp.min(jnp.where(logits == mx, eidx, N_EXPERTS), axis=0, keepdims=True)
        pick = eidx == first
        picks.append(pick)
        vals.append(mx)
        ids.append(first)
        onehot = jnp.where(pick, 1.0, onehot)
        logits = jnp.where(pick, -3e38, logits)
    ex = [jnp.exp(vk - vals[0]) for vk in vals]
    den = ex[0] + ex[1] + ex[2] + ex[3]
    before = _bdot(onehot.astype(BF16), tri_ref[...]) + carry_ref[...]
    pad = SUBLANES - TOP_K
    pos = [jnp.sum(jnp.where(pk, before, 0.0), axis=0, keepdims=True) for pk in picks]
    e_ref[...] = jnp.concatenate(ids + [jnp.zeros((pad, tm), I32)], axis=0)
    gt_ref[...] = jnp.concatenate([e / den for e in ex] + [jnp.zeros((pad, tm), F32)], axis=0)
    pos_ref[...] = jnp.concatenate(pos + [jnp.zeros((pad, tm), F32)], axis=0).astype(I32)
    carry_ref[...] = carry_ref[...] + jnp.sum(onehot, axis=1, keepdims=True)
    cnt_ref[...] = carry_ref[...]


def _router(x2, gain, scale, shift, wr_hi, wr_lo, rb, tri, seq):
    n, d = x2.shape
    tm = ROUTE_TILE
    tiles_per_b = seq // tm
    nsub = d // LANES
    return pl.pallas_call(
        _router_kernel,
        out_shape=[jax.ShapeDtypeStruct((n * nsub, LANES), F32),
                   jax.ShapeDtypeStruct((SUBLANES, n), I32),
                   jax.ShapeDtypeStruct((SUBLANES, n), F32),
                   jax.ShapeDtypeStruct((SUBLANES, n), I32),
                   jax.ShapeDtypeStruct((N_EXPERTS, 1), F32)],
        grid=(n // tm,),
        in_specs=[pl.BlockSpec((tm, d), lambda i: (i, 0)),
                  pl.BlockSpec((1, d), lambda i: (0, 0)),
                  pl.BlockSpec((1, 1, d), lambda i: (i // tiles_per_b, 0, 0)),
                  pl.BlockSpec((1, 1, d), lambda i: (i // tiles_per_b, 0, 0)),
                  pl.BlockSpec(wr_hi.shape, lambda i: (0, 0)),
                  pl.BlockSpec(wr_lo.shape, lambda i: (0, 0)),
                  pl.BlockSpec((N_EXPERTS, 1), lambda i: (0, 0)),
                  pl.BlockSpec((tm, tm), lambda i: (0, 0))],
        out_specs=[pl.BlockSpec((tm * nsub, LANES), lambda i: (i, 0)),
                   pl.BlockSpec((SUBLANES, tm), lambda i: (0, i)),
                   pl.BlockSpec((SUBLANES, tm), lambda i: (0, i)),
                   pl.BlockSpec((SUBLANES, tm), lambda i: (0, i)),
                   pl.BlockSpec((N_EXPERTS, 1), lambda i: (0, 0))],
        scratch_shapes=[pltpu.VMEM((N_EXPERTS, 1), F32)],
        compiler_params=_cparams(("arbitrary",)),
        name="router",
    )(x2, gain.reshape(1, d), scale[:, None, :], shift[:, None, :], wr_hi, wr_lo, rb.reshape(N_EXPERTS, 1), tri)


def _dispatch_kernel(dest_ref, fill_ref, h_ref, xs_ref, zero_ref, sem, *, nsub):
    i = pl.program_id(0)
    tt = MOVE_TILE

    def row_copy(n, k):
        dst = dest_ref[0, 0, k * tt + n]
        return pltpu.make_async_copy(h_ref.at[pl.ds(n * nsub, nsub), :],
                                     xs_ref.at[pl.ds(dst * nsub, nsub), :], sem)

    def issue(n, carry):
        for k in range(TOP_K):
            row_copy(n, k).start()
        return carry

    def drain(n, carry):
        for k in range(TOP_K):
            row_copy(n, k).wait()
        return carry

    lax.fori_loop(0, tt, issue, 0)

    @pl.when(i == 0)
    def _():
        zero_ref[...] = jnp.zeros(zero_ref.shape, F32)

        def fill_copy(r):
            return pltpu.make_async_copy(zero_ref, xs_ref.at[pl.ds(r * nsub, nsub), :], sem)

        def per_expert(e, carry):
            lo, hi = fill_ref[2 * e], fill_ref[2 * e + 1]
            lax.fori_loop(lo, hi, lambda r, c: (fill_copy(r).start(), c)[1], 0)
            lax.fori_loop(lo, hi, lambda r, c: (fill_copy(r).wait(), c)[1], 0)
            return carry

        lax.fori_loop(0, fill_ref.shape[0] // 2, per_expert, 0)

    lax.fori_loop(0, tt, drain, 0)


def _dispatch(h_rows, dest_tiles, fill, n_rows, nsub):
    n_tok = h_rows.shape[0] // nsub
    tt = MOVE_TILE
    grid_spec = pltpu.PrefetchScalarGridSpec(
        num_scalar_prefetch=0,
        grid=(n_tok // tt,),
        in_specs=[pl.BlockSpec((1, 1, TOP_K * tt), lambda i: (i, 0, 0), memory_space=pltpu.SMEM),
                  pl.BlockSpec(memory_space=pltpu.SMEM),
                  pl.BlockSpec((tt * nsub, LANES), lambda i: (i, 0))],
        out_specs=pl.BlockSpec(memory_space=pl.ANY),
        scratch_shapes=[pltpu.VMEM((nsub, LANES), F32), pltpu.SemaphoreType.DMA(())],
    )
    return pl.pallas_call(
        functools.partial(_dispatch_kernel, nsub=nsub),
        out_shape=jax.ShapeDtypeStruct((n_rows * nsub, LANES), F32),
        grid_spec=grid_spec,
        compiler_params=pltpu.CompilerParams(dimension_semantics=("arbitrary",), vmem_limit_bytes=VMEM_LIMIT,
                                             has_side_effects=True),
        name="dispatch",
    )(dest_tiles, fill, h_rows)


PAIR = 2 * LANES


def _pair_permutation():
    p = np.zeros((PAIR, PAIR), np.float32)
    p[2 * np.arange(LANES), np.arange(LANES)] = 1.0
    p[2 * np.arange(LANES) + 1, LANES + np.arange(LANES)] = 1.0
    return p


def _regroup_kernel(w_ref, p_ref, o_ref):
    w = w_ref[0].astype(BF16)
    for j in range(w.shape[1] // PAIR):
        o_ref[0, :, j * PAIR:(j + 1) * PAIR] = _bdot(w[:, j * PAIR:(j + 1) * PAIR], p_ref[...]).astype(BF16)


def _regroup_gate_up(w_gate_up):
    depth, n_e, d, f2 = w_gate_up.shape
    tm = 512
    perm = jnp.asarray(_pair_permutation(), BF16)
    out = pl.pallas_call(
        _regroup_kernel,
        out_shape=jax.ShapeDtypeStruct((depth * n_e, d, f2), BF16),
        grid=(depth * n_e, d // tm),
        in_specs=[pl.BlockSpec((1, tm, f2), lambda e, i: (e, i, 0)),
                  pl.BlockSpec((PAIR, PAIR), lambda e, i: (0, 0))],
        out_specs=pl.BlockSpec((1, tm, f2), lambda e, i: (e, i, 0)),
        compiler_params=_cparams(("arbitrary", "arbitrary")),
        name="regroup_gate_up",
    )(w_gate_up.reshape(depth * n_e, d, f2), perm)
    return out.reshape(depth, n_e, d, f2)


def _regroup_bias(b):
    lead = b.shape[:-1]
    return b.reshape(*lead, -1, LANES, 2).swapaxes(-1, -2).reshape(*lead, -1)


def _expert_kernel(be_ref, nu_ref, xs_ref, wgu_ref, bgu_ref, wd_ref, bd_ref, ys_ref, *, nsub):
    i = pl.program_id(0)
    rows = EXPERT_ROWS
    f = wd_ref.shape[1]

    @pl.when(i < nu_ref[0])
    def _():
        x = jnp.concatenate([xs_ref[pl.ds(s, rows, stride=nsub), :] for s in range(nsub)], axis=-1)
        gu = _bdot(x.astype(BF16), wgu_ref[0]) + bgu_ref[0]
        acts = []
        for j in range(f // LANES):
            g_lin = jnp.minimum(gu[:, j * PAIR:j * PAIR + LANES], SWIGLU_LIMIT)
            u_lin = jnp.clip(gu[:, j * PAIR + LANES:(j + 1) * PAIR], -SWIGLU_LIMIT, SWIGLU_LIMIT)
            acts.append(((u_lin + 1.0) * g_lin * _sigmoid(g_lin * SWIGLU_ALPHA)).astype(BF16))
        act = jnp.concatenate(acts, axis=-1)
        y = _bdot(act, wd_ref[0]) + bd_ref[0]
        for s in range(nsub):
            ys_ref[pl.ds(s, rows, stride=nsub), :] = y[:, s * LANES:(s + 1) * LANES]

    @pl.when(i >= nu_ref[0])
    def _():
        ys_ref[...] = jnp.zeros(ys_ref.shape, F32)


def _experts(blk_e, n_used, xs, w_gu, b_gu, w_d, b_d, nsub):
    rows = EXPERT_ROWS
    n_blocks = xs.shape[0] // (rows * nsub)
    d, f2 = w_gu.shape[1], w_gu.shape[2]
    f = w_d.shape[1]
    row_map = lambda i, be, nu: (jnp.minimum(i, nu[0] - 1), 0)
    grid_spec = pltpu.PrefetchScalarGridSpec(
        num_scalar_prefetch=2,
        grid=(n_blocks,),
        in_specs=[pl.BlockSpec((rows * nsub, LANES), row_map),
                  pl.BlockSpec((1, d, f2), lambda i, be, nu: (be[i], 0, 0)),
                  pl.BlockSpec((1, 1, f2), lambda i, be, nu: (be[i], 0, 0)),
                  pl.BlockSpec((1, f, d), lambda i, be, nu: (be[i], 0, 0)),
                  pl.BlockSpec((1, 1, d), lambda i, be, nu: (be[i], 0, 0))],
        out_specs=pl.BlockSpec((rows * nsub, LANES), lambda i, be, nu: (i, 0)),
    )
    return pl.pallas_call(
        functools.partial(_expert_kernel, nsub=nsub),
        out_shape=jax.ShapeDtypeStruct(xs.shape, F32),
        grid_spec=grid_spec,
        compiler_params=_cparams(("arbitrary",)),
        name="experts",
    )(blk_e, n_used, xs, w_gu, b_gu, w_d, b_d)


def _combine_kernel(dest_ref, ys_ref, x_ref, gt_ref, g2_ref, o_ref, buf_ref, sem, *, nsub):
    tt = MOVE_TILE

    def row_copy(n, k):
        src = dest_ref[0, 0, k * tt + n]
        return pltpu.make_async_copy(ys_ref.at[pl.ds(src * nsub, nsub), :],
                                     buf_ref.at[k, pl.ds(n * nsub, nsub), :], sem)

    def issue(n, carry):
        for k in range(TOP_K):
            row_copy(n, k).start()
        return carry

    def drain(n, carry):
        for k in range(TOP_K):
            row_copy(n, k).wait()
        return carry

    lax.fori_loop(0, tt, issue, 0)
    lax.fori_loop(0, tt, drain, 0)
    gt = gt_ref[...]
    acc = None
    for k in range(TOP_K):
        yk = jnp.concatenate([buf_ref[k, pl.ds(s, tt, stride=nsub), :] for s in range(nsub)], axis=-1)
        term = gt[:, k:k + 1] * yk
        acc = term if acc is None else acc + term
    o_ref[...] = x_ref[...] + g2_ref[0] * acc


def _combine(dest_tiles, ys, x2, gates_t, gate2, seq, nsub):
    n, d = x2.shape
    tt = MOVE_TILE
    tiles_per_b = seq // tt
    grid_spec = pltpu.PrefetchScalarGridSpec(
        num_scalar_prefetch=0,
        grid=(n // tt,),
        in_specs=[pl.BlockSpec((1, 1, TOP_K * tt), lambda i: (i, 0, 0), memory_space=pltpu.SMEM),
                  pl.BlockSpec(memory_space=pl.ANY),
                  pl.BlockSpec((tt, d), lambda i: (i, 0)),
                  pl.BlockSpec((tt, SUBLANES), lambda i: (i, 0)),
                  pl.BlockSpec((1, 1, d), lambda i: (i // tiles_per_b, 0, 0))],
        out_specs=pl.BlockSpec((tt, d), lambda i: (i, 0)),
        scratch_shapes=[pltpu.VMEM((TOP_K, tt * nsub, LANES), F32), pltpu.SemaphoreType.DMA(())],
    )
    return pl.pallas_call(
        functools.partial(_combine_kernel, nsub=nsub),
        out_shape=jax.ShapeDtypeStruct((n, d), F32),
        grid_spec=grid_spec,
        compiler_params=_cparams(("arbitrary",)),
        name="combine",
    )(dest_tiles, ys, x2, gates_t, gate2[:, None, :])


def _attention_constants(seq):
    nsel = seq // SEL_LEN
    ncp = seq // CMP_STRIDE
    cmp_start = np.arange(ncp) * CMP_STRIDE
    sel_start = np.arange(nsel) * SEL_LEN
    ovl = ((cmp_start[:, None] < sel_start[None, :] + SEL_LEN)
           & (cmp_start[:, None] + CMP_LEN - 1 >= sel_start[None, :])).astype(np.float32)
    ovl[ncp - 1, :] = 0.0
    eneg = np.where(np.arange(seq)[None, :] // SEL_LEN == np.arange(nsel)[:, None], NEG, 0.0).astype(np.float32)
    return jnp.asarray(ovl.T, BF16), jnp.asarray(eneg, BF16)


def _layer(x2, mod, consts, p, batch, seq):
    n, d = x2.shape
    shift1, scale1, gate1, shift2, scale2, gate2 = [mod[:, k * d:(k + 1) * d] for k in range(6)]
    (cos2, sin2, cos_c, sin_c, ovl, eneg, tri) = consts

    u, q, kcm, vcm, kv4, gl, mg = _in_projection(x2, p["norm1"], scale1, shift1, p["q_gain2"], cos2, sin2,
                                                 p["w_in_parts"], seq, q_index=1)
    y_a = _pool_mixer(u, p["w_pool"], p["pool_scale"], p["w_proj_a"], batch, seq)
    ks, vs, kw, vw = _kv_prep(kv4, p["k_gain2"], cos2, sin2, batch, seq)
    kc, vc = _compress(kcm, vcm, p["cmp_pos2"], p["cmp_wa"], p["cmp_wb"], p["cmp_b1"], p["cmp_w2"],
                       p["kc_gain"], cos_c, sin_c, batch, seq)
    att = _attention(q, gl, ks, vs, kw, vw, kc, vc, ovl, eneg, batch, seq)
    x2 = _merge(x2, att, y_a, mg, gate1, p["w_proj_b"], p["w_out"], seq)

    nsub = d // LANES
    h_rows, top_e, gates, pos, counts = _router(x2, p["norm2"], scale2, shift2, p["wr_hi"], p["wr_lo"],
                                                p["router_b"], tri, seq)
    rows = EXPERT_ROWS
    n_blocks = -(-n * TOP_K // rows) + N_EXPERTS
    cnt = counts[:, 0].astype(I32)
    padded = (cnt + rows - 1) // rows * rows
    pad_end = jnp.cumsum(padded)
    pad_start = pad_end - padded
    e_ids = jnp.arange(N_EXPERTS, dtype=I32)
    dest = jnp.sum(jnp.where(top_e[:TOP_K, :, None] == e_ids, pad_start, 0), axis=-1) + pos[:TOP_K]
    tt = MOVE_TILE
    dest_tiles = dest.reshape(TOP_K, n // tt, tt).transpose(1, 0, 2).reshape(n // tt, 1, TOP_K * tt)
    fill_lo = jnp.concatenate([pad_start + cnt, pad_end[-1:]])
    fill_hi = jnp.concatenate([pad_end, jnp.full((1,), n_blocks * rows, I32)])
    fill = jnp.stack([fill_lo, fill_hi], axis=1).reshape(-1).astype(I32)
    blk_first = jnp.arange(n_blocks, dtype=I32)[:, None] * rows
    blk_e = jnp.minimum(jnp.sum((pad_end[None, :] <= blk_first).astype(I32), axis=1), N_EXPERTS - 1)
    n_used = (pad_end[-1:] // rows).astype(I32)
    xs = _dispatch(h_rows, dest_tiles, fill, n_blocks * rows, nsub)
    ys = _experts(blk_e, n_used, xs, p["w_gu"], p["b_gu"], p["w_d"], p["b_d"], nsub)
    gates_t = gates.T
    return _combine(dest_tiles, ys, x2, gates_t, gate2, seq, nsub)


def _prep_layer(l, norm1, norm2, w_in, w_pool, pool_scale, q_norm, k_norm, cmp_pos, cmp_w1, cmp_b1, cmp_w2,
                w_proj_a, w_proj_b, w_out, router_w, router_b, w_gate_up, b_gate_up, w_down, b_down):
    d = w_in.shape[1]
    w = w_in[l]
    o_q = POOL_WIDTH
    o_kv = o_q + N_KV * HEADS_PER_KV * HEAD_DIM
    o_g = o_kv + 6 * LANES
    n_gate = 3 * N_KV * HEADS_PER_KV
    o_m = o_g + n_gate
    wg = w[:, o_g:o_m].reshape(d, 3, N_KV, HEADS_PER_KV).transpose(0, 2, 1, 3).reshape(d, N_KV, 3 * HEADS_PER_KV)
    wg = jnp.pad(wg, ((0, 0), (0, 0), (0, LANES - 3 * HEADS_PER_KV))).reshape(d, N_KV * LANES)
    parts = [w[:, 0:o_q], w[:, o_q:o_kv], w[:, o_kv:o_kv + LANES], w[:, o_kv + LANES:o_kv + 2 * LANES],
             w[:, o_kv + 2 * LANES:o_g], wg, w[:, o_m:]]
    w_in_parts = [a.astype(BF16) for a in parts]

    half = CMP_LEN // 2
    w1 = cmp_w1[l].reshape(2, CMP_LEN, HEAD_DIM, CMP_HIDDEN)

    def chunk_weight(wh):
        z = jnp.zeros_like(wh)
        g0 = jnp.stack([wh, z], axis=2)
        g1 = jnp.stack([z, wh], axis=2)
        return jnp.concatenate([g0, g1], axis=-1).reshape(2, half * LANES, N_KV * CMP_HIDDEN)

    cmp_wa = chunk_weight(w1[:, :half]).astype(BF16)
    cmp_wb = chunk_weight(w1[:, half:]).astype(BF16)
    pos = cmp_pos[l]
    pos_t = jnp.broadcast_to(pos[:, :, None, :], (2, CMP_LEN, N_KV, HEAD_DIM))
    cmp_pos2 = jnp.stack([pos_t[:, :half].reshape(2, half * LANES), pos_t[:, half:].reshape(2, half * LANES)], axis=1)
    b1 = jnp.tile(cmp_b1[l], (1, N_KV))[:, None, :]
    w2 = cmp_w2[l]
    z2 = jnp.zeros_like(w2)
    cmp_w2b = jnp.concatenate([jnp.concatenate([w2, z2], axis=-1), jnp.concatenate([z2, w2], axis=-1)],
                              axis=1).astype(BF16)
    wr = router_w[l].T
    wr_hi = wr.astype(BF16)
    wr_lo = (wr - wr_hi.astype(F32)).astype(BF16)
    w_gu = w_gate_up[l]
    b_gu = _regroup_bias(b_gate_up[l])[:, None, :]
    return dict(
        norm1=norm1[l], norm2=norm2[l], w_in_parts=w_in_parts,
        w_pool=w_pool[l].astype(BF16), pool_scale=pool_scale[l], w_proj_a=w_proj_a[l].astype(BF16),
        k_gain2=jnp.stack([jnp.tile(k_norm[l, 1], N_KV), jnp.tile(k_norm[l, 2], N_KV)], axis=0),
        kc_gain=jnp.tile(k_norm[l, 0], N_KV)[None, :], q_gain2=jnp.tile(q_norm[l], 2)[None, :],
        cmp_pos2=cmp_pos2, cmp_wa=cmp_wa, cmp_wb=cmp_wb, cmp_b1=b1, cmp_w2=cmp_w2b,
        w_proj_b=w_proj_b[l].astype(BF16), w_out=w_out[l].astype(BF16),
        wr_hi=wr_hi, wr_lo=wr_lo, router_b=router_b[l],
        w_gu=w_gu, b_gu=b_gu, w_d=w_down[l].astype(BF16), b_d=b_down[l][:, None, :],
    )


def kernel(x, c, norm1, norm2, ada_w, ada_b, w_in, w_pool, pool_scale, q_norm, k_norm, cmp_pos, cmp_w1, cmp_b1,
           cmp_w2, w_proj_a, w_proj_b, w_out, router_w, router_b, w_gate_up, b_gate_up, w_down, b_down):
    batch, seq, d = x.shape
    depth = norm1.shape[0]
    assert seq % KV_TILE == 0 and seq % ROUTE_TILE == 0 and seq >= WINDOW + Q_TILE and d % LANES == 0
    assert (batch * seq * TOP_K) % EXPERT_ROWS == 0

    pos = np.arange(seq)
    cq, sq = _rope_tables(pos)
    cc, sc = _rope_tables(np.arange(seq // CMP_STRIDE) * CMP_STRIDE + CMP_LEN - 1)
    tile2 = lambda a: jnp.asarray(np.concatenate([a, a], axis=1))
    tri = jnp.asarray(np.triu(np.ones((ROUTE_TILE, ROUTE_TILE), np.float32), 1), BF16)
    consts = (tile2(cq), tile2(sq), tile2(cc), tile2(sc), *_attention_constants(seq), tri)

    c_pad = jnp.pad(c, ((0, SUBLANES - batch % SUBLANES if batch % SUBLANES else 0), (0, 0)))
    x2 = x.reshape(batch * seq, d)
    w_gate_up = _regroup_gate_up(w_gate_up)
    for l in range(depth):
        mod = _modulation(c_pad, ada_w[l], ada_b[l])[:batch]
        p = _prep_layer(l, norm1, norm2, w_in, w_pool, pool_scale, q_norm, k_norm, cmp_pos, cmp_w1, cmp_b1, cmp_w2,
                        w_proj_a, w_proj_b, w_out, router_w, router_b, w_gate_up, b_gate_up, w_down, b_down)
        x2 = _layer(x2, mod, consts, p, batch, seq)
    return x2.reshape(batch, seq, d)
```

```python
import functools

import numpy as np
import jax
import jax.numpy as jnp
from jax import lax
from jax.experimental import pallas as pl
from jax.experimental.pallas import tpu as pltpu

F32 = jnp.float32
BF16 = jnp.bfloat16
I32 = jnp.int32

POOL_WINDOWS = (2, 4, 8, 16)
POOL_GROUP = 128
POOL_WIDTH = 512
HEAD_DIM = 64
N_KV = 2
HEADS_PER_KV = 8
ROPE_DIM = 16
ROPE_HALF = 8
ROPE_THETA = 500000.0
CMP_LEN = 32
CMP_STRIDE = 16
CMP_HIDDEN = 256
SEL_LEN = 64
SEL_TOP = 16
WINDOW = 512
FORCED_SCORE = 1e9
N_EXPERTS = 32
TOP_K = 4
SWIGLU_LIMIT = 7.0
SWIGLU_ALPHA = 1.702
EPS = 1e-6
NEG = -1e30
QK_SCALE = HEAD_DIM ** -0.5 * 1.4426950408889634

LANES = 128
SUBLANES = 8
VMEM_LIMIT = 56 * 1024 * 1024

ROW_TILE = 256
POOL_HALO = 16
Q_TILE = 256
KV_TILE = 512
ROUTE_TILE = 512
EXPERT_ROWS = 512
DISPATCH_TILE = 1024
COMBINE_TILE = 256


def _cparams(sem):
    return pltpu.CompilerParams(dimension_semantics=sem, vmem_limit_bytes=VMEM_LIMIT)


def _bdot(a, b):
    return jnp.dot(a, b, preferred_element_type=F32)


def _split(a):
    hi = a.astype(BF16)
    lo = (a - hi.astype(F32)).astype(BF16)
    return hi, lo


def _sigmoid(x):
    return 1.0 / (1.0 + jnp.exp(-x))


def _mod_kernel(c_ref, w_ref, b_ref, o_ref):
    c = c_ref[...]
    a = c * _sigmoid(c)
    a_hi, a_lo = _split(a)
    w_hi, w_lo = _split(w_ref[...])
    o_ref[...] = _bdot(a_hi, w_hi) + _bdot(a_lo, w_hi) + _bdot(a_hi, w_lo) + b_ref[...]


def _modulation(c_pad, ada_w, ada_b):
    rows, d = c_pad.shape
    n = ada_w.shape[1]
    tn = 512
    return pl.pallas_call(
        _mod_kernel,
        out_shape=jax.ShapeDtypeStruct((rows, n), F32),
        grid=(n // tn,),
        in_specs=[pl.BlockSpec((rows, d), lambda i: (0, 0)),
                  pl.BlockSpec((d, tn), lambda i: (0, i)),
                  pl.BlockSpec((1, tn), lambda i: (0, i))],
        out_specs=pl.BlockSpec((rows, tn), lambda i: (0, i)),
        compiler_params=_cparams(("arbitrary",)),
        name="modulation",
    )(c_pad, ada_w, ada_b.reshape(1, n))


def _pool_mixer(u, t_in_seq, wp_ref, ps_ref, wa_ref, ext_ref):
    ts = u.shape[0]

    @pl.when(t_in_seq == 0)
    def _():
        ext_ref[0:POOL_HALO, :] = jnp.zeros((POOL_HALO, POOL_WIDTH), F32)

    @pl.when(t_in_seq > 0)
    def _():
        ext_ref[0:POOL_HALO, :] = ext_ref[ts:ts + POOL_HALO, :]

    ext_ref[POOL_HALO:POOL_HALO + ts, :] = u
    t = t_in_seq * ts + lax.broadcasted_iota(I32, (ts, 1), 0)
    outs = []
    for gi, win in enumerate(POOL_WINDOWS):
        lo, hi = gi * POOL_GROUP, (gi + 1) * POOL_GROUP
        ug = u[:, lo:hi]
        total = ug
        for jj in range(1, win):
            total = total + ext_ref[POOL_HALO - jj:POOL_HALO - jj + ts, lo:hi]
        cnt = jnp.minimum(t + 1, win).astype(F32)
        dlt = total / cnt - ug
        outs.append(_bdot(dlt.astype(BF16), wp_ref[gi]))
    y = jnp.concatenate(outs, axis=-1) * ps_ref[...]
    return _bdot(y.astype(BF16), wa_ref[...])


def _inproj_kernel(x_ref, g_ref, sc_ref, sh_ref, qg_ref, kg_ref, cos_ref, sin_ref,
                   wu_ref, wq_ref, wkc_ref, wvc_ref, wkv_ref, wgl_ref, wmg_ref, wp_ref, ps_ref, wa_ref,
                   ya_ref, q_ref, kc_ref, vc_ref, ks_ref, vs_ref, kw_ref, vw_ref, gl_ref, mg_ref,
                   ext_ref, *, tiles_per_seq):
    x = x_ref[...]
    ms = jnp.mean(x * x, axis=-1, keepdims=True)
    h = x * lax.rsqrt(ms + EPS) * g_ref[...]
    h = h * (1.0 + sc_ref[0]) + sh_ref[0]
    hb = h.astype(BF16)
    cos, sin = cos_ref[...], sin_ref[...]

    yq = _bdot(hb, wq_ref[...])
    for j in range(yq.shape[1] // LANES):
        q2 = _norm_rope_pair(yq[:, j * LANES:(j + 1) * LANES], qg_ref[...], cos, sin)
        q_ref[:, j * LANES:(j + 1) * LANES] = (q2 * QK_SCALE).astype(BF16)

    kc_ref[...] = _bdot(hb, wkc_ref[...])
    vc_ref[...] = _bdot(hb, wvc_ref[...])
    gl_ref[...] = _bdot(hb, wgl_ref[...])
    mg_ref[...] = _bdot(hb, wmg_ref[...])

    kv = _bdot(hb, wkv_ref[...])
    ks = _norm_rope_pair(kv[:, 0:LANES], kg_ref[0:1, :], cos, sin).T.astype(BF16)
    kw = _norm_rope_pair(kv[:, 2 * LANES:3 * LANES], kg_ref[1:2, :], cos, sin).T.astype(BF16)
    for gi in range(N_KV):
        ks_ref[0, gi] = ks[gi * HEAD_DIM:(gi + 1) * HEAD_DIM, :]
        kw_ref[0, gi] = kw[gi * HEAD_DIM:(gi + 1) * HEAD_DIM, :]
        vs_ref[0, gi] = _values_with_ones(kv[:, LANES:2 * LANES], gi)
        vw_ref[0, gi] = _values_with_ones(kv[:, 3 * LANES:4 * LANES], gi)

    ya_ref[...] = _pool_mixer(_bdot(hb, wu_ref[...]), pl.program_id(0) % tiles_per_seq,
                              wp_ref, ps_ref, wa_ref, ext_ref)


def _in_projection(x2, gain, scale, shift, q_gain2, k_gain2, cos2, sin2, weights, w_pool, pool_scale, w_proj_a,
                   batch, seq):
    n, d = x2.shape
    tm = ROW_TILE
    tps = seq // tm
    w_u, w_q, w_kc, w_vc, w_kv, w_gl, w_mg = weights
    const = lambda a: pl.BlockSpec(a.shape, lambda i: (0,) * a.ndim)
    rows = lambda width: pl.BlockSpec((tm, width), lambda i: (i, 0))
    per_b = pl.BlockSpec((1, 1, d), lambda i: (i // tps, 0, 0))
    table = pl.BlockSpec((tm, LANES), lambda i: (i % tps, 0))
    kt_shape = jax.ShapeDtypeStruct((batch, N_KV, HEAD_DIM, seq), BF16)
    v_shape = jax.ShapeDtypeStruct((batch, N_KV, seq, LANES), BF16)
    kt_spec = pl.BlockSpec((1, N_KV, HEAD_DIM, tm), lambda i: (i // tps, 0, 0, i % tps))
    v_spec = pl.BlockSpec((1, N_KV, tm, LANES), lambda i: (i // tps, 0, i % tps, 0))
    f32_out = lambda width: jax.ShapeDtypeStruct((n, width), F32)
    pool_scale2 = pool_scale.reshape(1, POOL_WIDTH)
    return pl.pallas_call(
        functools.partial(_inproj_kernel, tiles_per_seq=tps),
        out_shape=[f32_out(w_proj_a.shape[1]), jax.ShapeDtypeStruct((n, w_q.shape[1]), BF16),
                   f32_out(w_kc.shape[1]), f32_out(w_vc.shape[1]), kt_shape, v_shape, kt_shape, v_shape,
                   f32_out(w_gl.shape[1]), f32_out(w_mg.shape[1])],
        grid=(n // tm,),
        in_specs=[rows(d), const(gain.reshape(1, d)), per_b, per_b, const(q_gain2), const(k_gain2), table, table,
                  const(w_u), const(w_q), const(w_kc), const(w_vc), const(w_kv), const(w_gl), const(w_mg),
                  const(w_pool), const(pool_scale2), const(w_proj_a)],
        out_specs=[rows(w_proj_a.shape[1]), rows(w_q.shape[1]), rows(w_kc.shape[1]), rows(w_vc.shape[1]),
                   kt_spec, v_spec, kt_spec, v_spec, rows(w_gl.shape[1]), rows(w_mg.shape[1])],
        scratch_shapes=[pltpu.VMEM((tm + POOL_HALO, POOL_WIDTH), F32)],
        compiler_params=_cparams(("arbitrary",)),
        name="in_projection",
    )(x2, gain.reshape(1, d), scale[:, None, :], shift[:, None, :], q_gain2, k_gain2, cos2, sin2,
      w_u, w_q, w_kc, w_vc, w_kv, w_gl, w_mg, w_pool, pool_scale2, w_proj_a)


def _rope_tables(pos):
    inv_freq = np.float32(ROPE_THETA) ** (-(np.arange(ROPE_HALF, dtype=np.float32)) / np.float32(ROPE_HALF))
    ang = pos.astype(np.float32)[:, None] * inv_freq.astype(np.float32)[None, :]
    cos, sin = np.cos(ang), np.sin(ang)
    c = np.ones((pos.shape[0], HEAD_DIM), np.float32)
    s = np.zeros((pos.shape[0], HEAD_DIM), np.float32)
    c[:, :ROPE_HALF] = cos
    c[:, ROPE_HALF:ROPE_DIM] = cos
    s[:, :ROPE_HALF] = -sin
    s[:, ROPE_HALF:ROPE_DIM] = sin
    return c, s


def _norm_rope_pair(x, gain, cos, sin):
    rows = x.shape[0]
    lane = lax.broadcasted_iota(I32, (rows, LANES), 1)
    first = lane < HEAD_DIM
    x2 = x * x
    s0 = jnp.sum(jnp.where(first, x2, 0.0), axis=-1, keepdims=True)
    s1 = jnp.sum(jnp.where(first, 0.0, x2), axis=-1, keepdims=True)
    r = lax.rsqrt(jnp.where(first, s0, s1) * (1.0 / HEAD_DIM) + EPS)
    y = x * r * gain
    low = (lane & (HEAD_DIM - 1)) < ROPE_HALF
    sw = jnp.where(low, pltpu.roll(y, LANES - ROPE_HALF, 1), pltpu.roll(y, ROPE_HALF, 1))
    return y * cos + sw * sin


def _values_with_ones(v, gi):
    lane = lax.broadcasted_iota(I32, v.shape, 1)
    vg = v if gi == 0 else pltpu.roll(v, HEAD_DIM, 1)
    return jnp.where(lane < HEAD_DIM, vg, 1.0).astype(BF16)


def _compress_kernel(k_ref, v_ref, pos_ref, wa_ref, wb_ref, b1_ref, w2_ref, gain_ref, cos_ref, sin_ref,
                     kc_ref, vc_ref):
    nch = k_ref.shape[0]

    def mlp(x, idx):
        xa = (x + pos_ref[idx, 0:1, :]).astype(BF16)
        xb = (x + pos_ref[idx, 1:2, :]).astype(BF16)
        a = _bdot(xa, wa_ref[idx])
        b = _bdot(xb, wb_ref[idx])
        b_next = jnp.concatenate([b[1:nch, :], jnp.zeros((1, b.shape[1]), F32)], axis=0)
        pre = a + b_next + b1_ref[idx]
        hid = pre * _sigmoid(pre)
        return _bdot(hid.astype(BF16), w2_ref[idx])

    kc = _norm_rope_pair(mlp(k_ref[...], 0), gain_ref[...], cos_ref[...], sin_ref[...]).T.astype(BF16)
    vc = mlp(v_ref[...], 1)
    for gi in range(N_KV):
        kc_ref[0, gi] = kc[gi * HEAD_DIM:(gi + 1) * HEAD_DIM, :]
        vc_ref[0, gi] = _values_with_ones(vc, gi)


def _compress(kcm, vcm, pos2, wa, wb, b1, w2, gain, cos_c, sin_c, batch, seq):
    nch = seq // CMP_STRIDE
    width = CMP_STRIDE * LANES
    full = lambda a: pl.BlockSpec(a.shape, lambda b: (0,) * a.ndim)
    return pl.pallas_call(
        _compress_kernel,
        out_shape=[jax.ShapeDtypeStruct((batch, N_KV, HEAD_DIM, nch), BF16),
                   jax.ShapeDtypeStruct((batch, N_KV, nch, LANES), BF16)],
        grid=(batch,),
        in_specs=[pl.BlockSpec((nch, width), lambda b: (b, 0)),
                  pl.BlockSpec((nch, width), lambda b: (b, 0)),
                  full(pos2), full(wa), full(wb), full(b1), full(w2), full(gain), full(cos_c), full(sin_c)],
        out_specs=[pl.BlockSpec((1, N_KV, HEAD_DIM, nch), lambda b: (b, 0, 0, 0)),
                   pl.BlockSpec((1, N_KV, nch, LANES), lambda b: (b, 0, 0, 0))],
        compiler_params=_cparams(("arbitrary",)),
        name="compress",
    )(kcm.reshape(batch * nch, width), vcm.reshape(batch * nch, width), pos2, wa, wb, b1, w2, gain, cos_c, sin_c)


def _attn_kernel(q_ref, gl_ref, ks_ref, vs_ref, kw_ref, vw_ref, kc_ref, vc_ref, ovl_ref, eneg_ref,
                 o_ref, qa_scr, pre_scr, m_scr, acc_scr, *, seq):
    H, QB, TK = HEADS_PER_KV, Q_TILE, KV_TILE
    q0 = pl.program_id(2) * QB
    ncp = seq // CMP_STRIDE
    nsel = seq // SEL_LEN
    n_top = min(SEL_TOP, nsel)
    nt_dims = (((1,), (1,)), ((), ()))

    def row_sum(a):
        return a[:, HEAD_DIM:HEAD_DIM + 1]

    def pair_keys(k_t, lead=None):
        z = jnp.zeros_like(k_t)
        head = [] if lead is None else [lead]
        return (jnp.concatenate(head + [k_t, z], axis=0), jnp.concatenate(head + [z, k_t], axis=0))

    q_pairs = [q_ref[:, j * LANES:(j + 1) * LANES] for j in range(H // 2)]
    qpos = q0 + lax.broadcasted_iota(I32, (QB, 1), 0)

    def one_shot_branch(k_t, v_aug, mask, zero_masked, between=None):
        k_var = pair_keys(k_t)
        score = lambda h: _bdot(q_pairs[h // 2], k_var[h % 2])
        probs, accs = [], []
        ahead = 3
        scores = [score(h) for h in range(ahead)]
        for h in range(H):
            s = jnp.where(mask, scores[h], NEG)
            scores[h] = None
            if h + ahead < H:
                scores.append(score(h + ahead))
            p = jnp.exp2(s - jnp.max(s, axis=-1, keepdims=True))
            if zero_masked:
                p = jnp.where(mask, p, 0.0)
            probs.append(p)
            accs.append(_bdot(p.astype(BF16), v_aug))
            if between is not None:
                between(h, probs, accs)
        return accs

    gs = _sigmoid(gl_ref[...])
    valid = (lax.broadcasted_iota(I32, (QB, ncp), 1) * CMP_STRIDE + (CMP_LEN - 1)) <= qpos
    imp_state = {"ps": None}

    def finish_cmp_head(h, probs, accs):
        l = row_sum(accs[h])
        inv = jnp.where(l > 0.0, 1.0 / l, 0.0)
        pre_scr[h] = accs[h] * (gs[:, h:h + 1] * inv)
        term = probs[h] * inv
        imp_state["ps"] = term if imp_state["ps"] is None else imp_state["ps"] + term
        probs[h] = None

    def after_cmp_head(h, probs, accs):
        if h > 0:
            finish_cmp_head(h - 1, probs, accs)
        if h == H - 1:
            finish_cmp_head(h, probs, accs)

    one_shot_branch(kc_ref[0, 0], vc_ref[0, 0], valid, True, between=after_cmp_head)
    ps = imp_state["ps"]

    ps_hi, ps_lo = _split(ps)
    imp = (lax.dot_general(ovl_ref[...], ps_hi, nt_dims, preferred_element_type=F32)
           + lax.dot_general(ovl_ref[...], ps_lo, nt_dims, preferred_element_type=F32))
    sidx = lax.broadcasted_iota(I32, (nsel, QB), 0)
    cur = (q0 + lax.broadcasted_iota(I32, (nsel, QB), 1)) // SEL_LEN
    causal = sidx <= cur
    forced = (sidx == cur) | (sidx == 0)
    pick_state = {"v": jnp.where(forced, FORCED_SCORE, jnp.where(causal, imp, -1.0)),
                  "sel": jnp.zeros((nsel, QB), jnp.bool_)}

    def pick_next():
        v = pick_state["v"]
        mx = jnp.max(v, axis=0, keepdims=True)
        first = jnp.min(jnp.where(v == mx, sidx, nsel), axis=0, keepdims=True)
        pick = sidx == first
        pick_state["sel"] = pick_state["sel"] | pick
        pick_state["v"] = jnp.where(pick, -3e38, v)

    def picks_after_head(h, probs, accs):
        probs[h] = None
        for _ in range(n_top // H + (1 if h < n_top % H else 0)):
            pick_next()

    span = WINDOW + QB
    base = pl.multiple_of(jnp.maximum(q0 - WINDOW, 0), LANES)
    dist = qpos - (base + lax.broadcasted_iota(I32, (QB, span), 1))
    inwin = (dist >= 0) & (dist < WINDOW)
    accs_w = one_shot_branch(kw_ref[0, 0, :, pl.ds(base, span)], vw_ref[0, 0, pl.ds(base, span), :],
                             inwin, False, between=picks_after_head)
    for h in range(H):
        pre_scr[h] = pre_scr[h] + accs_w[h] * (gs[:, 2 * H + h:2 * H + h + 1] / row_sum(accs_w[h]))

    notsel = jnp.where(pick_state["sel"] & causal, 0.0, 1.0).T.astype(BF16)
    for j in range(H // 2):
        qa_scr[j] = jnp.concatenate([notsel, q_pairs[j]], axis=1)

    m_scr[...] = jnp.full(m_scr.shape, NEG, F32)
    acc_scr[...] = jnp.zeros(acc_scr.shape, F32)

    def sel_tile(kt, causal_tile):
        off = pl.multiple_of(kt * TK, TK)
        k_var = pair_keys(ks_ref[0, 0, :, pl.ds(off, TK)], lead=eneg_ref[:, pl.ds(off, TK)])
        score = lambda h: _bdot(qa_scr[h // 2], k_var[h % 2])
        v_t = vs_ref[0, 0, pl.ds(off, TK), :]
        if causal_tile:
            keep = (off + lax.broadcasted_iota(I32, (QB, TK), 1)) <= qpos
        ahead = 4
        scores = [score(h) for h in range(ahead)]
        for h in range(H):
            s = scores[h]
            scores[h] = None
            if h + ahead < H:
                scores.append(score(h + ahead))
            if causal_tile:
                s = jnp.where(keep, s, NEG)
            m_old = m_scr[h]
            m_new = jnp.maximum(m_old, jnp.max(s, axis=-1, keepdims=True))
            alpha = jnp.exp2(m_old - m_new)
            pt = jnp.concatenate([jnp.exp2(s[:, c * LANES:(c + 1) * LANES] - m_new).astype(BF16)
                                  for c in range(TK // LANES)], axis=1)
            acc_scr[h] = alpha * acc_scr[h] + _bdot(pt, v_t)
            m_scr[h] = m_new

    n_tiles = (q0 + QB - 1) // TK + 1

    def body(kt, carry):
        sel_tile(kt, False)
        return carry

    lax.fori_loop(0, n_tiles - 1, body, 0)
    sel_tile(n_tiles - 1, True)

    def head_out(h):
        a_s = acc_scr[h]
        return pre_scr[h] + a_s * (gs[:, H + h:H + h + 1] / row_sum(a_s))

    low = lax.broadcasted_iota(I32, (QB, LANES), 1) < HEAD_DIM
    for j in range(H // 2):
        slab = jnp.where(low, head_out(2 * j), pltpu.roll(head_out(2 * j + 1), HEAD_DIM, 1))
        o_ref[:, j * LANES:(j + 1) * LANES] = slab.astype(BF16)


def _attention(q, gl, ks, vs, kw, vw, kc, vc, ovl, eneg, batch, seq):
    n = q.shape[0]
    QB = Q_TILE
    nq = seq // QB
    H = HEADS_PER_KV
    gw = HEADS_PER_KV * HEAD_DIM
    ncp = seq // CMP_STRIDE
    nsel = seq // SEL_LEN
    const2 = lambda a: pl.BlockSpec(a.shape, lambda b, g, j: (0,) * a.ndim)
    return pl.pallas_call(
        functools.partial(_attn_kernel, seq=seq),
        out_shape=jax.ShapeDtypeStruct((n, N_KV * gw), BF16),
        grid=(batch, N_KV, nq),
        in_specs=[pl.BlockSpec((QB, gw), lambda b, g, j: (b * nq + j, g)),
                  pl.BlockSpec((QB, LANES), lambda b, g, j: (b * nq + j, g)),
                  pl.BlockSpec((1, 1, HEAD_DIM, seq), lambda b, g, j: (b, g, 0, 0)),
                  pl.BlockSpec((1, 1, seq, LANES), lambda b, g, j: (b, g, 0, 0)),
                  pl.BlockSpec((1, 1, HEAD_DIM, seq), lambda b, g, j: (b, g, 0, 0)),
                  pl.BlockSpec((1, 1, seq, LANES), lambda b, g, j: (b, g, 0, 0)),
                  pl.BlockSpec((1, 1, HEAD_DIM, ncp), lambda b, g, j: (b, g, 0, 0)),
                  pl.BlockSpec((1, 1, ncp, LANES), lambda b, g, j: (b, g, 0, 0)),
                  const2(ovl), const2(eneg)],
        out_specs=pl.BlockSpec((QB, gw), lambda b, g, j: (b * nq + j, g)),
        scratch_shapes=[pltpu.VMEM((H // 2, QB, nsel + LANES), BF16), pltpu.VMEM((H, QB, LANES), F32),
                        pltpu.VMEM((H, QB, LANES), F32), pltpu.VMEM((H, QB, LANES), F32)],
        compiler_params=_cparams(("arbitrary", "arbitrary", "arbitrary")),
        name="sparse_attention",
    )(q, gl, ks, vs, kw, vw, kc, vc, ovl, eneg)


def _merge_kernel(x_ref, att_ref, ya_ref, mg_ref, g1_ref, wb_ref, wo_ref, o_ref):
    d = x_ref.shape[1]
    y_b = _bdot(att_ref[...], wb_ref[...])
    mg = mg_ref[...]
    merged = _sigmoid(mg[:, 0:d]) * ya_ref[...] + _sigmoid(mg[:, d:2 * d]) * y_b
    o_ref[...] = x_ref[...] + g1_ref[0] * _bdot(merged.astype(BF16), wo_ref[...])


def _merge(x2, att, y_a, mg, gate1, w_proj_b, w_out, seq):
    n, d = x2.shape
    tm = ROW_TILE
    tiles_per_b = seq // tm
    return pl.pallas_call(
        _merge_kernel,
        out_shape=jax.ShapeDtypeStruct((n, d), F32),
        grid=(n // tm,),
        in_specs=[pl.BlockSpec((tm, d), lambda i: (i, 0)),
                  pl.BlockSpec((tm, d), lambda i: (i, 0)),
                  pl.BlockSpec((tm, d), lambda i: (i, 0)),
                  pl.BlockSpec((tm, 2 * d), lambda i: (i, 0)),
                  pl.BlockSpec((1, 1, d), lambda i: (i // tiles_per_b, 0, 0)),
                  pl.BlockSpec(w_proj_b.shape, lambda i: (0, 0)),
                  pl.BlockSpec(w_out.shape, lambda i: (0, 0))],
        out_specs=pl.BlockSpec((tm, d), lambda i: (i, 0)),
        compiler_params=_cparams(("arbitrary",)),
        name="merge_out_projection",
    )(x2, att, y_a, mg, gate1[:, None, :], w_proj_b, w_out)


def _router_kernel(x_ref, g_ref, sc_ref, sh_ref, wh_ref, wl_ref, rb_ref, tri_ref,
                   h_ref, e_ref, gt_ref, pos_ref, cnt_ref, carry_ref):
    i = pl.program_id(0)
    tm, d = x_ref.shape

    @pl.when(i == 0)
    def _():
        carry_ref[...] = jnp.zeros((N_EXPERTS, 1), F32)

    x = x_ref[...]
    ms = jnp.mean(x * x, axis=-1, keepdims=True)
    h = x * lax.rsqrt(ms + EPS) * g_ref[...]
    h = h * (1.0 + sc_ref[0]) + sh_ref[0]
    for s in range(d // LANES):
        h_ref[pl.ds(s, tm, stride=SUBLANES), :] = h[:, s * LANES:(s + 1) * LANES]

    h_hi, h_lo = _split(h)
    nt = (((1,), (1,)), ((), ()))
    logits = (lax.dot_general(wh_ref[...], h_hi, nt, preferred_element_type=F32)
              + lax.dot_general(wh_ref[...], h_lo, nt, preferred_element_type=F32)
              + lax.dot_general(wl_ref[...], h_hi, nt, preferred_element_type=F32)
              + rb_ref[...])
    eidx = lax.broadcasted_iota(I32, (N_EXPERTS, tm), 0)
    onehot = jnp.zeros((N_EXPERTS, tm), F32)
    picks, vals, ids = [], [], []
    for _ in range(TOP_K):
        mx = jnp.max(logits, axis=0, keepdims=True)
        first = jnp.min(jnp.where(logits == mx, eidx, N_EXPERTS), axis=0, keepdims=True)
        pick = eidx == first
        picks.append(pick)
        vals.append(mx)
        ids.append(first)
        onehot = jnp.where(pick, 1.0, onehot)
        logits = jnp.where(pick, -3e38, logits)
    ex = [jnp.exp(vk - vals[0]) for vk in vals]
    den = ex[0] + ex[1] + ex[2] + ex[3]
    before = _bdot(onehot.astype(BF16), tri_ref[...]) + carry_ref[...]
    pad = SUBLANES - TOP_K
    pos = [jnp.sum(jnp.where(pk, before, 0.0), axis=0, keepdims=True) for pk in picks]
    e_ref[...] = jnp.concatenate(ids + [jnp.zeros((pad, tm), I32)], axis=0)
    gt_ref[...] = jnp.concatenate([e / den for e in ex] + [jnp.zeros((pad, tm), F32)], axis=0)
    pos_ref[...] = jnp.concatenate(pos + [jnp.zeros((pad, tm), F32)], axis=0).astype(I32)
    carry_ref[...] = carry_ref[...] + jnp.sum(onehot, axis=1, keepdims=True)
    cnt_ref[...] = carry_ref[...]


def _router(x2, gain, scale, shift, wr_hi, wr_lo, rb, tri, seq):
    n, d = x2.shape
    tm = ROUTE_TILE
    tiles_per_b = seq // tm
    nsub = d // LANES
    return pl.pallas_call(
        _router_kernel,
        out_shape=[jax.ShapeDtypeStruct((n * nsub, LANES), F32),
                   jax.ShapeDtypeStruct((SUBLANES, n), I32),
                   jax.ShapeDtypeStruct((SUBLANES, n), F32),
                   jax.ShapeDtypeStruct((SUBLANES, n), I32),
                   jax.ShapeDtypeStruct((N_EXPERTS, 1), F32)],
        grid=(n // tm,),
        in_specs=[pl.BlockSpec((tm, d), lambda i: (i, 0)),
                  pl.BlockSpec((1, d), lambda i: (0, 0)),
                  pl.BlockSpec((1, 1, d), lambda i: (i // tiles_per_b, 0, 0)),
                  pl.BlockSpec((1, 1, d), lambda i: (i // tiles_per_b, 0, 0)),
                  pl.BlockSpec(wr_hi.shape, lambda i: (0, 0)),
                  pl.BlockSpec(wr_lo.shape, lambda i: (0, 0)),
                  pl.BlockSpec((N_EXPERTS, 1), lambda i: (0, 0)),
                  pl.BlockSpec((tm, tm), lambda i: (0, 0))],
        out_specs=[pl.BlockSpec((tm * nsub, LANES), lambda i: (i, 0)),
                   pl.BlockSpec((SUBLANES, tm), lambda i: (0, i)),
                   pl.BlockSpec((SUBLANES, tm), lambda i: (0, i)),
                   pl.BlockSpec((SUBLANES, tm), lambda i: (0, i)),
                   pl.BlockSpec((N_EXPERTS, 1), lambda i: (0, 0))],
        scratch_shapes=[pltpu.VMEM((N_EXPERTS, 1), F32)],
        compiler_params=_cparams(("arbitrary",)),
        name="router",
    )(x2, gain.reshape(1, d), scale[:, None, :], shift[:, None, :], wr_hi, wr_lo, rb.reshape(N_EXPERTS, 1), tri)


def _dispatch_kernel(dest_ref, fill_ref, h_ref, xs_ref, zero_ref, sem, *, nsub):
    i = pl.program_id(0)
    tt = DISPATCH_TILE

    def row_copy(n, k):
        dst = dest_ref[0, 0, k * tt + n]
        return pltpu.make_async_copy(h_ref.at[pl.ds(n * nsub, nsub), :],
                                     xs_ref.at[pl.ds(dst * nsub, nsub), :], sem)

    def fill_copy(r):
        return pltpu.make_async_copy(zero_ref, xs_ref.at[pl.ds(r * nsub, nsub), :], sem)

    def for_fill_rows(fn):
        def per_range(e, carry):
            lax.fori_loop(fill_ref[2 * e], fill_ref[2 * e + 1], lambda r, c: (fn(fill_copy(r)), c)[1], 0)
            return carry

        lax.fori_loop(0, fill_ref.shape[0] // 2, per_range, 0)

    def for_token_rows(fn):
        def per_token(n, carry):
            for k in range(TOP_K):
                fn(row_copy(n, k))
            return carry

        lax.fori_loop(0, tt, per_token, 0)

    @pl.when(i == 0)
    def _():
        zero_ref[...] = jnp.zeros(zero_ref.shape, F32)
        for_fill_rows(lambda cp: cp.start())

    for_token_rows(lambda cp: cp.start())
    for_token_rows(lambda cp: cp.wait())

    @pl.when(i == 0)
    def _():
        for_fill_rows(lambda cp: cp.wait())


def _tile_indices(dest, tt):
    k, n = dest.shape
    return dest.reshape(k, n // tt, tt).transpose(1, 0, 2).reshape(n // tt, 1, k * tt)


def _dispatch(h_rows, dest, fill, n_rows, nsub):
    n_tok = h_rows.shape[0] // nsub
    tt = DISPATCH_TILE
    dest_tiles = _tile_indices(dest, tt)
    grid_spec = pltpu.PrefetchScalarGridSpec(
        num_scalar_prefetch=0,
        grid=(n_tok // tt,),
        in_specs=[pl.BlockSpec((1, 1, TOP_K * tt), lambda i: (i, 0, 0), memory_space=pltpu.SMEM),
                  pl.BlockSpec(memory_space=pltpu.SMEM),
                  pl.BlockSpec((tt * nsub, LANES), lambda i: (i, 0))],
        out_specs=pl.BlockSpec(memory_space=pl.ANY),
        scratch_shapes=[pltpu.VMEM((nsub, LANES), F32), pltpu.SemaphoreType.DMA(())],
    )
    return pl.pallas_call(
        functools.partial(_dispatch_kernel, nsub=nsub),
        out_shape=jax.ShapeDtypeStruct((n_rows * nsub, LANES), F32),
        grid_spec=grid_spec,
        compiler_params=pltpu.CompilerParams(dimension_semantics=("arbitrary",), vmem_limit_bytes=VMEM_LIMIT,
                                             has_side_effects=True),
        name="dispatch",
    )(dest_tiles, fill, h_rows)


PAIR = 2 * LANES


def _pair_permutation():
    p = np.zeros((PAIR, PAIR), np.float32)
    p[2 * np.arange(LANES), np.arange(LANES)] = 1.0
    p[2 * np.arange(LANES) + 1, LANES + np.arange(LANES)] = 1.0
    return p


def _regroup_kernel(w_ref, p_ref, o_ref):
    w = w_ref[0].astype(BF16)
    for j in range(w.shape[1] // PAIR):
        o_ref[0, :, j * PAIR:(j + 1) * PAIR] = _bdot(w[:, j * PAIR:(j + 1) * PAIR], p_ref[...]).astype(BF16)


def _regroup_gate_up(w_gate_up):
    depth, n_e, d, f2 = w_gate_up.shape
    tm = 512
    perm = jnp.asarray(_pair_permutation(), BF16)
    out = pl.pallas_call(
        _regroup_kernel,
        out_shape=jax.ShapeDtypeStruct((depth * n_e, d, f2), BF16),
        grid=(depth * n_e, d // tm),
        in_specs=[pl.BlockSpec((1, tm, f2), lambda e, i: (e, i, 0)),
                  pl.BlockSpec((PAIR, PAIR), lambda e, i: (0, 0))],
        out_specs=pl.BlockSpec((1, tm, f2), lambda e, i: (e, i, 0)),
        compiler_params=_cparams(("arbitrary", "arbitrary")),
        name="regroup_gate_up",
    )(w_gate_up.reshape(depth * n_e, d, f2), perm)
    return out.reshape(depth, n_e, d, f2)


def _regroup_bias(b):
    lead = b.shape[:-1]
    return b.reshape(*lead, -1, LANES, 2).swapaxes(-1, -2).reshape(*lead, -1)


def _expert_kernel(be_ref, nu_ref, xs_ref, wgu_ref, bgu_ref, wd_ref, bd_ref, ys_ref, *, nsub):
    i = pl.program_id(0)
    rows = EXPERT_ROWS
    f = wd_ref.shape[1]

    @pl.when(i < nu_ref[0])
    def _():
        x = jnp.concatenate([xs_ref[pl.ds(s, rows, stride=nsub), :] for s in range(nsub)], axis=-1)
        gu = _bdot(x.astype(BF16), wgu_ref[0]) + bgu_ref[0]
        acts = []
        for j in range(f // LANES):
            g_lin = jnp.minimum(gu[:, j * PAIR:j * PAIR + LANES], SWIGLU_LIMIT)
            u_lin = jnp.clip(gu[:, j * PAIR + LANES:(j + 1) * PAIR], -SWIGLU_LIMIT, SWIGLU_LIMIT)
            acts.append(((u_lin + 1.0) * g_lin * _sigmoid(g_lin * SWIGLU_ALPHA)).astype(BF16))
        act = jnp.concatenate(acts, axis=-1)
        y = _bdot(act, wd_ref[0]) + bd_ref[0]
        for s in range(nsub):
            ys_ref[pl.ds(s, rows, stride=nsub), :] = y[:, s * LANES:(s + 1) * LANES]

    @pl.when(i >= nu_ref[0])
    def _():
        ys_ref[...] = jnp.zeros(ys_ref.shape, F32)


def _experts(blk_e, n_used, xs, w_gu, b_gu, w_d, b_d, nsub):
    rows = EXPERT_ROWS
    n_blocks = xs.shape[0] // (rows * nsub)
    d, f2 = w_gu.shape[1], w_gu.shape[2]
    f = w_d.shape[1]
    row_map = lambda i, be, nu: (jnp.minimum(i, nu[0] - 1), 0)
    grid_spec = pltpu.PrefetchScalarGridSpec(
        num_scalar_prefetch=2,
        grid=(n_blocks,),
        in_specs=[pl.BlockSpec((rows * nsub, LANES), row_map),
                  pl.BlockSpec((1, d, f2), lambda i, be, nu: (be[i], 0, 0)),
                  pl.BlockSpec((1, 1, f2), lambda i, be, nu: (be[i], 0, 0)),
                  pl.BlockSpec((1, f, d), lambda i, be, nu: (be[i], 0, 0)),
                  pl.BlockSpec((1, 1, d), lambda i, be, nu: (be[i], 0, 0))],
        out_specs=pl.BlockSpec((rows * nsub, LANES), lambda i, be, nu: (i, 0)),
    )
    return pl.pallas_call(
        functools.partial(_expert_kernel, nsub=nsub),
        out_shape=jax.ShapeDtypeStruct(xs.shape, F32),
        grid_spec=grid_spec,
        compiler_params=_cparams(("arbitrary",)),
        name="experts",
    )(blk_e, n_used, xs, w_gu, b_gu, w_d, b_d)


def _combine_kernel(cur_ref, nxt_ref, ys_ref, x_ref, gt_ref, g2_ref, o_ref, buf_ref, sems, *, nsub, n_steps):
    i = pl.program_id(0)
    tt = COMBINE_TILE
    slot = i % 2

    def for_tile_rows(idx_ref, s, fn):
        def per_token(n, carry):
            for k in range(TOP_K):
                src = idx_ref[0, 0, k * tt + n]
                fn(pltpu.make_async_copy(ys_ref.at[pl.ds(src * nsub, nsub), :],
                                         buf_ref.at[s, k, pl.ds(n * nsub, nsub), :], sems.at[s]))
            return carry

        lax.fori_loop(0, tt, per_token, 0)

    @pl.when(i == 0)
    def _():
        for_tile_rows(cur_ref, 0, lambda cp: cp.start())

    @pl.when(i + 1 < n_steps)
    def _():
        for_tile_rows(nxt_ref, 1 - slot, lambda cp: cp.start())

    for_tile_rows(cur_ref, slot, lambda cp: cp.wait())
    gt = gt_ref[...]
    acc = None
    for k in range(TOP_K):
        yk = jnp.concatenate([buf_ref[slot, k, pl.ds(s, tt, stride=nsub), :] for s in range(nsub)], axis=-1)
        term = gt[:, k:k + 1] * yk
        acc = term if acc is None else acc + term
    o_ref[...] = x_ref[...] + g2_ref[0] * acc


def _combine(dest, ys, x2, gates_t, gate2, seq, nsub):
    n, d = x2.shape
    tt = COMBINE_TILE
    tiles_per_b = seq // tt
    n_steps = n // tt
    dest_tiles = _tile_indices(dest, tt)
    grid_spec = pltpu.PrefetchScalarGridSpec(
        num_scalar_prefetch=0,
        grid=(n_steps,),
        in_specs=[pl.BlockSpec((1, 1, TOP_K * tt), lambda i: (i, 0, 0), memory_space=pltpu.SMEM),
                  pl.BlockSpec((1, 1, TOP_K * tt), lambda i: (jnp.minimum(i + 1, n_steps - 1), 0, 0),
                               memory_space=pltpu.SMEM),
                  pl.BlockSpec(memory_space=pl.ANY),
                  pl.BlockSpec((tt, d), lambda i: (i, 0)),
                  pl.BlockSpec((tt, SUBLANES), lambda i: (i, 0)),
                  pl.BlockSpec((1, 1, d), lambda i: (i // tiles_per_b, 0, 0))],
        out_specs=pl.BlockSpec((tt, d), lambda i: (i, 0)),
        scratch_shapes=[pltpu.VMEM((2, TOP_K, tt * nsub, LANES), F32), pltpu.SemaphoreType.DMA((2,))],
    )
    return pl.pallas_call(
        functools.partial(_combine_kernel, nsub=nsub, n_steps=n_steps),
        out_shape=jax.ShapeDtypeStruct((n, d), F32),
        grid_spec=grid_spec,
        compiler_params=_cparams(("arbitrary",)),
        name="combine",
    )(dest_tiles, dest_tiles, ys, x2, gates_t, gate2[:, None, :])


def _attention_constants(seq):
    nsel = seq // SEL_LEN
    ncp = seq // CMP_STRIDE
    cmp_start = np.arange(ncp) * CMP_STRIDE
    sel_start = np.arange(nsel) * SEL_LEN
    ovl = ((cmp_start[:, None] < sel_start[None, :] + SEL_LEN)
           & (cmp_start[:, None] + CMP_LEN - 1 >= sel_start[None, :])).astype(np.float32)
    ovl[ncp - 1, :] = 0.0
    eneg = np.where(np.arange(seq)[None, :] // SEL_LEN == np.arange(nsel)[:, None], NEG, 0.0).astype(np.float32)
    return jnp.asarray(ovl.T, BF16), jnp.asarray(eneg, BF16)


def _layer(x2, mod, consts, p, batch, seq):
    n, d = x2.shape
    shift1, scale1, gate1, shift2, scale2, gate2 = [mod[:, k * d:(k + 1) * d] for k in range(6)]
    (cos2, sin2, cos_c, sin_c, ovl, eneg, tri) = consts

    y_a, q, kcm, vcm, ks, vs, kw, vw, gl, mg = _in_projection(
        x2, p["norm1"], scale1, shift1, p["q_gain2"], p["k_gain2"], cos2, sin2, p["w_in_parts"],
        p["w_pool"], p["pool_scale"], p["w_proj_a"], batch, seq)
    kc, vc = _compress(kcm, vcm, p["cmp_pos2"], p["cmp_wa"], p["cmp_wb"], p["cmp_b1"], p["cmp_w2"],
                       p["kc_gain"], cos_c, sin_c, batch, seq)
    att = _attention(q, gl, ks, vs, kw, vw, kc, vc, ovl, eneg, batch, seq)
    x2 = _merge(x2, att, y_a, mg, gate1, p["w_proj_b"], p["w_out"], seq)

    nsub = d // LANES
    h_rows, top_e, gates, pos, counts = _router(x2, p["norm2"], scale2, shift2, p["wr_hi"], p["wr_lo"],
                                                p["router_b"], tri, seq)
    rows = EXPERT_ROWS
    n_blocks = -(-n * TOP_K // rows) + N_EXPERTS
    cnt = counts[:, 0].astype(I32)
    padded = (cnt + rows - 1) // rows * rows
    pad_end = jnp.cumsum(padded)
    pad_start = pad_end - padded
    e_ids = jnp.arange(N_EXPERTS, dtype=I32)
    dest = jnp.sum(jnp.where(top_e[:TOP_K, :, None] == e_ids, pad_start, 0), axis=-1) + pos[:TOP_K]
    fill_lo = jnp.concatenate([pad_start + cnt, pad_end[-1:]])
    fill_hi = jnp.concatenate([pad_end, jnp.full((1,), n_blocks * rows, I32)])
    fill = jnp.stack([fill_lo, fill_hi], axis=1).reshape(-1).astype(I32)
    blk_first = jnp.arange(n_blocks, dtype=I32)[:, None] * rows
    blk_e = jnp.minimum(jnp.sum((pad_end[None, :] <= blk_first).astype(I32), axis=1), N_EXPERTS - 1)
    n_used = (pad_end[-1:] // rows).astype(I32)
    xs = _dispatch(h_rows, dest, fill, n_blocks * rows, nsub)
    ys = _experts(blk_e, n_used, xs, p["w_gu"], p["b_gu"], p["w_d"], p["b_d"], nsub)
    gates_t = gates.T
    return _combine(dest, ys, x2, gates_t, gate2, seq, nsub)


def _prep_layer(l, norm1, norm2, w_in, w_pool, pool_scale, q_norm, k_norm, cmp_pos, cmp_w1, cmp_b1, cmp_w2,
                w_proj_a, w_proj_b, w_out, router_w, router_b, w_gate_up, b_gate_up, w_down, b_down):
    d = w_in.shape[1]
    w = w_in[l]
    o_q = POOL_WIDTH
    o_kv = o_q + N_KV * HEADS_PER_KV * HEAD_DIM
    o_g = o_kv + 6 * LANES
    n_gate = 3 * N_KV * HEADS_PER_KV
    o_m = o_g + n_gate
    wg = w[:, o_g:o_m].reshape(d, 3, N_KV, HEADS_PER_KV).transpose(0, 2, 1, 3).reshape(d, N_KV, 3 * HEADS_PER_KV)
    wg = jnp.pad(wg, ((0, 0), (0, 0), (0, LANES - 3 * HEADS_PER_KV))).reshape(d, N_KV * LANES)
    parts = [w[:, 0:o_q], w[:, o_q:o_kv], w[:, o_kv:o_kv + LANES], w[:, o_kv + LANES:o_kv + 2 * LANES],
             w[:, o_kv + 2 * LANES:o_g], wg, w[:, o_m:]]
    w_in_parts = [a.astype(BF16) for a in parts]

    half = CMP_LEN // 2
    w1 = cmp_w1[l].reshape(2, CMP_LEN, HEAD_DIM, CMP_HIDDEN)

    def chunk_weight(wh):
        z = jnp.zeros_like(wh)
        g0 = jnp.stack([wh, z], axis=2)
        g1 = jnp.stack([z, wh], axis=2)
        return jnp.concatenate([g0, g1], axis=-1).reshape(2, half * LANES, N_KV * CMP_HIDDEN)

    cmp_wa = chunk_weight(w1[:, :half]).astype(BF16)
    cmp_wb = chunk_weight(w1[:, half:]).astype(BF16)
    pos = cmp_pos[l]
    pos_t = jnp.broadcast_to(pos[:, :, None, :], (2, CMP_LEN, N_KV, HEAD_DIM))
    cmp_pos2 = jnp.stack([pos_t[:, :half].reshape(2, half * LANES), pos_t[:, half:].reshape(2, half * LANES)], axis=1)
    b1 = jnp.tile(cmp_b1[l], (1, N_KV))[:, None, :]
    w2 = cmp_w2[l]
    z2 = jnp.zeros_like(w2)
    cmp_w2b = jnp.concatenate([jnp.concatenate([w2, z2], axis=-1), jnp.concatenate([z2, w2], axis=-1)],
                              axis=1).astype(BF16)
    wr = router_w[l].T
    wr_hi = wr.astype(BF16)
    wr_lo = (wr - wr_hi.astype(F32)).astype(BF16)
    w_gu = w_gate_up[l]
    b_gu = _regroup_bias(b_gate_up[l])[:, None, :]
    return dict(
        norm1=norm1[l], norm2=norm2[l], w_in_parts=w_in_parts,
        w_pool=w_pool[l].astype(BF16), pool_scale=pool_scale[l], w_proj_a=w_proj_a[l].astype(BF16),
        k_gain2=jnp.stack([jnp.tile(k_norm[l, 1], N_KV), jnp.tile(k_norm[l, 2], N_KV)], axis=0),
        kc_gain=jnp.tile(k_norm[l, 0], N_KV)[None, :], q_gain2=jnp.tile(q_norm[l], 2)[None, :],
        cmp_pos2=cmp_pos2, cmp_wa=cmp_wa, cmp_wb=cmp_wb, cmp_b1=b1, cmp_w2=cmp_w2b,
        w_proj_b=w_proj_b[l].astype(BF16), w_out=w_out[l].astype(BF16),
        wr_hi=wr_hi, wr_lo=wr_lo, router_b=router_b[l],
        w_gu=w_gu, b_gu=b_gu, w_d=w_down[l].astype(BF16), b_d=b_down[l][:, None, :],
    )


def kernel(x, c, norm1, norm2, ada_w, ada_b, w_in, w_pool, pool_scale, q_norm, k_norm, cmp_pos, cmp_w1, cmp_b1,
           cmp_w2, w_proj_a, w_proj_b, w_out, router_w, router_b, w_gate_up, b_gate_up, w_down, b_down):
    batch, seq, d = x.shape
    depth = norm1.shape[0]
    assert seq % KV_TILE == 0 and seq % ROUTE_TILE == 0 and seq >= WINDOW + Q_TILE and d % LANES == 0
    assert (batch * seq * TOP_K) % EXPERT_ROWS == 0 and (batch * seq) % DISPATCH_TILE == 0
    assert seq % COMBINE_TILE == 0 and seq % Q_TILE == 0 and seq % ROW_TILE == 0

    pos = np.arange(seq)
    cq, sq = _rope_tables(pos)
    cc, sc = _rope_tables(np.arange(seq // CMP_STRIDE) * CMP_STRIDE + CMP_LEN - 1)
    tile2 = lambda a: jnp.asarray(np.concatenate([a, a], axis=1))
    tri = jnp.asarray(np.triu(np.ones((ROUTE_TILE, ROUTE_TILE), np.float32), 1), BF16)
    consts = (tile2(cq), tile2(sq), tile2(cc), tile2(sc), *_attention_constants(seq), tri)

    c_pad = jnp.pad(c, ((0, SUBLANES - batch % SUBLANES if batch % SUBLANES else 0), (0, 0)))
    x2 = x.reshape(batch * seq, d)
    w_gate_up = _regroup_gate_up(w_gate_up)
    for l in range(depth):
        mod = _modulation(c_pad, ada_w[l], ada_b[l])[:batch]
        p = _prep_layer(l, norm1, norm2, w_in, w_pool, pool_scale, q_norm, k_norm, cmp_pos, cmp_w1, cmp_b1, cmp_w2,
                        w_proj_a, w_proj_b, w_out, router_w, router_b, w_gate_up, b_gate_up, w_down, b_down)
        x2 = _layer(x2, mod, consts, p, batch, seq)
    return x2.reshape(batch, seq, d)
```

```python
import functools

import numpy as np
import jax
import jax.numpy as jnp
from jax import lax
from jax.experimental import pallas as pl
from jax.experimental.pallas import tpu as pltpu

F32 = jnp.float32
BF16 = jnp.bfloat16
I32 = jnp.int32

POOL_WINDOWS = (2, 4, 8, 16)
POOL_GROUP = 128
POOL_WIDTH = 512
HEAD_DIM = 64
N_KV = 2
HEADS_PER_KV = 8
ROPE_DIM = 16
ROPE_HALF = 8
ROPE_THETA = 500000.0
CMP_LEN = 32
CMP_STRIDE = 16
CMP_HIDDEN = 256
SEL_LEN = 64
SEL_TOP = 16
WINDOW = 512
FORCED_SCORE = 1e9
N_EXPERTS = 32
TOP_K = 4
SWIGLU_LIMIT = 7.0
SWIGLU_ALPHA = 1.702
EPS = 1e-6
NEG = -1e30
QK_SCALE = HEAD_DIM ** -0.5 * 1.4426950408889634

LANES = 128
SUBLANES = 8
VMEM_LIMIT = 56 * 1024 * 1024

ROW_TILE = 256
POOL_HALO = 16
Q_TILE = 256
KV_TILE = 512
ROUTE_TILE = 512
EXPERT_ROWS = 512
DISPATCH_TILE = 1024
COMBINE_TILE = 256


def _cparams(sem):
    return pltpu.CompilerParams(dimension_semantics=sem, vmem_limit_bytes=VMEM_LIMIT)


def _bdot(a, b):
    return jnp.dot(a, b, preferred_element_type=F32)


def _split(a):
    hi = a.astype(BF16)
    lo = (a - hi.astype(F32)).astype(BF16)
    return hi, lo


def _sigmoid(x):
    return 1.0 / (1.0 + jnp.exp(-x))


def _mod_kernel(c_ref, w_ref, b_ref, o_ref):
    c = c_ref[...]
    a = c * _sigmoid(c)
    a_hi, a_lo = _split(a)
    w_hi, w_lo = _split(w_ref[...])
    o_ref[...] = _bdot(a_hi, w_hi) + _bdot(a_lo, w_hi) + _bdot(a_hi, w_lo) + b_ref[...]


def _modulation(c_pad, ada_w, ada_b):
    rows, d = c_pad.shape
    n = ada_w.shape[1]
    tn = 512
    return pl.pallas_call(
        _mod_kernel,
        out_shape=jax.ShapeDtypeStruct((rows, n), F32),
        grid=(n // tn,),
        in_specs=[pl.BlockSpec((rows, d), lambda i: (0, 0)),
                  pl.BlockSpec((d, tn), lambda i: (0, i)),
                  pl.BlockSpec((1, tn), lambda i: (0, i))],
        out_specs=pl.BlockSpec((rows, tn), lambda i: (0, i)),
        compiler_params=_cparams(("arbitrary",)),
        name="modulation",
    )(c_pad, ada_w, ada_b.reshape(1, n))


def _pool_mixer(u, t_in_seq, wp_ref, ps_ref, wa_ref, ext_ref):
    ts = u.shape[0]

    @pl.when(t_in_seq == 0)
    def _():
        ext_ref[0:POOL_HALO, :] = jnp.zeros((POOL_HALO, POOL_WIDTH), F32)

    @pl.when(t_in_seq > 0)
    def _():
        ext_ref[0:POOL_HALO, :] = ext_ref[ts:ts + POOL_HALO, :]

    ext_ref[POOL_HALO:POOL_HALO + ts, :] = u
    t = t_in_seq * ts + lax.broadcasted_iota(I32, (ts, 1), 0)
    outs = []
    for gi, win in enumerate(POOL_WINDOWS):
        lo, hi = gi * POOL_GROUP, (gi + 1) * POOL_GROUP
        ug = u[:, lo:hi]
        total = ug
        for jj in range(1, win):
            total = total + ext_ref[POOL_HALO - jj:POOL_HALO - jj + ts, lo:hi]
        cnt = jnp.minimum(t + 1, win).astype(F32)
        dlt = total / cnt - ug
        outs.append(_bdot(dlt.astype(BF16), wp_ref[gi]))
    y = jnp.concatenate(outs, axis=-1) * ps_ref[...]
    return _bdot(y.astype(BF16), wa_ref[...])


def _inproj_kernel(x_ref, g_ref, sc_ref, sh_ref, qg_ref, kg_ref, cos_ref, sin_ref,
                   wu_ref, wq_ref, wkc_ref, wvc_ref, wkv_ref, wgl_ref, wmg_ref, wp_ref, ps_ref, wa_ref,
                   ya_ref, q_ref, kc_ref, vc_ref, ks_ref, vs_ref, kw_ref, vw_ref, gl_ref, mg_ref,
                   ext_ref, *, tiles_per_seq):
    x = x_ref[...]
    ms = jnp.mean(x * x, axis=-1, keepdims=True)
    h = x * lax.rsqrt(ms + EPS) * g_ref[...]
    h = h * (1.0 + sc_ref[0]) + sh_ref[0]
    hb = h.astype(BF16)
    cos, sin = cos_ref[...], sin_ref[...]

    yq = _bdot(hb, wq_ref[...])
    kv = _bdot(hb, wkv_ref[...])

    for j in range(yq.shape[1] // LANES):
        q2 = _norm_rope_pair(yq[:, j * LANES:(j + 1) * LANES], qg_ref[...], cos, sin)
        q_ref[:, j * LANES:(j + 1) * LANES] = (q2 * QK_SCALE).astype(BF16)

    mg_ref[...] = _bdot(hb, wmg_ref[...])
    u = _bdot(hb, wu_ref[...])

    ks = _norm_rope_pair(kv[:, 0:LANES], kg_ref[0:1, :], cos, sin).T.astype(BF16)
    kw = _norm_rope_pair(kv[:, 2 * LANES:3 * LANES], kg_ref[1:2, :], cos, sin).T.astype(BF16)
    for gi in range(N_KV):
        ks_ref[0, gi] = ks[gi * HEAD_DIM:(gi + 1) * HEAD_DIM, :]
        kw_ref[0, gi] = kw[gi * HEAD_DIM:(gi + 1) * HEAD_DIM, :]
        vs_ref[0, gi] = _values_with_ones(kv[:, LANES:2 * LANES], gi)
        vw_ref[0, gi] = _values_with_ones(kv[:, 3 * LANES:4 * LANES], gi)

    kc_ref[...] = _bdot(hb, wkc_ref[...])
    vc_ref[...] = _bdot(hb, wvc_ref[...])
    gl_ref[...] = _bdot(hb, wgl_ref[...])
    ya_ref[...] = _pool_mixer(u, pl.program_id(0) % tiles_per_seq, wp_ref, ps_ref, wa_ref, ext_ref)


def _in_projection(x2, gain, scale, shift, q_gain2, k_gain2, cos2, sin2, weights, w_pool, pool_scale, w_proj_a,
                   batch, seq):
    n, d = x2.shape
    tm = ROW_TILE
    tps = seq // tm
    w_u, w_q, w_kc, w_vc, w_kv, w_gl, w_mg = weights
    const = lambda a: pl.BlockSpec(a.shape, lambda i: (0,) * a.ndim)
    rows = lambda width: pl.BlockSpec((tm, width), lambda i: (i, 0))
    per_b = pl.BlockSpec((1, 1, d), lambda i: (i // tps, 0, 0))
    table = pl.BlockSpec((tm, LANES), lambda i: (i % tps, 0))
    kt_shape = jax.ShapeDtypeStruct((batch, N_KV, HEAD_DIM, seq), BF16)
    v_shape = jax.ShapeDtypeStruct((batch, N_KV, seq, LANES), BF16)
    kt_spec = pl.BlockSpec((1, N_KV, HEAD_DIM, tm), lambda i: (i // tps, 0, 0, i % tps))
    v_spec = pl.BlockSpec((1, N_KV, tm, LANES), lambda i: (i // tps, 0, i % tps, 0))
    f32_out = lambda width: jax.ShapeDtypeStruct((n, width), F32)
    pool_scale2 = pool_scale.reshape(1, POOL_WIDTH)
    return pl.pallas_call(
        functools.partial(_inproj_kernel, tiles_per_seq=tps),
        out_shape=[f32_out(w_proj_a.shape[1]), jax.ShapeDtypeStruct((n, w_q.shape[1]), BF16),
                   f32_out(w_kc.shape[1]), f32_out(w_vc.shape[1]), kt_shape, v_shape, kt_shape, v_shape,
                   f32_out(w_gl.shape[1]), f32_out(w_mg.shape[1])],
        grid=(n // tm,),
        in_specs=[rows(d), const(gain.reshape(1, d)), per_b, per_b, const(q_gain2), const(k_gain2), table, table,
                  const(w_u), const(w_q), const(w_kc), const(w_vc), const(w_kv), const(w_gl), const(w_mg),
                  const(w_pool), const(pool_scale2), const(w_proj_a)],
        out_specs=[rows(w_proj_a.shape[1]), rows(w_q.shape[1]), rows(w_kc.shape[1]), rows(w_vc.shape[1]),
                   kt_spec, v_spec, kt_spec, v_spec, rows(w_gl.shape[1]), rows(w_mg.shape[1])],
        scratch_shapes=[pltpu.VMEM((tm + POOL_HALO, POOL_WIDTH), F32)],
        compiler_params=_cparams(("arbitrary",)),
        name="in_projection",
    )(x2, gain.reshape(1, d), scale[:, None, :], shift[:, None, :], q_gain2, k_gain2, cos2, sin2,
      w_u, w_q, w_kc, w_vc, w_kv, w_gl, w_mg, w_pool, pool_scale2, w_proj_a)


def _rope_tables(pos):
    inv_freq = np.float32(ROPE_THETA) ** (-(np.arange(ROPE_HALF, dtype=np.float32)) / np.float32(ROPE_HALF))
    ang = pos.astype(np.float32)[:, None] * inv_freq.astype(np.float32)[None, :]
    cos, sin = np.cos(ang), np.sin(ang)
    c = np.ones((pos.shape[0], HEAD_DIM), np.float32)
    s = np.zeros((pos.shape[0], HEAD_DIM), np.float32)
    c[:, :ROPE_HALF] = cos
    c[:, ROPE_HALF:ROPE_DIM] = cos
    s[:, :ROPE_HALF] = -sin
    s[:, ROPE_HALF:ROPE_DIM] = sin
    return c, s


def _norm_rope_pair(x, gain, cos, sin):
    rows = x.shape[0]
    lane = lax.broadcasted_iota(I32, (rows, LANES), 1)
    first = lane < HEAD_DIM
    x2 = x * x
    s0 = jnp.sum(jnp.where(first, x2, 0.0), axis=-1, keepdims=True)
    s1 = jnp.sum(jnp.where(first, 0.0, x2), axis=-1, keepdims=True)
    r = lax.rsqrt(jnp.where(first, s0, s1) * (1.0 / HEAD_DIM) + EPS)
    y = x * r * gain
    low = (lane & (HEAD_DIM - 1)) < ROPE_HALF
    sw = jnp.where(low, pltpu.roll(y, LANES - ROPE_HALF, 1), pltpu.roll(y, ROPE_HALF, 1))
    return y * cos + sw * sin


def _values_with_ones(v, gi):
    lane = lax.broadcasted_iota(I32, v.shape, 1)
    vg = v if gi == 0 else pltpu.roll(v, HEAD_DIM, 1)
    return jnp.where(lane < HEAD_DIM, vg, 1.0).astype(BF16)


def _compress_kernel(k_ref, v_ref, pos_ref, wa_ref, wb_ref, b1_ref, w2_ref, gain_ref, cos_ref, sin_ref,
                     kc_ref, vc_ref):
    nch = k_ref.shape[0]

    def mlp(x, idx):
        xa = (x + pos_ref[idx, 0:1, :]).astype(BF16)
        xb = (x + pos_ref[idx, 1:2, :]).astype(BF16)
        a = _bdot(xa, wa_ref[idx])
        b = _bdot(xb, wb_ref[idx])
        b_next = jnp.concatenate([b[1:nch, :], jnp.zeros((1, b.shape[1]), F32)], axis=0)
        pre = a + b_next + b1_ref[idx]
        hid = pre * _sigmoid(pre)
        return _bdot(hid.astype(BF16), w2_ref[idx])

    kc = _norm_rope_pair(mlp(k_ref[...], 0), gain_ref[...], cos_ref[...], sin_ref[...]).T.astype(BF16)
    vc = mlp(v_ref[...], 1)
    for gi in range(N_KV):
        kc_ref[0, gi] = kc[gi * HEAD_DIM:(gi + 1) * HEAD_DIM, :]
        vc_ref[0, gi] = _values_with_ones(vc, gi)


def _compress(kcm, vcm, pos2, wa, wb, b1, w2, gain, cos_c, sin_c, batch, seq):
    nch = seq // CMP_STRIDE
    width = CMP_STRIDE * LANES
    full = lambda a: pl.BlockSpec(a.shape, lambda b: (0,) * a.ndim)
    return pl.pallas_call(
        _compress_kernel,
        out_shape=[jax.ShapeDtypeStruct((batch, N_KV, HEAD_DIM, nch), BF16),
                   jax.ShapeDtypeStruct((batch, N_KV, nch, LANES), BF16)],
        grid=(batch,),
        in_specs=[pl.BlockSpec((nch, width), lambda b: (b, 0)),
                  pl.BlockSpec((nch, width), lambda b: (b, 0)),
                  full(pos2), full(wa), full(wb), full(b1), full(w2), full(gain), full(cos_c), full(sin_c)],
        out_specs=[pl.BlockSpec((1, N_KV, HEAD_DIM, nch), lambda b: (b, 0, 0, 0)),
                   pl.BlockSpec((1, N_KV, nch, LANES), lambda b: (b, 0, 0, 0))],
        compiler_params=_cparams(("arbitrary",)),
        name="compress",
    )(kcm.reshape(batch * nch, width), vcm.reshape(batch * nch, width), pos2, wa, wb, b1, w2, gain, cos_c, sin_c)


def _attn_kernel(q_ref, gl_ref, ks_ref, vs_ref, kw_ref, vw_ref, kc_ref, vc_ref, ovl_ref, eneg_ref,
                 o_ref, qa_scr, pre_scr, m_scr, acc_scr, *, seq):
    H, QB, TK = HEADS_PER_KV, Q_TILE, KV_TILE
    q0 = pl.program_id(2) * QB
    ncp = seq // CMP_STRIDE
    nsel = seq // SEL_LEN
    n_top = min(SEL_TOP, nsel)
    nt_dims = (((1,), (1,)), ((), ()))

    def row_sum(a):
        return a[:, HEAD_DIM:HEAD_DIM + 1]

    def pair_keys(k_t, lead=None):
        z = jnp.zeros_like(k_t)
        head = [] if lead is None else [lead]
        return (jnp.concatenate(head + [k_t, z], axis=0), jnp.concatenate(head + [z, k_t], axis=0))

    q_pairs = [q_ref[:, j * LANES:(j + 1) * LANES] for j in range(H // 2)]
    qpos = q0 + lax.broadcasted_iota(I32, (QB, 1), 0)

    def one_shot_branch(k_t, v_aug, mask, zero_masked, between=None):
        k_var = pair_keys(k_t)
        score = lambda h: _bdot(q_pairs[h // 2], k_var[h % 2])
        probs, accs = [], []
        ahead = 1
        scores = [score(h) for h in range(ahead)]
        for h in range(H):
            s = jnp.where(mask, scores[h], NEG)
            scores[h] = None
            if h + ahead < H:
                scores.append(score(h + ahead))
            p = jnp.exp2(s - jnp.max(s, axis=-1, keepdims=True))
            if zero_masked:
                p = jnp.where(mask, p, 0.0)
            probs.append(p)
            accs.append(_bdot(p.astype(BF16), v_aug))
            if between is not None:
                between(h, probs, accs)
        return accs

    gs = _sigmoid(gl_ref[...])

    def compressed_branch(width):
        valid = (lax.broadcasted_iota(I32, (QB, width), 1) * CMP_STRIDE + (CMP_LEN - 1)) <= qpos
        imp_state = {"ps": None}

        def finish_cmp_head(h, probs, accs):
            l = row_sum(accs[h])
            inv = jnp.where(l > 0.0, 1.0 / l, 0.0)
            pre_scr[h] = accs[h] * (gs[:, h:h + 1] * inv)
            term = probs[h] * inv
            imp_state["ps"] = term if imp_state["ps"] is None else imp_state["ps"] + term
            probs[h] = None

        def after_cmp_head(h, probs, accs):
            if h > 0:
                finish_cmp_head(h - 1, probs, accs)
            if h == H - 1:
                finish_cmp_head(h, probs, accs)

        one_shot_branch(kc_ref[0, 0, :, 0:width], vc_ref[0, 0, 0:width, :], valid, True, between=after_cmp_head)
        ps_hi, ps_lo = _split(imp_state["ps"])
        ovl_t = ovl_ref[:, 0:width]
        return (lax.dot_general(ovl_t, ps_hi, nt_dims, preferred_element_type=F32)
                + lax.dot_general(ovl_t, ps_lo, nt_dims, preferred_element_type=F32))

    imp = compressed_branch(ncp)

    sidx = lax.broadcasted_iota(I32, (nsel, QB), 0)
    cur = (q0 + lax.broadcasted_iota(I32, (nsel, QB), 1)) // SEL_LEN
    causal = sidx <= cur
    forced = (sidx == cur) | (sidx == 0)
    pick_state = {"v": jnp.where(forced, FORCED_SCORE, jnp.where(causal, imp, -1.0)),
                  "sel": jnp.zeros((nsel, QB), jnp.bool_)}

    def pick_next():
        v = pick_state["v"]
        mx = jnp.max(v, axis=0, keepdims=True)
        first = jnp.min(jnp.where(v == mx, sidx, nsel), axis=0, keepdims=True)
        pick = sidx == first
        pick_state["sel"] = pick_state["sel"] | pick
        pick_state["v"] = jnp.where(pick, -3e38, v)

    def picks_after_head(h, probs, accs):
        probs[h] = None
        for _ in range(n_top // H + (1 if h < n_top % H else 0)):
            pick_next()

    span = WINDOW + QB
    base = pl.multiple_of(jnp.maximum(q0 - WINDOW, 0), LANES)
    dist = qpos - (base + lax.broadcasted_iota(I32, (QB, span), 1))
    inwin = (dist >= 0) & (dist < WINDOW)
    accs_w = one_shot_branch(kw_ref[0, 0, :, pl.ds(base, span)], vw_ref[0, 0, pl.ds(base, span), :],
                             inwin, False, between=picks_after_head)
    for h in range(H):
        pre_scr[h] = pre_scr[h] + accs_w[h] * (gs[:, 2 * H + h:2 * H + h + 1] / row_sum(accs_w[h]))

    notsel = jnp.where(pick_state["sel"] & causal, 0.0, 1.0).T.astype(BF16)
    for j in range(H // 2):
        qa_scr[j] = jnp.concatenate([notsel, q_pairs[j]], axis=1)

    m_scr[...] = jnp.full(m_scr.shape, NEG, F32)
    acc_scr[...] = jnp.zeros(acc_scr.shape, F32)

    def sel_tile(kt, causal_tile):
        off = pl.multiple_of(kt * TK, TK)
        k_var = pair_keys(ks_ref[0, 0, :, pl.ds(off, TK)], lead=eneg_ref[:, pl.ds(off, TK)])
        score = lambda h: _bdot(qa_scr[h // 2], k_var[h % 2])
        v_t = vs_ref[0, 0, pl.ds(off, TK), :]
        if causal_tile:
            keep = (off + lax.broadcasted_iota(I32, (QB, TK), 1)) <= qpos
        ahead = 4
        scores = [score(h) for h in range(ahead)]
        for h in range(H):
            s = scores[h]
            scores[h] = None
            if h + ahead < H:
                scores.append(score(h + ahead))
            if causal_tile:
                s = jnp.where(keep, s, NEG)
            m_old = m_scr[h]
            m_new = jnp.maximum(m_old, jnp.max(s, axis=-1, keepdims=True))
            alpha = jnp.exp2(m_old - m_new)
            pt = jnp.concatenate([jnp.exp2(s[:, c * LANES:(c + 1) * LANES] - m_new).astype(BF16)
                                  for c in range(TK // LANES)], axis=1)
            acc_scr[h] = alpha * acc_scr[h] + _bdot(pt, v_t)
            m_scr[h] = m_new

    n_tiles = (q0 + QB - 1) // TK + 1

    def body(kt, carry):
        sel_tile(kt, False)
        return carry

    lax.fori_loop(0, n_tiles - 1, body, 0)
    sel_tile(n_tiles - 1, True)

    def head_out(h):
        a_s = acc_scr[h]
        return pre_scr[h] + a_s * (gs[:, H + h:H + h + 1] / row_sum(a_s))

    low = lax.broadcasted_iota(I32, (QB, LANES), 1) < HEAD_DIM
    for j in range(H // 2):
        slab = jnp.where(low, head_out(2 * j), pltpu.roll(head_out(2 * j + 1), HEAD_DIM, 1))
        o_ref[:, j * LANES:(j + 1) * LANES] = slab.astype(BF16)


def _attention(q, gl, ks, vs, kw, vw, kc, vc, ovl, eneg, batch, seq):
    n = q.shape[0]
    QB = Q_TILE
    nq = seq // QB
    H = HEADS_PER_KV
    gw = HEADS_PER_KV * HEAD_DIM
    ncp = seq // CMP_STRIDE
    nsel = seq // SEL_LEN
    const2 = lambda a: pl.BlockSpec(a.shape, lambda b, g, j: (0,) * a.ndim)
    return pl.pallas_call(
        functools.partial(_attn_kernel, seq=seq),
        out_shape=jax.ShapeDtypeStruct((n, N_KV * gw), BF16),
        grid=(batch, N_KV, nq),
        in_specs=[pl.BlockSpec((QB, gw), lambda b, g, j: (b * nq + j, g)),
                  pl.BlockSpec((QB, LANES), lambda b, g, j: (b * nq + j, g)),
                  pl.BlockSpec((1, 1, HEAD_DIM, seq), lambda b, g, j: (b, g, 0, 0)),
                  pl.BlockSpec((1, 1, seq, LANES), lambda b, g, j: (b, g, 0, 0)),
                  pl.BlockSpec((1, 1, HEAD_DIM, seq), lambda b, g, j: (b, g, 0, 0)),
                  pl.BlockSpec((1, 1, seq, LANES), lambda b, g, j: (b, g, 0, 0)),
                  pl.BlockSpec((1, 1, HEAD_DIM, ncp), lambda b, g, j: (b, g, 0, 0)),
                  pl.BlockSpec((1, 1, ncp, LANES), lambda b, g, j: (b, g, 0, 0)),
                  const2(ovl), const2(eneg)],
        out_specs=pl.BlockSpec((QB, gw), lambda b, g, j: (b * nq + j, g)),
        scratch_shapes=[pltpu.VMEM((H // 2, QB, nsel + LANES), BF16), pltpu.VMEM((H, QB, LANES), F32),
                        pltpu.VMEM((H, QB, LANES), F32), pltpu.VMEM((H, QB, LANES), F32)],
        compiler_params=_cparams(("arbitrary", "arbitrary", "arbitrary")),
        name="sparse_attention",
    )(q, gl, ks, vs, kw, vw, kc, vc, ovl, eneg)


def _merge_kernel(x_ref, att_ref, ya_ref, mg_ref, g1_ref, wb_ref, wo_ref, o_ref):
    d = x_ref.shape[1]
    y_b = _bdot(att_ref[...], wb_ref[...])
    mg = mg_ref[...]
    merged = _sigmoid(mg[:, 0:d]) * ya_ref[...] + _sigmoid(mg[:, d:2 * d]) * y_b
    o_ref[...] = x_ref[...] + g1_ref[0] * _bdot(merged.astype(BF16), wo_ref[...])


def _merge(x2, att, y_a, mg, gate1, w_proj_b, w_out, seq):
    n, d = x2.shape
    tm = ROW_TILE
    tiles_per_b = seq // tm
    return pl.pallas_call(
        _merge_kernel,
        out_shape=jax.ShapeDtypeStruct((n, d), F32),
        grid=(n // tm,),
        in_specs=[pl.BlockSpec((tm, d), lambda i: (i, 0)),
                  pl.BlockSpec((tm, d), lambda i: (i, 0)),
                  pl.BlockSpec((tm, d), lambda i: (i, 0)),
                  pl.BlockSpec((tm, 2 * d), lambda i: (i, 0)),
                  pl.BlockSpec((1, 1, d), lambda i: (i // tiles_per_b, 0, 0)),
                  pl.BlockSpec(w_proj_b.shape, lambda i: (0, 0)),
                  pl.BlockSpec(w_out.shape, lambda i: (0, 0))],
        out_specs=pl.BlockSpec((tm, d), lambda i: (i, 0)),
        compiler_params=_cparams(("arbitrary",)),
        name="merge_out_projection",
    )(x2, att, y_a, mg, gate1[:, None, :], w_proj_b, w_out)


def _router_kernel(x_ref, g_ref, sc_ref, sh_ref, wh_ref, wl_ref, rb_ref, tri_ref,
                   h_ref, e_ref, gt_ref, pos_ref, cnt_ref, carry_ref):
    i = pl.program_id(0)
    tm, d = x_ref.shape

    @pl.when(i == 0)
    def _():
        carry_ref[...] = jnp.zeros((N_EXPERTS, 1), F32)

    x = x_ref[...]
    ms = jnp.mean(x * x, axis=-1, keepdims=True)
    h = x * lax.rsqrt(ms + EPS) * g_ref[...]
    h = h * (1.0 + sc_ref[0]) + sh_ref[0]
    for s in range(d // LANES):
        h_ref[pl.ds(s, tm, stride=SUBLANES), :] = h[:, s * LANES:(s + 1) * LANES]

    h_hi, h_lo = _split(h)
    nt = (((1,), (1,)), ((), ()))
    logits = (lax.dot_general(wh_ref[...], h_hi, nt, preferred_element_type=F32)
              + lax.dot_general(wh_ref[...], h_lo, nt, preferred_element_type=F32)
              + lax.dot_general(wl_ref[...], h_hi, nt, preferred_element_type=F32)
              + rb_ref[...])
    eidx = lax.broadcasted_iota(I32, (N_EXPERTS, tm), 0)
    onehot = jnp.zeros((N_EXPERTS, tm), F32)
    picks, vals, ids = [], [], []
    for _ in range(TOP_K):
        mx = jnp.max(logits, axis=0, keepdims=True)
        first = jnp.min(jnp.where(logits == mx, eidx, N_EXPERTS), axis=0, keepdims=True)
        pick = eidx == first
        picks.append(pick)
        vals.append(mx)
        ids.append(first)
        onehot = jnp.where(pick, 1.0, onehot)
        logits = jnp.where(pick, -3e38, logits)
    ex = [jnp.exp(vk - vals[0]) for vk in vals]
    den = ex[0] + ex[1] + ex[2] + ex[3]
    before = _bdot(onehot.astype(BF16), tri_ref[...]) + carry_ref[...]
    pad = SUBLANES - TOP_K
    pos = [jnp.sum(jnp.where(pk, before, 0.0), axis=0, keepdims=True) for pk in picks]
    e_ref[...] = jnp.concatenate(ids + [jnp.zeros((pad, tm), I32)], axis=0)
    gt_ref[...] = jnp.concatenate([e / den for e in ex] + [jnp.zeros((pad, tm), F32)], axis=0)
    pos_ref[...] = jnp.concatenate(pos + [jnp.zeros((pad, tm), F32)], axis=0).astype(I32)
    carry_ref[...] = carry_ref[...] + jnp.sum(onehot, axis=1, keepdims=True)
    cnt_ref[...] = carry_ref[...]


def _router(x2, gain, scale, shift, wr_hi, wr_lo, rb, tri, seq):
    n, d = x2.shape
    tm = ROUTE_TILE
    tiles_per_b = seq // tm
    nsub = d // LANES
    return pl.pallas_call(
        _router_kernel,
        out_shape=[jax.ShapeDtypeStruct((n * nsub, LANES), F32),
                   jax.ShapeDtypeStruct((SUBLANES, n), I32),
                   jax.ShapeDtypeStruct((SUBLANES, n), F32),
                   jax.ShapeDtypeStruct((SUBLANES, n), I32),
                   jax.ShapeDtypeStruct((N_EXPERTS, 1), F32)],
        grid=(n // tm,),
        in_specs=[pl.BlockSpec((tm, d), lambda i: (i, 0)),
                  pl.BlockSpec((1, d), lambda i: (0, 0)),
                  pl.BlockSpec((1, 1, d), lambda i: (i // tiles_per_b, 0, 0)),
                  pl.BlockSpec((1, 1, d), lambda i: (i // tiles_per_b, 0, 0)),
                  pl.BlockSpec(wr_hi.shape, lambda i: (0, 0)),
                  pl.BlockSpec(wr_lo.shape, lambda i: (0, 0)),
                  pl.BlockSpec((N_EXPERTS, 1), lambda i: (0, 0)),
                  pl.BlockSpec((tm, tm), lambda i: (0, 0))],
        out_specs=[pl.BlockSpec((tm * nsub, LANES), lambda i: (i, 0)),
                   pl.BlockSpec((SUBLANES, tm), lambda i: (0, i)),
                   pl.BlockSpec((SUBLANES, tm), lambda i: (0, i)),
                   pl.BlockSpec((SUBLANES, tm), lambda i: (0, i)),
                   pl.BlockSpec((N_EXPERTS, 1), lambda i: (0, 0))],
        scratch_shapes=[pltpu.VMEM((N_EXPERTS, 1), F32)],
        compiler_params=_cparams(("arbitrary",)),
        name="router",
    )(x2, gain.reshape(1, d), scale[:, None, :], shift[:, None, :], wr_hi, wr_lo, rb.reshape(N_EXPERTS, 1), tri)


def _dispatch_kernel(dest_ref, fill_ref, h_ref, xs_ref, zero_ref, sem, *, nsub):
    i = pl.program_id(0)
    tt = DISPATCH_TILE

    def row_copy(n, k):
        dst = dest_ref[0, 0, k * tt + n]
        return pltpu.make_async_copy(h_ref.at[pl.ds(n * nsub, nsub), :],
                                     xs_ref.at[pl.ds(dst * nsub, nsub), :], sem)

    def fill_copy(r):
        return pltpu.make_async_copy(zero_ref, xs_ref.at[pl.ds(r * nsub, nsub), :], sem)

    def for_fill_rows(fn):
        def per_range(e, carry):
            lax.fori_loop(fill_ref[2 * e], fill_ref[2 * e + 1], lambda r, c: (fn(fill_copy(r)), c)[1], 0)
            return carry

        lax.fori_loop(0, fill_ref.shape[0] // 2, per_range, 0)

    def for_token_rows(fn):
        def per_token(n, carry):
            for k in range(TOP_K):
                fn(row_copy(n, k))
            return carry

        lax.fori_loop(0, tt, per_token, 0)

    @pl.when(i == 0)
    def _():
        zero_ref[...] = jnp.zeros(zero_ref.shape, F32)
        for_fill_rows(lambda cp: cp.start())

    for_token_rows(lambda cp: cp.start())
    for_token_rows(lambda cp: cp.wait())

    @pl.when(i == 0)
    def _():
        for_fill_rows(lambda cp: cp.wait())


def _tile_indices(dest, tt):
    k, n = dest.shape
    return dest.reshape(k, n // tt, tt).transpose(1, 0, 2).reshape(n // tt, 1, k * tt)


def _dispatch(h_rows, dest, fill, n_rows, nsub):
    n_tok = h_rows.shape[0] // nsub
    tt = DISPATCH_TILE
    dest_tiles = _tile_indices(dest, tt)
    grid_spec = pltpu.PrefetchScalarGridSpec(
        num_scalar_prefetch=0,
        grid=(n_tok // tt,),
        in_specs=[pl.BlockSpec((1, 1, TOP_K * tt), lambda i: (i, 0, 0), memory_space=pltpu.SMEM),
                  pl.BlockSpec(memory_space=pltpu.SMEM),
                  pl.BlockSpec((tt * nsub, LANES), lambda i: (i, 0))],
        out_specs=pl.BlockSpec(memory_space=pl.ANY),
        scratch_shapes=[pltpu.VMEM((nsub, LANES), F32), pltpu.SemaphoreType.DMA(())],
    )
    return pl.pallas_call(
        functools.partial(_dispatch_kernel, nsub=nsub),
        out_shape=jax.ShapeDtypeStruct((n_rows * nsub, LANES), F32),
        grid_spec=grid_spec,
        compiler_params=pltpu.CompilerParams(dimension_semantics=("arbitrary",), vmem_limit_bytes=VMEM_LIMIT,
                                             has_side_effects=True),
        name="dispatch",
    )(dest_tiles, fill, h_rows)


PAIR = 2 * LANES


def _pair_permutation():
    p = np.zeros((PAIR, PAIR), np.float32)
    p[2 * np.arange(LANES), np.arange(LANES)] = 1.0
    p[2 * np.arange(LANES) + 1, LANES + np.arange(LANES)] = 1.0
    return p


def _regroup_kernel(w_ref, p_ref, o_ref):
    w = w_ref[0].astype(BF16)
    for j in range(w.shape[1] // PAIR):
        o_ref[0, :, j * PAIR:(j + 1) * PAIR] = _bdot(w[:, j * PAIR:(j + 1) * PAIR], p_ref[...]).astype(BF16)


def _regroup_gate_up(w_gate_up):
    depth, n_e, d, f2 = w_gate_up.shape
    tm = 512
    perm = jnp.asarray(_pair_permutation(), BF16)
    out = pl.pallas_call(
        _regroup_kernel,
        out_shape=jax.ShapeDtypeStruct((depth * n_e, d, f2), BF16),
        grid=(depth * n_e, d // tm),
        in_specs=[pl.BlockSpec((1, tm, f2), lambda e, i: (e, i, 0)),
                  pl.BlockSpec((PAIR, PAIR), lambda e, i: (0, 0))],
        out_specs=pl.BlockSpec((1, tm, f2), lambda e, i: (e, i, 0)),
        compiler_params=_cparams(("arbitrary", "arbitrary")),
        name="regroup_gate_up",
    )(w_gate_up.reshape(depth * n_e, d, f2), perm)
    return out.reshape(depth, n_e, d, f2)


def _regroup_bias(b):
    lead = b.shape[:-1]
    return b.reshape(*lead, -1, LANES, 2).swapaxes(-1, -2).reshape(*lead, -1)


def _expert_kernel(be_ref, nu_ref, xs_ref, wgu_ref, bgu_ref, wd_ref, bd_ref, ys_ref, *, nsub):
    i = pl.program_id(0)
    rows = EXPERT_ROWS
    f = wd_ref.shape[1]

    @pl.when(i < nu_ref[0])
    def _():
        x = jnp.concatenate([xs_ref[pl.ds(s, rows, stride=nsub), :] for s in range(nsub)], axis=-1)
        gu = _bdot(x.astype(BF16), wgu_ref[0]) + bgu_ref[0]
        acts = []
        for j in range(f // LANES):
            g_lin = jnp.minimum(gu[:, j * PAIR:j * PAIR + LANES], SWIGLU_LIMIT)
            u_lin = jnp.clip(gu[:, j * PAIR + LANES:(j + 1) * PAIR], -SWIGLU_LIMIT, SWIGLU_LIMIT)
            acts.append(((u_lin + 1.0) * g_lin * _sigmoid(g_lin * SWIGLU_ALPHA)).astype(BF16))
        act = jnp.concatenate(acts, axis=-1)
        y = _bdot(act, wd_ref[0]) + bd_ref[0]
        for s in range(nsub):
            ys_ref[pl.ds(s, rows, stride=nsub), :] = y[:, s * LANES:(s + 1) * LANES]

    @pl.when(i >= nu_ref[0])
    def _():
        ys_ref[...] = jnp.zeros(ys_ref.shape, F32)


def _experts(blk_e, n_used, xs, w_gu, b_gu, w_d, b_d, nsub):
    rows = EXPERT_ROWS
    n_blocks = xs.shape[0] // (rows * nsub)
    d, f2 = w_gu.shape[1], w_gu.shape[2]
    f = w_d.shape[1]
    row_map = lambda i, be, nu: (jnp.minimum(i, nu[0] - 1), 0)
    grid_spec = pltpu.PrefetchScalarGridSpec(
        num_scalar_prefetch=2,
        grid=(n_blocks,),
        in_specs=[pl.BlockSpec((rows * nsub, LANES), row_map),
                  pl.BlockSpec((1, d, f2), lambda i, be, nu: (be[i], 0, 0)),
                  pl.BlockSpec((1, 1, f2), lambda i, be, nu: (be[i], 0, 0)),
                  pl.BlockSpec((1, f, d), lambda i, be, nu: (be[i], 0, 0)),
                  pl.BlockSpec((1, 1, d), lambda i, be, nu: (be[i], 0, 0))],
        out_specs=pl.BlockSpec((rows * nsub, LANES), lambda i, be, nu: (i, 0)),
    )
    return pl.pallas_call(
        functools.partial(_expert_kernel, nsub=nsub),
        out_shape=jax.ShapeDtypeStruct(xs.shape, F32),
        grid_spec=grid_spec,
        compiler_params=_cparams(("arbitrary",)),
        name="experts",
    )(blk_e, n_used, xs, w_gu, b_gu, w_d, b_d)


def _combine_kernel(cur_ref, nxt_ref, ys_ref, x_ref, gt_ref, g2_ref, o_ref, buf_ref, sems, *, nsub, n_steps):
    i = pl.program_id(0)
    tt = COMBINE_TILE
    slot = i % 2

    def for_tile_rows(idx_ref, s, fn):
        def per_token(n, carry):
            for k in range(TOP_K):
                src = idx_ref[0, 0, k * tt + n]
                fn(pltpu.make_async_copy(ys_ref.at[pl.ds(src * nsub, nsub), :],
                                         buf_ref.at[s, k, pl.ds(n * nsub, nsub), :], sems.at[s]))
            return carry

        lax.fori_loop(0, tt, per_token, 0)

    @pl.when(i == 0)
    def _():
        for_tile_rows(cur_ref, 0, lambda cp: cp.start())

    @pl.when(i + 1 < n_steps)
    def _():
        for_tile_rows(nxt_ref, 1 - slot, lambda cp: cp.start())

    for_tile_rows(cur_ref, slot, lambda cp: cp.wait())
    gt = gt_ref[...]
    acc = None
    for k in range(TOP_K):
        yk = jnp.concatenate([buf_ref[slot, k, pl.ds(s, tt, stride=nsub), :] for s in range(nsub)], axis=-1)
        term = gt[:, k:k + 1] * yk
        acc = term if acc is None else acc + term
    o_ref[...] = x_ref[...] + g2_ref[0] * acc


def _combine(dest, ys, x2, gates_t, gate2, seq, nsub):
    n, d = x2.shape
    tt = COMBINE_TILE
    tiles_per_b = seq // tt
    n_steps = n // tt
    dest_tiles = _tile_indices(dest, tt)
    grid_spec = pltpu.PrefetchScalarGridSpec(
        num_scalar_prefetch=0,
        grid=(n_steps,),
        in_specs=[pl.BlockSpec((1, 1, TOP_K * tt), lambda i: (i, 0, 0), memory_space=pltpu.SMEM),
                  pl.BlockSpec((1, 1, TOP_K * tt), lambda i: (jnp.minimum(i + 1, n_steps - 1), 0, 0),
                               memory_space=pltpu.SMEM),
                  pl.BlockSpec(memory_space=pl.ANY),
                  pl.BlockSpec((tt, d), lambda i: (i, 0)),
                  pl.BlockSpec((tt, SUBLANES), lambda i: (i, 0)),
                  pl.BlockSpec((1, 1, d), lambda i: (i // tiles_per_b, 0, 0))],
        out_specs=pl.BlockSpec((tt, d), lambda i: (i, 0)),
        scratch_shapes=[pltpu.VMEM((2, TOP_K, tt * nsub, LANES), F32), pltpu.SemaphoreType.DMA((2,))],
    )
    return pl.pallas_call(
        functools.partial(_combine_kernel, nsub=nsub, n_steps=n_steps),
        out_shape=jax.ShapeDtypeStruct((n, d), F32),
        grid_spec=grid_spec,
        compiler_params=_cparams(("arbitrary",)),
        name="combine",
    )(dest_tiles, dest_tiles, ys, x2, gates_t, gate2[:, None, :])


def _attention_constants(seq):
    nsel = seq // SEL_LEN
    ncp = seq // CMP_STRIDE
    cmp_start = np.arange(ncp) * CMP_STRIDE
    sel_start = np.arange(nsel) * SEL_LEN
    ovl = ((cmp_start[:, None] < sel_start[None, :] + SEL_LEN)
           & (cmp_start[:, None] + CMP_LEN - 1 >= sel_start[None, :])).astype(np.float32)
    ovl[ncp - 1, :] = 0.0
    eneg = np.where(np.arange(seq)[None, :] // SEL_LEN == np.arange(nsel)[:, None], NEG, 0.0).astype(np.float32)
    return jnp.asarray(ovl.T, BF16), jnp.asarray(eneg, BF16)


def _layer(x2, mod, consts, p, batch, seq):
    n, d = x2.shape
    shift1, scale1, gate1, shift2, scale2, gate2 = [mod[:, k * d:(k + 1) * d] for k in range(6)]
    (cos2, sin2, cos_c, sin_c, ovl, eneg, tri) = consts

    y_a, q, kcm, vcm, ks, vs, kw, vw, gl, mg = _in_projection(
        x2, p["norm1"], scale1, shift1, p["q_gain2"], p["k_gain2"], cos2, sin2, p["w_in_parts"],
        p["w_pool"], p["pool_scale"], p["w_proj_a"], batch, seq)
    kc, vc = _compress(kcm, vcm, p["cmp_pos2"], p["cmp_wa"], p["cmp_wb"], p["cmp_b1"], p["cmp_w2"],
                       p["kc_gain"], cos_c, sin_c, batch, seq)
    att = _attention(q, gl, ks, vs, kw, vw, kc, vc, ovl, eneg, batch, seq)
    x2 = _merge(x2, att, y_a, mg, gate1, p["w_proj_b"], p["w_out"], seq)

    nsub = d // LANES
    h_rows, top_e, gates, pos, counts = _router(x2, p["norm2"], scale2, shift2, p["wr_hi"], p["wr_lo"],
                                                p["router_b"], tri, seq)
    rows = EXPERT_ROWS
    n_blocks = -(-n * TOP_K // rows) + N_EXPERTS
    cnt = counts[:, 0].astype(I32)
    padded = (cnt + rows - 1) // rows * rows
    pad_end = jnp.cumsum(padded)
    pad_start = pad_end - padded
    e_ids = jnp.arange(N_EXPERTS, dtype=I32)
    dest = jnp.sum(jnp.where(top_e[:TOP_K, :, None] == e_ids, pad_start, 0), axis=-1) + pos[:TOP_K]
    fill_lo = jnp.concatenate([pad_start + cnt, pad_end[-1:]])
    fill_hi = jnp.concatenate([pad_end, jnp.full((1,), n_blocks * rows, I32)])
    fill = jnp.stack([fill_lo, fill_hi], axis=1).reshape(-1).astype(I32)
    blk_first = jnp.arange(n_blocks, dtype=I32)[:, None] * rows
    blk_e = jnp.minimum(jnp.sum((pad_end[None, :] <= blk_first).astype(I32), axis=1), N_EXPERTS - 1)
    n_used = (pad_end[-1:] // rows).astype(I32)
    xs = _dispatch(h_rows, dest, fill, n_blocks * rows, nsub)
    ys = _experts(blk_e, n_used, xs, p["w_gu"], p["b_gu"], p["w_d"], p["b_d"], nsub)
    gates_t = gates.T
    return _combine(dest, ys, x2, gates_t, gate2, seq, nsub)


def _prep_layer(l, norm1, norm2, w_in, w_pool, pool_scale, q_norm, k_norm, cmp_pos, cmp_w1, cmp_b1, cmp_w2,
                w_proj_a, w_proj_b, w_out, router_w, router_b, w_gate_up, b_gate_up, w_down, b_down):
    d = w_in.shape[1]
    w = w_in[l]
    o_q = POOL_WIDTH
    o_kv = o_q + N_KV * HEADS_PER_KV * HEAD_DIM
    o_g = o_kv + 6 * LANES
    n_gate = 3 * N_KV * HEADS_PER_KV
    o_m = o_g + n_gate
    wg = w[:, o_g:o_m].reshape(d, 3, N_KV, HEADS_PER_KV).transpose(0, 2, 1, 3).reshape(d, N_KV, 3 * HEADS_PER_KV)
    wg = jnp.pad(wg, ((0, 0), (0, 0), (0, LANES - 3 * HEADS_PER_KV))).reshape(d, N_KV * LANES)
    parts = [w[:, 0:o_q], w[:, o_q:o_kv], w[:, o_kv:o_kv + LANES], w[:, o_kv + LANES:o_kv + 2 * LANES],
             w[:, o_kv + 2 * LANES:o_g], wg, w[:, o_m:]]
    w_in_parts = [a.astype(BF16) for a in parts]

    half = CMP_LEN // 2
    w1 = cmp_w1[l].reshape(2, CMP_LEN, HEAD_DIM, CMP_HIDDEN)

    def chunk_weight(wh):
        z = jnp.zeros_like(wh)
        g0 = jnp.stack([wh, z], axis=2)
        g1 = jnp.stack([z, wh], axis=2)
        return jnp.concatenate([g0, g1], axis=-1).reshape(2, half * LANES, N_KV * CMP_HIDDEN)

    cmp_wa = chunk_weight(w1[:, :half]).astype(BF16)
    cmp_wb = chunk_weight(w1[:, half:]).astype(BF16)
    pos = cmp_pos[l]
    pos_t = jnp.broadcast_to(pos[:, :, None, :], (2, CMP_LEN, N_KV, HEAD_DIM))
    cmp_pos2 = jnp.stack([pos_t[:, :half].reshape(2, half * LANES), pos_t[:, half:].reshape(2, half * LANES)], axis=1)
    b1 = jnp.tile(cmp_b1[l], (1, N_KV))[:, None, :]
    w2 = cmp_w2[l]
    z2 = jnp.zeros_like(w2)
    cmp_w2b = jnp.concatenate([jnp.concatenate([w2, z2], axis=-1), jnp.concatenate([z2, w2], axis=-1)],
                              axis=1).astype(BF16)
    wr = router_w[l].T
    wr_hi = wr.astype(BF16)
    wr_lo = (wr - wr_hi.astype(F32)).astype(BF16)
    w_gu = w_gate_up[l]
    b_gu = _regroup_bias(b_gate_up[l])[:, None, :]
    return dict(
        norm1=norm1[l], norm2=norm2[l], w_in_parts=w_in_parts,
        w_pool=w_pool[l].astype(BF16), pool_scale=pool_scale[l], w_proj_a=w_proj_a[l].astype(BF16),
        k_gain2=jnp.stack([jnp.tile(k_norm[l, 1], N_KV), jnp.tile(k_norm[l, 2], N_KV)], axis=0),
        kc_gain=jnp.tile(k_norm[l, 0], N_KV)[None, :], q_gain2=jnp.tile(q_norm[l], 2)[None, :],
        cmp_pos2=cmp_pos2, cmp_wa=cmp_wa, cmp_wb=cmp_wb, cmp_b1=b1, cmp_w2=cmp_w2b,
        w_proj_b=w_proj_b[l].astype(BF16), w_out=w_out[l].astype(BF16),
        wr_hi=wr_hi, wr_lo=wr_lo, router_b=router_b[l],
        w_gu=w_gu, b_gu=b_gu, w_d=w_down[l].astype(BF16), b_d=b_down[l][:, None, :],
    )


def kernel(x, c, norm1, norm2, ada_w, ada_b, w_in, w_pool, pool_scale, q_norm, k_norm, cmp_pos, cmp_w1, cmp_b1,
           cmp_w2, w_proj_a, w_proj_b, w_out, router_w, router_b, w_gate_up, b_gate_up, w_down, b_down):
    batch, seq, d = x.shape
    depth = norm1.shape[0]
    assert seq % KV_TILE == 0 and seq % ROUTE_TILE == 0 and seq >= WINDOW + Q_TILE and d % LANES == 0
    assert (batch * seq * TOP_K) % EXPERT_ROWS == 0 and (batch * seq) % DISPATCH_TILE == 0
    assert seq % COMBINE_TILE == 0 and seq % Q_TILE == 0 and seq % ROW_TILE == 0

    pos = np.arange(seq)
    cq, sq = _rope_tables(pos)
    cc, sc = _rope_tables(np.arange(seq // CMP_STRIDE) * CMP_STRIDE + CMP_LEN - 1)
    tile2 = lambda a: jnp.asarray(np.concatenate([a, a], axis=1))
    tri = jnp.asarray(np.triu(np.ones((ROUTE_TILE, ROUTE_TILE), np.float32), 1), BF16)
    consts = (tile2(cq), tile2(sq), tile2(cc), tile2(sc), *_attention_constants(seq), tri)

    c_pad = jnp.pad(c, ((0, SUBLANES - batch % SUBLANES if batch % SUBLANES else 0), (0, 0)))
    x2 = x.reshape(batch * seq, d)
    w_gate_up = _regroup_gate_up(w_gate_up)
    for l in range(depth):
        mod = _modulation(c_pad, ada_w[l], ada_b[l])[:batch]
        p = _prep_layer(l, norm1, norm2, w_in, w_pool, pool_scale, q_norm, k_norm, cmp_pos, cmp_w1, cmp_b1, cmp_w2,
                        w_proj_a, w_proj_b, w_out, router_w, router_b, w_gate_up, b_gate_up, w_down, b_down)
        x2 = _layer(x2, mod, consts, p, batch, seq)
    return x2.reshape(batch, seq, d)
```

```python
import functools

import numpy as np
import jax
import jax.numpy as jnp
from jax import lax
from jax.experimental import pallas as pl
from jax.experimental.pallas import tpu as pltpu

F32 = jnp.float32
BF16 = jnp.bfloat16
I32 = jnp.int32

POOL_WINDOWS = (2, 4, 8, 16)
POOL_GROUP = 128
POOL_WIDTH = 512
HEAD_DIM = 64
N_KV = 2
HEADS_PER_KV = 8
ROPE_DIM = 16
ROPE_HALF = 8
ROPE_THETA = 500000.0
CMP_LEN = 32
CMP_STRIDE = 16
CMP_HIDDEN = 256
SEL_LEN = 64
SEL_TOP = 16
WINDOW = 512
FORCED_SCORE = 1e9
N_EXPERTS = 32
TOP_K = 4
SWIGLU_LIMIT = 7.0
SWIGLU_ALPHA = 1.702
EPS = 1e-6
NEG = -1e30
QK_SCALE = HEAD_DIM ** -0.5 * 1.4426950408889634

LANES = 128
SUBLANES = 8
VMEM_LIMIT = 56 * 1024 * 1024

ROW_TILE = 512
POOL_HALO = 16
Q_TILE = 256
KV_TILE = 512
ROUTE_TILE = 512
EXPERT_ROWS = 512
DISPATCH_TILE = 1024
COMBINE_TILE = 256


def _cparams(sem):
    return pltpu.CompilerParams(dimension_semantics=sem, vmem_limit_bytes=VMEM_LIMIT)


def _bdot(a, b):
    return jnp.dot(a, b, preferred_element_type=F32)


def _split(a):
    hi = a.astype(BF16)
    lo = (a - hi.astype(F32)).astype(BF16)
    return hi, lo


def _sigmoid(x):
    return 1.0 / (1.0 + jnp.exp(-x))


def _mod_kernel(c_ref, w_ref, b_ref, o_ref):
    c = c_ref[...]
    a = c * _sigmoid(c)
    a_hi, a_lo = _split(a)
    w_hi, w_lo = _split(w_ref[...])
    o_ref[...] = _bdot(a_hi, w_hi) + _bdot(a_lo, w_hi) + _bdot(a_hi, w_lo) + b_ref[...]


def _modulation(c_pad, ada_w, ada_b):
    rows, d = c_pad.shape
    n = ada_w.shape[1]
    tn = 512
    return pl.pallas_call(
        _mod_kernel,
        out_shape=jax.ShapeDtypeStruct((rows, n), F32),
        grid=(n // tn,),
        in_specs=[pl.BlockSpec((rows, d), lambda i: (0, 0)),
                  pl.BlockSpec((d, tn), lambda i: (0, i)),
                  pl.BlockSpec((1, tn), lambda i: (0, i))],
        out_specs=pl.BlockSpec((rows, tn), lambda i: (0, i)),
        compiler_params=_cparams(("arbitrary",)),
        name="modulation",
    )(c_pad, ada_w, ada_b.reshape(1, n))


def _pool_mixer(u, t_in_seq, wp_ref, ps_ref, wa_ref, ext_ref):
    ts = u.shape[0]

    @pl.when(t_in_seq == 0)
    def _():
        ext_ref[0:POOL_HALO, :] = jnp.zeros((POOL_HALO, POOL_WIDTH), F32)

    @pl.when(t_in_seq > 0)
    def _():
        ext_ref[0:POOL_HALO, :] = ext_ref[ts:ts + POOL_HALO, :]

    ext_ref[POOL_HALO:POOL_HALO + ts, :] = u
    t = t_in_seq * ts + lax.broadcasted_iota(I32, (ts, 1), 0)
    outs = []
    for gi, win in enumerate(POOL_WINDOWS):
        lo, hi = gi * POOL_GROUP, (gi + 1) * POOL_GROUP
        ug = u[:, lo:hi]
        total = ug
        for jj in range(1, win):
            total = total + ext_ref[POOL_HALO - jj:POOL_HALO - jj + ts, lo:hi]
        cnt = jnp.minimum(t + 1, win).astype(F32)
        dlt = total / cnt - ug
        outs.append(_bdot(dlt.astype(BF16), wp_ref[gi]))
    y = jnp.concatenate(outs, axis=-1) * ps_ref[...]
    return _bdot(y.astype(BF16), wa_ref[...])


def _inproj_kernel(x_ref, g_ref, sc_ref, sh_ref, qg_ref, kg_ref, cos_ref, sin_ref,
                   wu_ref, wq_ref, wkc_ref, wvc_ref, wkv_ref, wgl_ref, wmg_ref, wp_ref, ps_ref, wa_ref,
                   ya_ref, q_ref, kc_ref, vc_ref, ks_ref, vs_ref, kw_ref, vw_ref, gl_ref, mg_ref,
                   ext_ref, *, tiles_per_seq):
    x = x_ref[...]
    ms = jnp.mean(x * x, axis=-1, keepdims=True)
    h = x * lax.rsqrt(ms + EPS) * g_ref[...]
    h = h * (1.0 + sc_ref[0]) + sh_ref[0]
    hb = h.astype(BF16)
    cos, sin = cos_ref[...], sin_ref[...]

    yq = _bdot(hb, wq_ref[...])
    kv = _bdot(hb, wkv_ref[...])

    for j in range(yq.shape[1] // LANES):
        q2 = _norm_rope_pair(yq[:, j * LANES:(j + 1) * LANES], qg_ref[...], cos, sin)
        q_ref[:, j * LANES:(j + 1) * LANES] = (q2 * QK_SCALE).astype(BF16)

    mg_ref[...] = _bdot(hb, wmg_ref[...]).astype(BF16)
    u = _bdot(hb, wu_ref[...])

    ks = _norm_rope_pair(kv[:, 0:LANES], kg_ref[0:1, :], cos, sin).T.astype(BF16)
    kw = _norm_rope_pair(kv[:, 2 * LANES:3 * LANES], kg_ref[1:2, :], cos, sin).T.astype(BF16)
    for gi in range(N_KV):
        ks_ref[0, gi] = ks[gi * HEAD_DIM:(gi + 1) * HEAD_DIM, :]
        kw_ref[0, gi] = kw[gi * HEAD_DIM:(gi + 1) * HEAD_DIM, :]
        vs_ref[0, gi] = _values_with_ones(kv[:, LANES:2 * LANES], gi)
        vw_ref[0, gi] = _values_with_ones(kv[:, 3 * LANES:4 * LANES], gi)

    kc_ref[...] = _bdot(hb, wkc_ref[...])
    vc_ref[...] = _bdot(hb, wvc_ref[...])
    gl_ref[...] = _bdot(hb, wgl_ref[...])
    ya_ref[...] = _pool_mixer(u, pl.program_id(0) % tiles_per_seq, wp_ref, ps_ref, wa_ref, ext_ref).astype(BF16)


def _in_projection(x2, gain, scale, shift, q_gain2, k_gain2, cos2, sin2, weights, w_pool, pool_scale, w_proj_a,
                   batch, seq):
    n, d = x2.shape
    tm = ROW_TILE
    tps = seq // tm
    w_u, w_q, w_kc, w_vc, w_kv, w_gl, w_mg = weights
    const = lambda a: pl.BlockSpec(a.shape, lambda i: (0,) * a.ndim)
    rows = lambda width: pl.BlockSpec((tm, width), lambda i: (i, 0))
    per_b = pl.BlockSpec((1, 1, d), lambda i: (i // tps, 0, 0))
    table = pl.BlockSpec((tm, LANES), lambda i: (i % tps, 0))
    kt_shape = jax.ShapeDtypeStruct((batch, N_KV, HEAD_DIM, seq), BF16)
    v_shape = jax.ShapeDtypeStruct((batch, N_KV, seq, LANES), BF16)
    kt_spec = pl.BlockSpec((1, N_KV, HEAD_DIM, tm), lambda i: (i // tps, 0, 0, i % tps))
    v_spec = pl.BlockSpec((1, N_KV, tm, LANES), lambda i: (i // tps, 0, i % tps, 0))
    f32_out = lambda width: jax.ShapeDtypeStruct((n, width), F32)
    bf16_out = lambda width: jax.ShapeDtypeStruct((n, width), BF16)
    pool_scale2 = pool_scale.reshape(1, POOL_WIDTH)
    return pl.pallas_call(
        functools.partial(_inproj_kernel, tiles_per_seq=tps),
        out_shape=[bf16_out(w_proj_a.shape[1]), bf16_out(w_q.shape[1]),
                   f32_out(w_kc.shape[1]), f32_out(w_vc.shape[1]), kt_shape, v_shape, kt_shape, v_shape,
                   f32_out(w_gl.shape[1]), bf16_out(w_mg.shape[1])],
        grid=(n // tm,),
        in_specs=[rows(d), const(gain.reshape(1, d)), per_b, per_b, const(q_gain2), const(k_gain2), table, table,
                  const(w_u), const(w_q), const(w_kc), const(w_vc), const(w_kv), const(w_gl), const(w_mg),
                  const(w_pool), const(pool_scale2), const(w_proj_a)],
        out_specs=[rows(w_proj_a.shape[1]), rows(w_q.shape[1]), rows(w_kc.shape[1]), rows(w_vc.shape[1]),
                   kt_spec, v_spec, kt_spec, v_spec, rows(w_gl.shape[1]), rows(w_mg.shape[1])],
        scratch_shapes=[pltpu.VMEM((tm + POOL_HALO, POOL_WIDTH), F32)],
        compiler_params=_cparams(("arbitrary",)),
        name="in_projection",
    )(x2, gain.reshape(1, d), scale[:, None, :], shift[:, None, :], q_gain2, k_gain2, cos2, sin2,
      w_u, w_q, w_kc, w_vc, w_kv, w_gl, w_mg, w_pool, pool_scale2, w_proj_a)


def _rope_tables(pos):
    inv_freq = np.float32(ROPE_THETA) ** (-(np.arange(ROPE_HALF, dtype=np.float32)) / np.float32(ROPE_HALF))
    ang = pos.astype(np.float32)[:, None] * inv_freq.astype(np.float32)[None, :]
    cos, sin = np.cos(ang), np.sin(ang)
    c = np.ones((pos.shape[0], HEAD_DIM), np.float32)
    s = np.zeros((pos.shape[0], HEAD_DIM), np.float32)
    c[:, :ROPE_HALF] = cos
    c[:, ROPE_HALF:ROPE_DIM] = cos
    s[:, :ROPE_HALF] = -sin
    s[:, ROPE_HALF:ROPE_DIM] = sin
    return c, s


def _norm_rope_pair(x, gain, cos, sin):
    rows = x.shape[0]
    lane = lax.broadcasted_iota(I32, (rows, LANES), 1)
    first = lane < HEAD_DIM
    x2 = x * x
    s0 = jnp.sum(jnp.where(first, x2, 0.0), axis=-1, keepdims=True)
    s1 = jnp.sum(jnp.where(first, 0.0, x2), axis=-1, keepdims=True)
    r = lax.rsqrt(jnp.where(first, s0, s1) * (1.0 / HEAD_DIM) + EPS)
    y = x * r * gain
    low = (lane & (HEAD_DIM - 1)) < ROPE_HALF
    sw = jnp.where(low, pltpu.roll(y, LANES - ROPE_HALF, 1), pltpu.roll(y, ROPE_HALF, 1))
    return y * cos + sw * sin


def _values_with_ones(v, gi):
    lane = lax.broadcasted_iota(I32, v.shape, 1)
    vg = v if gi == 0 else pltpu.roll(v, HEAD_DIM, 1)
    return jnp.where(lane < HEAD_DIM, vg, 1.0).astype(BF16)


def _compress_kernel(k_ref, v_ref, pos_ref, wa_ref, wb_ref, b1_ref, w2_ref, gain_ref, cos_ref, sin_ref,
                     kc_ref, vc_ref):
    nch = k_ref.shape[0]

    def mlp(x, idx):
        xa = (x + pos_ref[idx, 0:1, :]).astype(BF16)
        xb = (x + pos_ref[idx, 1:2, :]).astype(BF16)
        a = _bdot(xa, wa_ref[idx])
        b = _bdot(xb, wb_ref[idx])
        b_next = jnp.concatenate([b[1:nch, :], jnp.zeros((1, b.shape[1]), F32)], axis=0)
        pre = a + b_next + b1_ref[idx]
        hid = pre * _sigmoid(pre)
        return _bdot(hid.astype(BF16), w2_ref[idx])

    kc = _norm_rope_pair(mlp(k_ref[...], 0), gain_ref[...], cos_ref[...], sin_ref[...]).T.astype(BF16)
    vc = mlp(v_ref[...], 1)
    for gi in range(N_KV):
        kc_ref[0, gi] = kc[gi * HEAD_DIM:(gi + 1) * HEAD_DIM, :]
        vc_ref[0, gi] = _values_with_ones(vc, gi)


def _compress(kcm, vcm, pos2, wa, wb, b1, w2, gain, cos_c, sin_c, batch, seq):
    nch = seq // CMP_STRIDE
    width = CMP_STRIDE * LANES
    full = lambda a: pl.BlockSpec(a.shape, lambda b: (0,) * a.ndim)
    return pl.pallas_call(
        _compress_kernel,
        out_shape=[jax.ShapeDtypeStruct((batch, N_KV, HEAD_DIM, nch), BF16),
                   jax.ShapeDtypeStruct((batch, N_KV, nch, LANES), BF16)],
        grid=(batch,),
        in_specs=[pl.BlockSpec((nch, width), lambda b: (b, 0)),
                  pl.BlockSpec((nch, width), lambda b: (b, 0)),
                  full(pos2), full(wa), full(wb), full(b1), full(w2), full(gain), full(cos_c), full(sin_c)],
        out_specs=[pl.BlockSpec((1, N_KV, HEAD_DIM, nch), lambda b: (b, 0, 0, 0)),
                   pl.BlockSpec((1, N_KV, nch, LANES), lambda b: (b, 0, 0, 0))],
        compiler_params=_cparams(("arbitrary",)),
        name="compress",
    )(kcm.reshape(batch * nch, width), vcm.reshape(batch * nch, width), pos2, wa, wb, b1, w2, gain, cos_c, sin_c)


def _attn_kernel(q_ref, gl_ref, ks_ref, vs_ref, kw_ref, vw_ref, kc_ref, vc_ref, ovl_ref, eneg_ref,
                 o_ref, qa_scr, pre_scr, m_scr, acc_scr, *, seq):
    H, QB, TK = HEADS_PER_KV, Q_TILE, KV_TILE
    q0 = pl.program_id(2) * QB
    ncp = seq // CMP_STRIDE
    nsel = seq // SEL_LEN
    n_top = min(SEL_TOP, nsel)
    nt_dims = (((1,), (1,)), ((), ()))

    def row_sum(a):
        return a[:, HEAD_DIM:HEAD_DIM + 1]

    def pair_keys(k_t, lead=None):
        z = jnp.zeros_like(k_t)
        head = [] if lead is None else [lead]
        return (jnp.concatenate(head + [k_t, z], axis=0), jnp.concatenate(head + [z, k_t], axis=0))

    q_pairs = [q_ref[:, j * LANES:(j + 1) * LANES] for j in range(H // 2)]
    qpos = q0 + lax.broadcasted_iota(I32, (QB, 1), 0)

    def one_shot_branch(k_t, v_aug, mask, zero_masked, between=None):
        k_var = pair_keys(k_t)
        score = lambda h: _bdot(q_pairs[h // 2], k_var[h % 2])
        probs, accs = [], []
        ahead = 1
        scores = [score(h) for h in range(ahead)]
        for h in range(H):
            s = jnp.where(mask, scores[h], NEG)
            scores[h] = None
            if h + ahead < H:
                scores.append(score(h + ahead))
            p = jnp.exp2(s - jnp.max(s, axis=-1, keepdims=True))
            if zero_masked:
                p = jnp.where(mask, p, 0.0)
            probs.append(p)
            accs.append(_bdot(p.astype(BF16), v_aug))
            if between is not None:
                between(h, probs, accs)
        return accs

    gs = _sigmoid(gl_ref[...])

    def compressed_branch(width):
        valid = (lax.broadcasted_iota(I32, (QB, width), 1) * CMP_STRIDE + (CMP_LEN - 1)) <= qpos
        imp_state = {"ps": None}

        def finish_cmp_head(h, probs, accs):
            l = row_sum(accs[h])
            inv = jnp.where(l > 0.0, 1.0 / l, 0.0)
            pre_scr[h] = accs[h] * (gs[:, h:h + 1] * inv)
            term = probs[h] * inv
            imp_state["ps"] = term if imp_state["ps"] is None else imp_state["ps"] + term
            probs[h] = None

        def after_cmp_head(h, probs, accs):
            if h > 0:
                finish_cmp_head(h - 1, probs, accs)
            if h == H - 1:
                finish_cmp_head(h, probs, accs)

        one_shot_branch(kc_ref[0, 0, :, 0:width], vc_ref[0, 0, 0:width, :], valid, True, between=after_cmp_head)
        ps_hi, ps_lo = _split(imp_state["ps"])
        ovl_t = ovl_ref[:, 0:width]
        return (lax.dot_general(ovl_t, ps_hi, nt_dims, preferred_element_type=F32)
                + lax.dot_general(ovl_t, ps_lo, nt_dims, preferred_element_type=F32))

    imp = compressed_branch(ncp)

    sidx = lax.broadcasted_iota(I32, (nsel, QB), 0)
    cur = (q0 + lax.broadcasted_iota(I32, (nsel, QB), 1)) // SEL_LEN
    causal = sidx <= cur
    forced = (sidx == cur) | (sidx == 0)
    pick_state = {"v": jnp.where(forced, FORCED_SCORE, jnp.where(causal, imp, -1.0)),
                  "sel": jnp.zeros((nsel, QB), jnp.bool_)}

    def pick_next():
        v = pick_state["v"]
        mx = jnp.max(v, axis=0, keepdims=True)
        first = jnp.min(jnp.where(v == mx, sidx, nsel), axis=0, keepdims=True)
        pick = sidx == first
        pick_state["sel"] = pick_state["sel"] | pick
        pick_state["v"] = jnp.where(pick, -3e38, v)

    def picks_after_head(h, probs, accs):
        probs[h] = None
        for _ in range(n_top // H + (1 if h < n_top % H else 0)):
            pick_next()

    span = WINDOW + QB
    base = pl.multiple_of(jnp.maximum(q0 - WINDOW, 0), LANES)
    dist = qpos - (base + lax.broadcasted_iota(I32, (QB, span), 1))
    inwin = (dist >= 0) & (dist < WINDOW)
    accs_w = one_shot_branch(kw_ref[0, 0, :, pl.ds(base, span)], vw_ref[0, 0, pl.ds(base, span), :],
                             inwin, False, between=picks_after_head)
    for h in range(H):
        pre_scr[h] = pre_scr[h] + accs_w[h] * (gs[:, 2 * H + h:2 * H + h + 1] / row_sum(accs_w[h]))

    notsel = jnp.where(pick_state["sel"] & causal, 0.0, 1.0).T.astype(BF16)
    for j in range(H // 2):
        qa_scr[j] = jnp.concatenate([notsel, q_pairs[j]], axis=1)

    m_scr[...] = jnp.full(m_scr.shape, NEG, F32)
    acc_scr[...] = jnp.zeros(acc_scr.shape, F32)

    def sel_tile(kt, causal_tile):
        off = pl.multiple_of(kt * TK, TK)
        k_var = pair_keys(ks_ref[0, 0, :, pl.ds(off, TK)], lead=eneg_ref[:, pl.ds(off, TK)])
        score = lambda h: _bdot(qa_scr[h // 2], k_var[h % 2])
        v_t = vs_ref[0, 0, pl.ds(off, TK), :]
        if causal_tile:
            keep = (off + lax.broadcasted_iota(I32, (QB, TK), 1)) <= qpos
        ahead = 4
        scores = [score(h) for h in range(ahead)]
        for h in range(H):
            s = scores[h]
            scores[h] = None
            if h + ahead < H:
                scores.append(score(h + ahead))
            if causal_tile:
                s = jnp.where(keep, s, NEG)
            m_old = m_scr[h]
            m_new = jnp.maximum(m_old, jnp.max(s, axis=-1, keepdims=True))
            alpha = jnp.exp2(m_old - m_new)
            pt = jnp.concatenate([jnp.exp2(s[:, c * LANES:(c + 1) * LANES] - m_new).astype(BF16)
                                  for c in range(TK // LANES)], axis=1)
            acc_scr[h] = alpha * acc_scr[h] + _bdot(pt, v_t)
            m_scr[h] = m_new

    n_tiles = (q0 + QB - 1) // TK + 1

    def body(kt, carry):
        sel_tile(kt, False)
        return carry

    lax.fori_loop(0, n_tiles - 1, body, 0)
    sel_tile(n_tiles - 1, True)

    def head_out(h):
        a_s = acc_scr[h]
        return pre_scr[h] + a_s * (gs[:, H + h:H + h + 1] / row_sum(a_s))

    low = lax.broadcasted_iota(I32, (QB, LANES), 1) < HEAD_DIM
    for j in range(H // 2):
        slab = jnp.where(low, head_out(2 * j), pltpu.roll(head_out(2 * j + 1), HEAD_DIM, 1))
        o_ref[:, j * LANES:(j + 1) * LANES] = slab.astype(BF16)


def _attention(q, gl, ks, vs, kw, vw, kc, vc, ovl, eneg, batch, seq):
    n = q.shape[0]
    QB = Q_TILE
    nq = seq // QB
    H = HEADS_PER_KV
    gw = HEADS_PER_KV * HEAD_DIM
    ncp = seq // CMP_STRIDE
    nsel = seq // SEL_LEN
    const2 = lambda a: pl.BlockSpec(a.shape, lambda b, g, j: (0,) * a.ndim)
    return pl.pallas_call(
        functools.partial(_attn_kernel, seq=seq),
        out_shape=jax.ShapeDtypeStruct((n, N_KV * gw), BF16),
        grid=(batch, N_KV, nq),
        in_specs=[pl.BlockSpec((QB, gw), lambda b, g, j: (b * nq + j, g)),
                  pl.BlockSpec((QB, LANES), lambda b, g, j: (b * nq + j, g)),
                  pl.BlockSpec((1, 1, HEAD_DIM, seq), lambda b, g, j: (b, g, 0, 0)),
                  pl.BlockSpec((1, 1, seq, LANES), lambda b, g, j: (b, g, 0, 0)),
                  pl.BlockSpec((1, 1, HEAD_DIM, seq), lambda b, g, j: (b, g, 0, 0)),
                  pl.BlockSpec((1, 1, seq, LANES), lambda b, g, j: (b, g, 0, 0)),
                  pl.BlockSpec((1, 1, HEAD_DIM, ncp), lambda b, g, j: (b, g, 0, 0)),
                  pl.BlockSpec((1, 1, ncp, LANES), lambda b, g, j: (b, g, 0, 0)),
                  const2(ovl), const2(eneg)],
        out_specs=pl.BlockSpec((QB, gw), lambda b, g, j: (b * nq + j, g)),
        scratch_shapes=[pltpu.VMEM((H // 2, QB, nsel + LANES), BF16), pltpu.VMEM((H, QB, LANES), F32),
                        pltpu.VMEM((H, QB, LANES), F32), pltpu.VMEM((H, QB, LANES), F32)],
        compiler_params=_cparams(("arbitrary", "arbitrary", "arbitrary")),
        name="sparse_attention",
    )(q, gl, ks, vs, kw, vw, kc, vc, ovl, eneg)


def _merge_kernel(x_ref, att_ref, ya_ref, mg_ref, g1_ref, wb_ref, wo_ref, o_ref):
    d = x_ref.shape[1]
    y_b = _bdot(att_ref[...], wb_ref[...])
    mg = mg_ref[...].astype(F32)
    merged = _sigmoid(mg[:, 0:d]) * ya_ref[...].astype(F32) + _sigmoid(mg[:, d:2 * d]) * y_b
    o_ref[...] = x_ref[...] + g1_ref[0] * _bdot(merged.astype(BF16), wo_ref[...])


def _merge(x2, att, y_a, mg, gate1, w_proj_b, w_out, seq):
    n, d = x2.shape
    tm = ROW_TILE
    tiles_per_b = seq // tm
    return pl.pallas_call(
        _merge_kernel,
        out_shape=jax.ShapeDtypeStruct((n, d), F32),
        grid=(n // tm,),
        in_specs=[pl.BlockSpec((tm, d), lambda i: (i, 0)),
                  pl.BlockSpec((tm, d), lambda i: (i, 0)),
                  pl.BlockSpec((tm, d), lambda i: (i, 0)),
                  pl.BlockSpec((tm, 2 * d), lambda i: (i, 0)),
                  pl.BlockSpec((1, 1, d), lambda i: (i // tiles_per_b, 0, 0)),
                  pl.BlockSpec(w_proj_b.shape, lambda i: (0, 0)),
                  pl.BlockSpec(w_out.shape, lambda i: (0, 0))],
        out_specs=pl.BlockSpec((tm, d), lambda i: (i, 0)),
        compiler_params=_cparams(("arbitrary",)),
        name="merge_out_projection",
    )(x2, att, y_a, mg, gate1[:, None, :], w_proj_b, w_out)


def _router_kernel(x_ref, g_ref, sc_ref, sh_ref, wh_ref, wl_ref, rb_ref, tri_ref,
                   h_ref, e_ref, gt_ref, pos_ref, cnt_ref, carry_ref):
    i = pl.program_id(0)
    tm, d = x_ref.shape

    @pl.when(i == 0)
    def _():
        carry_ref[...] = jnp.zeros((N_EXPERTS, 1), F32)

    x = x_ref[...]
    ms = jnp.mean(x * x, axis=-1, keepdims=True)
    h = x * lax.rsqrt(ms + EPS) * g_ref[...]
    h = h * (1.0 + sc_ref[0]) + sh_ref[0]
    for s in range(d // LANES):
        h_ref[pl.ds(s, tm, stride=SUBLANES), :] = h[:, s * LANES:(s + 1) * LANES]

    h_hi, h_lo = _split(h)
    nt = (((1,), (1,)), ((), ()))
    logits = (lax.dot_general(wh_ref[...], h_hi, nt, preferred_element_type=F32)
              + lax.dot_general(wh_ref[...], h_lo, nt, preferred_element_type=F32)
              + lax.dot_general(wl_ref[...], h_hi, nt, preferred_element_type=F32)
              + rb_ref[...])
    eidx = lax.broadcasted_iota(I32, (N_EXPERTS, tm), 0)
    onehot = jnp.zeros((N_EXPERTS, tm), F32)
    picks, vals, ids = [], [], []
    for _ in range(TOP_K):
        mx = jnp.max(logits, axis=0, keepdims=True)
        first = jnp.min(jnp.where(logits == mx, eidx, N_EXPERTS), axis=0, keepdims=True)
        pick = eidx == first
        picks.append(pick)
        vals.append(mx)
        ids.append(first)
        onehot = jnp.where(pick, 1.0, onehot)
        logits = jnp.where(pick, -3e38, logits)
    ex = [jnp.exp(vk - vals[0]) for vk in vals]
    den = ex[0] + ex[1] + ex[2] + ex[3]
    before = _bdot(onehot.astype(BF16), tri_ref[...]) + carry_ref[...]
    pad = SUBLANES - TOP_K
    pos = [jnp.sum(jnp.where(pk, before, 0.0), axis=0, keepdims=True) for pk in picks]
    e_ref[...] = jnp.concatenate(ids + [jnp.zeros((pad, tm), I32)], axis=0)
    gt_ref[...] = jnp.concatenate([e / den for e in ex] + [jnp.zeros((pad, tm), F32)], axis=0)
    pos_ref[...] = jnp.concatenate(pos + [jnp.zeros((pad, tm), F32)], axis=0).astype(I32)
    carry_ref[...] = carry_ref[...] + jnp.sum(onehot, axis=1, keepdims=True)
    cnt_ref[...] = carry_ref[...]


def _router(x2, gain, scale, shift, wr_hi, wr_lo, rb, tri, seq):
    n, d = x2.shape
    tm = ROUTE_TILE
    tiles_per_b = seq // tm
    nsub = d // LANES
    return pl.pallas_call(
        _router_kernel,
        out_shape=[jax.ShapeDtypeStruct((n * nsub, LANES), F32),
                   jax.ShapeDtypeStruct((SUBLANES, n), I32),
                   jax.ShapeDtypeStruct((SUBLANES, n), F32),
                   jax.ShapeDtypeStruct((SUBLANES, n), I32),
                   jax.ShapeDtypeStruct((N_EXPERTS, 1), F32)],
        grid=(n // tm,),
        in_specs=[pl.BlockSpec((tm, d), lambda i: (i, 0)),
                  pl.BlockSpec((1, d), lambda i: (0, 0)),
                  pl.BlockSpec((1, 1, d), lambda i: (i // tiles_per_b, 0, 0)),
                  pl.BlockSpec((1, 1, d), lambda i: (i // tiles_per_b, 0, 0)),
                  pl.BlockSpec(wr_hi.shape, lambda i: (0, 0)),
                  pl.BlockSpec(wr_lo.shape, lambda i: (0, 0)),
                  pl.BlockSpec((N_EXPERTS, 1), lambda i: (0, 0)),
                  pl.BlockSpec((tm, tm), lambda i: (0, 0))],
        out_specs=[pl.BlockSpec((tm * nsub, LANES), lambda i: (i, 0)),
                   pl.BlockSpec((SUBLANES, tm), lambda i: (0, i)),
                   pl.BlockSpec((SUBLANES, tm), lambda i: (0, i)),
                   pl.BlockSpec((SUBLANES, tm), lambda i: (0, i)),
                   pl.BlockSpec((N_EXPERTS, 1), lambda i: (0, 0))],
        scratch_shapes=[pltpu.VMEM((N_EXPERTS, 1), F32)],
        compiler_params=_cparams(("arbitrary",)),
        name="router",
    )(x2, gain.reshape(1, d), scale[:, None, :], shift[:, None, :], wr_hi, wr_lo, rb.reshape(N_EXPERTS, 1), tri)


def _dispatch_kernel(dest_ref, fill_ref, h_ref, xs_ref, zero_ref, sem, *, nsub):
    i = pl.program_id(0)
    tt = DISPATCH_TILE

    def row_copy(n, k):
        dst = dest_ref[0, 0, k * tt + n]
        return pltpu.make_async_copy(h_ref.at[pl.ds(n * nsub, nsub), :],
                                     xs_ref.at[pl.ds(dst * nsub, nsub), :], sem)

    def fill_copy(r):
        return pltpu.make_async_copy(zero_ref, xs_ref.at[pl.ds(r * nsub, nsub), :], sem)

    def for_fill_rows(fn):
        def per_range(e, carry):
            lax.fori_loop(fill_ref[2 * e], fill_ref[2 * e + 1], lambda r, c: (fn(fill_copy(r)), c)[1], 0)
            return carry

        lax.fori_loop(0, fill_ref.shape[0] // 2, per_range, 0)

    def for_token_rows(fn):
        def per_token(n, carry):
            for k in range(TOP_K):
                fn(row_copy(n, k))
            return carry

        lax.fori_loop(0, tt, per_token, 0)

    @pl.when(i == 0)
    def _():
        zero_ref[...] = jnp.zeros(zero_ref.shape, F32)
        for_fill_rows(lambda cp: cp.start())

    for_token_rows(lambda cp: cp.start())
    for_token_rows(lambda cp: cp.wait())

    @pl.when(i == 0)
    def _():
        for_fill_rows(lambda cp: cp.wait())


def _tile_indices(dest, tt):
    k, n = dest.shape
    return dest.reshape(k, n // tt, tt).transpose(1, 0, 2).reshape(n // tt, 1, k * tt)


def _dispatch(h_rows, dest, fill, n_rows, nsub):
    n_tok = h_rows.shape[0] // nsub
    tt = DISPATCH_TILE
    dest_tiles = _tile_indices(dest, tt)
    grid_spec = pltpu.PrefetchScalarGridSpec(
        num_scalar_prefetch=0,
        grid=(n_tok // tt,),
        in_specs=[pl.BlockSpec((1, 1, TOP_K * tt), lambda i: (i, 0, 0), memory_space=pltpu.SMEM),
                  pl.BlockSpec(memory_space=pltpu.SMEM),
                  pl.BlockSpec((tt * nsub, LANES), lambda i: (i, 0))],
        out_specs=pl.BlockSpec(memory_space=pl.ANY),
        scratch_shapes=[pltpu.VMEM((nsub, LANES), F32), pltpu.SemaphoreType.DMA(())],
    )
    return pl.pallas_call(
        functools.partial(_dispatch_kernel, nsub=nsub),
        out_shape=jax.ShapeDtypeStruct((n_rows * nsub, LANES), F32),
        grid_spec=grid_spec,
        compiler_params=pltpu.CompilerParams(dimension_semantics=("arbitrary",), vmem_limit_bytes=VMEM_LIMIT,
                                             has_side_effects=True),
        name="dispatch",
    )(dest_tiles, fill, h_rows)


PAIR = 2 * LANES


def _pair_permutation():
    p = np.zeros((PAIR, PAIR), np.float32)
    p[2 * np.arange(LANES), np.arange(LANES)] = 1.0
    p[2 * np.arange(LANES) + 1, LANES + np.arange(LANES)] = 1.0
    return p


def _regroup_kernel(w_ref, p_ref, o_ref):
    w = w_ref[0].astype(BF16)
    for j in range(w.shape[1] // PAIR):
        o_ref[0, :, j * PAIR:(j + 1) * PAIR] = _bdot(w[:, j * PAIR:(j + 1) * PAIR], p_ref[...]).astype(BF16)


def _regroup_gate_up(w_gate_up):
    depth, n_e, d, f2 = w_gate_up.shape
    tm = 512
    perm = jnp.asarray(_pair_permutation(), BF16)
    out = pl.pallas_call(
        _regroup_kernel,
        out_shape=jax.ShapeDtypeStruct((depth * n_e, d, f2), BF16),
        grid=(depth * n_e, d // tm),
        in_specs=[pl.BlockSpec((1, tm, f2), lambda e, i: (e, i, 0)),
                  pl.BlockSpec((PAIR, PAIR), lambda e, i: (0, 0))],
        out_specs=pl.BlockSpec((1, tm, f2), lambda e, i: (e, i, 0)),
        compiler_params=_cparams(("arbitrary", "arbitrary")),
        name="regroup_gate_up",
    )(w_gate_up.reshape(depth * n_e, d, f2), perm)
    return out.reshape(depth, n_e, d, f2)


def _regroup_bias(b):
    lead = b.shape[:-1]
    return b.reshape(*lead, -1, LANES, 2).swapaxes(-1, -2).reshape(*lead, -1)


def _expert_kernel(be_ref, nu_ref, xs_ref, wgu_ref, bgu_ref, wd_ref, bd_ref, ys_ref, *, nsub):
    i = pl.program_id(0)
    rows = EXPERT_ROWS
    f = wd_ref.shape[1]

    @pl.when(i < nu_ref[0])
    def _():
        x = jnp.concatenate([xs_ref[pl.ds(s, rows, stride=nsub), :] for s in range(nsub)], axis=-1)
        gu = _bdot(x.astype(BF16), wgu_ref[0]) + bgu_ref[0]
        acts = []
        for j in range(f // LANES):
            g_lin = jnp.minimum(gu[:, j * PAIR:j * PAIR + LANES], SWIGLU_LIMIT)
            u_lin = jnp.clip(gu[:, j * PAIR + LANES:(j + 1) * PAIR], -SWIGLU_LIMIT, SWIGLU_LIMIT)
            acts.append(((u_lin + 1.0) * g_lin * _sigmoid(g_lin * SWIGLU_ALPHA)).astype(BF16))
        act = jnp.concatenate(acts, axis=-1)
        y = _bdot(act, wd_ref[0]) + bd_ref[0]
        for s in range(nsub):
            ys_ref[pl.ds(s, rows, stride=nsub), :] = y[:, s * LANES:(s + 1) * LANES]

    @pl.when(i >= nu_ref[0])
    def _():
        ys_ref[...] = jnp.zeros(ys_ref.shape, F32)


def _experts(blk_e, n_used, xs, w_gu, b_gu, w_d, b_d, nsub):
    rows = EXPERT_ROWS
    n_blocks = xs.shape[0] // (rows * nsub)
    d, f2 = w_gu.shape[1], w_gu.shape[2]
    f = w_d.shape[1]
    row_map = lambda i, be, nu: (jnp.minimum(i, nu[0] - 1), 0)
    grid_spec = pltpu.PrefetchScalarGridSpec(
        num_scalar_prefetch=2,
        grid=(n_blocks,),
        in_specs=[pl.BlockSpec((rows * nsub, LANES), row_map),
                  pl.BlockSpec((1, d, f2), lambda i, be, nu: (be[i], 0, 0)),
                  pl.BlockSpec((1, 1, f2), lambda i, be, nu: (be[i], 0, 0)),
                  pl.BlockSpec((1, f, d), lambda i, be, nu: (be[i], 0, 0)),
                  pl.BlockSpec((1, 1, d), lambda i, be, nu: (be[i], 0, 0))],
        out_specs=pl.BlockSpec((rows * nsub, LANES), lambda i, be, nu: (i, 0)),
    )
    return pl.pallas_call(
        functools.partial(_expert_kernel, nsub=nsub),
        out_shape=jax.ShapeDtypeStruct(xs.shape, F32),
        grid_spec=grid_spec,
        compiler_params=_cparams(("arbitrary",)),
        name="experts",
    )(blk_e, n_used, xs, w_gu, b_gu, w_d, b_d)


def _combine_kernel(cur_ref, nxt_ref, ys_ref, x_ref, gt_ref, g2_ref, o_ref, buf_ref, sems, *, nsub, n_steps):
    i = pl.program_id(0)
    tt = COMBINE_TILE
    slot = i % 2

    def for_tile_rows(idx_ref, s, fn):
        def per_token(n, carry):
            for k in range(TOP_K):
                src = idx_ref[0, 0, k * tt + n]
                fn(pltpu.make_async_copy(ys_ref.at[pl.ds(src * nsub, nsub), :],
                                         buf_ref.at[s, k, pl.ds(n * nsub, nsub), :], sems.at[s]))
            return carry

        lax.fori_loop(0, tt, per_token, 0)

    @pl.when(i == 0)
    def _():
        for_tile_rows(cur_ref, 0, lambda cp: cp.start())

    @pl.when(i + 1 < n_steps)
    def _():
        for_tile_rows(nxt_ref, 1 - slot, lambda cp: cp.start())

    for_tile_rows(cur_ref, slot, lambda cp: cp.wait())
    gt = gt_ref[...]
    acc = None
    for k in range(TOP_K):
        yk = jnp.concatenate([buf_ref[slot, k, pl.ds(s, tt, stride=nsub), :] for s in range(nsub)], axis=-1)
        term = gt[:, k:k + 1] * yk
        acc = term if acc is None else acc + term
    o_ref[...] = x_ref[...] + g2_ref[0] * acc


def _combine(dest, ys, x2, gates_t, gate2, seq, nsub):
    n, d = x2.shape
    tt = COMBINE_TILE
    tiles_per_b = seq // tt
    n_steps = n // tt
    dest_tiles = _tile_indices(dest, tt)
    grid_spec = pltpu.PrefetchScalarGridSpec(
        num_scalar_prefetch=0,
        grid=(n_steps,),
        in_specs=[pl.BlockSpec((1, 1, TOP_K * tt), lambda i: (i, 0, 0), memory_space=pltpu.SMEM),
                  pl.BlockSpec((1, 1, TOP_K * tt), lambda i: (jnp.minimum(i + 1, n_steps - 1), 0, 0),
                               memory_space=pltpu.SMEM),
                  pl.BlockSpec(memory_space=pl.ANY),
                  pl.BlockSpec((tt, d), lambda i: (i, 0)),
                  pl.BlockSpec((tt, SUBLANES), lambda i: (i, 0)),
                  pl.BlockSpec((1, 1, d), lambda i: (i // tiles_per_b, 0, 0))],
        out_specs=pl.BlockSpec((tt, d), lambda i: (i, 0)),
        scratch_shapes=[pltpu.VMEM((2, TOP_K, tt * nsub, LANES), F32), pltpu.SemaphoreType.DMA((2,))],
    )
    return pl.pallas_call(
        functools.partial(_combine_kernel, nsub=nsub, n_steps=n_steps),
        out_shape=jax.ShapeDtypeStruct((n, d), F32),
        grid_spec=grid_spec,
        compiler_params=_cparams(("arbitrary",)),
        name="combine",
    )(dest_tiles, dest_tiles, ys, x2, gates_t, gate2[:, None, :])


def _attention_constants(seq):
    nsel = seq // SEL_LEN
    ncp = seq // CMP_STRIDE
    cmp_start = np.arange(ncp) * CMP_STRIDE
    sel_start = np.arange(nsel) * SEL_LEN
    ovl = ((cmp_start[:, None] < sel_start[None, :] + SEL_LEN)
           & (cmp_start[:, None] + CMP_LEN - 1 >= sel_start[None, :])).astype(np.float32)
    ovl[ncp - 1, :] = 0.0
    eneg = np.where(np.arange(seq)[None, :] // SEL_LEN == np.arange(nsel)[:, None], NEG, 0.0).astype(np.float32)
    return jnp.asarray(ovl.T, BF16), jnp.asarray(eneg, BF16)


def _layer(x2, mod, consts, p, batch, seq):
    n, d = x2.shape
    shift1, scale1, gate1, shift2, scale2, gate2 = [mod[:, k * d:(k + 1) * d] for k in range(6)]
    (cos2, sin2, cos_c, sin_c, ovl, eneg, tri) = consts

    y_a, q, kcm, vcm, ks, vs, kw, vw, gl, mg = _in_projection(
        x2, p["norm1"], scale1, shift1, p["q_gain2"], p["k_gain2"], cos2, sin2, p["w_in_parts"],
        p["w_pool"], p["pool_scale"], p["w_proj_a"], batch, seq)
    kc, vc = _compress(kcm, vcm, p["cmp_pos2"], p["cmp_wa"], p["cmp_wb"], p["cmp_b1"], p["cmp_w2"],
                       p["kc_gain"], cos_c, sin_c, batch, seq)
    att = _attention(q, gl, ks, vs, kw, vw, kc, vc, ovl, eneg, batch, seq)
    x2 = _merge(x2, att, y_a, mg, gate1, p["w_proj_b"], p["w_out"], seq)

    nsub = d // LANES
    h_rows, top_e, gates, pos, counts = _router(x2, p["norm2"], scale2, shift2, p["wr_hi"], p["wr_lo"],
                                                p["router_b"], tri, seq)
    rows = EXPERT_ROWS
    n_blocks = -(-n * TOP_K // rows) + N_EXPERTS
    cnt = counts[:, 0].astype(I32)
    padded = (cnt + rows - 1) // rows * rows
    pad_end = jnp.cumsum(padded)
    pad_start = pad_end - padded
    e_ids = jnp.arange(N_EXPERTS, dtype=I32)
    dest = jnp.sum(jnp.where(top_e[:TOP_K, :, None] == e_ids, pad_start, 0), axis=-1) + pos[:TOP_K]
    fill_lo = jnp.concatenate([pad_start + cnt, pad_end[-1:]])
    fill_hi = jnp.concatenate([pad_end, jnp.full((1,), n_blocks * rows, I32)])
    fill = jnp.stack([fill_lo, fill_hi], axis=1).reshape(-1).astype(I32)
    blk_first = jnp.arange(n_blocks, dtype=I32)[:, None] * rows
    blk_e = jnp.minimum(jnp.sum((pad_end[None, :] <= blk_first).astype(I32), axis=1), N_EXPERTS - 1)
    n_used = (pad_end[-1:] // rows).astype(I32)
    xs = _dispatch(h_rows, dest, fill, n_blocks * rows, nsub)
    ys = _experts(blk_e, n_used, xs, p["w_gu"], p["b_gu"], p["w_d"], p["b_d"], nsub)
    gates_t = gates.T
    return _combine(dest, ys, x2, gates_t, gate2, seq, nsub)


def _prep_layer(l, norm1, norm2, w_in, w_pool, pool_scale, q_norm, k_norm, cmp_pos, cmp_w1, cmp_b1, cmp_w2,
                w_proj_a, w_proj_b, w_out, router_w, router_b, w_gate_up, b_gate_up, w_down, b_down):
    d = w_in.shape[1]
    w = w_in[l]
    o_q = POOL_WIDTH
    o_kv = o_q + N_KV * HEADS_PER_KV * HEAD_DIM
    o_g = o_kv + 6 * LANES
    n_gate = 3 * N_KV * HEADS_PER_KV
    o_m = o_g + n_gate
    wg = w[:, o_g:o_m].reshape(d, 3, N_KV, HEADS_PER_KV).transpose(0, 2, 1, 3).reshape(d, N_KV, 3 * HEADS_PER_KV)
    wg = jnp.pad(wg, ((0, 0), (0, 0), (0, LANES - 3 * HEADS_PER_KV))).reshape(d, N_KV * LANES)
    parts = [w[:, 0:o_q], w[:, o_q:o_kv], w[:, o_kv:o_kv + LANES], w[:, o_kv + LANES:o_kv + 2 * LANES],
             w[:, o_kv + 2 * LANES:o_g], wg, w[:, o_m:]]
    w_in_parts = [a.astype(BF16) for a in parts]

    half = CMP_LEN // 2
    w1 = cmp_w1[l].reshape(2, CMP_LEN, HEAD_DIM, CMP_HIDDEN)

    def chunk_weight(wh):
        z = jnp.zeros_like(wh)
        g0 = jnp.stack([wh, z], axis=2)
        g1 = jnp.stack([z, wh], axis=2)
        return jnp.concatenate([g0, g1], axis=-1).reshape(2, half * LANES, N_KV * CMP_HIDDEN)

    cmp_wa = chunk_weight(w1[:, :half]).astype(BF16)
    cmp_wb = chunk_weight(w1[:, half:]).astype(BF16)
    pos = cmp_pos[l]
    pos_t = jnp.broadcast_to(pos[:, :, None, :], (2, CMP_LEN, N_KV, HEAD_DIM))
    cmp_pos2 = jnp.stack([pos_t[:, :half].reshape(2, half * LANES), pos_t[:, half:].reshape(2, half * LANES)], axis=1)
    b1 = jnp.tile(cmp_b1[l], (1, N_KV))[:, None, :]
    w2 = cmp_w2[l]
    z2 = jnp.zeros_like(w2)
    cmp_w2b = jnp.concatenate([jnp.concatenate([w2, z2], axis=-1), jnp.concatenate([z2, w2], axis=-1)],
                              axis=1).astype(BF16)
    wr = router_w[l].T
    wr_hi = wr.astype(BF16)
    wr_lo = (wr - wr_hi.astype(F32)).astype(BF16)
    w_gu = w_gate_up[l]
    b_gu = _regroup_bias(b_gate_up[l])[:, None, :]
    return dict(
        norm1=norm1[l], norm2=norm2[l], w_in_parts=w_in_parts,
        w_pool=w_pool[l].astype(BF16), pool_scale=pool_scale[l], w_proj_a=w_proj_a[l].astype(BF16),
        k_gain2=jnp.stack([jnp.tile(k_norm[l, 1], N_KV), jnp.tile(k_norm[l, 2], N_KV)], axis=0),
        kc_gain=jnp.tile(k_norm[l, 0], N_KV)[None, :], q_gain2=jnp.tile(q_norm[l], 2)[None, :],
        cmp_pos2=cmp_pos2, cmp_wa=cmp_wa, cmp_wb=cmp_wb, cmp_b1=b1, cmp_w2=cmp_w2b,
        w_proj_b=w_proj_b[l].astype(BF16), w_out=w_out[l].astype(BF16),
        wr_hi=wr_hi, wr_lo=wr_lo, router_b=router_b[l],
        w_gu=w_gu, b_gu=b_gu, w_d=w_down[l].astype(BF16), b_d=b_down[l][:, None, :],
    )


def kernel(x, c, norm1, norm2, ada_w, ada_b, w_in, w_pool, pool_scale, q_norm, k_norm, cmp_pos, cmp_w1, cmp_b1,
           cmp_w2, w_proj_a, w_proj_b, w_out, router_w, router_b, w_gate_up, b_gate_up, w_down, b_down):
    batch, seq, d = x.shape
    depth = norm1.shape[0]
    assert seq % KV_TILE == 0 and seq % ROUTE_TILE == 0 and seq >= WINDOW + Q_TILE and d % LANES == 0
    assert (batch * seq * TOP_K) % EXPERT_ROWS == 0 and (batch * seq) % DISPATCH_TILE == 0
    assert seq % COMBINE_TILE == 0 and seq % Q_TILE == 0 and seq % ROW_TILE == 0

    pos = np.arange(seq)
    cq, sq = _rope_tables(pos)
    cc, sc = _rope_tables(np.arange(seq // CMP_STRIDE) * CMP_STRIDE + CMP_LEN - 1)
    tile2 = lambda a: jnp.asarray(np.concatenate([a, a], axis=1))
    tri = jnp.asarray(np.triu(np.ones((ROUTE_TILE, ROUTE_TILE), np.float32), 1), BF16)
    consts = (tile2(cq), tile2(sq), tile2(cc), tile2(sc), *_attention_constants(seq), tri)

    c_pad = jnp.pad(c, ((0, SUBLANES - batch % SUBLANES if batch % SUBLANES else 0), (0, 0)))
    x2 = x.reshape(batch * seq, d)
    w_gate_up = _regroup_gate_up(w_gate_up)
    for l in range(depth):
        mod = _modulation(c_pad, ada_w[l], ada_b[l])[:batch]
        p = _prep_layer(l, norm1, norm2, w_in, w_pool, pool_scale, q_norm, k_norm, cmp_pos, cmp_w1, cmp_b1, cmp_w2,
                        w_proj_a, w_proj_b, w_out, router_w, router_b, w_gate_up, b_gate_up, w_down, b_down)
        x2 = _layer(x2, mod, consts, p, batch, seq)
    return x2.reshape(batch, seq, d)
```

```python
import functools

import numpy as np
import jax
import jax.numpy as jnp
from jax import lax
from jax.experimental import pallas as pl
from jax.experimental.pallas import tpu as pltpu

F32 = jnp.float32
BF16 = jnp.bfloat16
I32 = jnp.int32

POOL_WINDOWS = (2, 4, 8, 16)
POOL_GROUP = 128
POOL_WIDTH = 512
HEAD_DIM = 64
N_KV = 2
HEADS_PER_KV = 8
ROPE_DIM = 16
ROPE_HALF = 8
ROPE_THETA = 500000.0
CMP_LEN = 32
CMP_STRIDE = 16
CMP_HIDDEN = 256
SEL_LEN = 64
SEL_TOP = 16
WINDOW = 512
N_EXPERTS = 32
TOP_K = 4
SWIGLU_LIMIT = 7.0
SWIGLU_ALPHA = 1.702
EPS = 1e-6
NEG = -1e30
QK_SCALE = HEAD_DIM ** -0.5 * 1.4426950408889634

LANES = 128
SUBLANES = 8
VMEM_LIMIT = 56 * 1024 * 1024

ROW_TILE = 512
POOL_HALO = 16
Q_TILE = 256
KV_TILE = 512
ROUTE_TILE = 512
EXPERT_ROWS = 512
DISPATCH_TILE = 1024
COMBINE_TILE = 256


def _cparams(sem):
    return pltpu.CompilerParams(dimension_semantics=sem, vmem_limit_bytes=VMEM_LIMIT)


def _bdot(a, b):
    return jnp.dot(a, b, preferred_element_type=F32)


def _split(a):
    hi = a.astype(BF16)
    lo = (a - hi.astype(F32)).astype(BF16)
    return hi, lo


def _sigmoid(x):
    return 1.0 / (1.0 + jnp.exp(-x))


def _mod_kernel(c_ref, w_ref, b_ref, o_ref):
    c = c_ref[...]
    a = c * _sigmoid(c)
    a_hi, a_lo = _split(a)
    w_hi, w_lo = _split(w_ref[...])
    o_ref[...] = _bdot(a_hi, w_hi) + _bdot(a_lo, w_hi) + _bdot(a_hi, w_lo) + b_ref[...]


def _modulation(c_pad, ada_w, ada_b):
    rows, d = c_pad.shape
    n = ada_w.shape[1]
    tn = 512
    return pl.pallas_call(
        _mod_kernel,
        out_shape=jax.ShapeDtypeStruct((rows, n), F32),
        grid=(n // tn,),
        in_specs=[pl.BlockSpec((rows, d), lambda i: (0, 0)),
                  pl.BlockSpec((d, tn), lambda i: (0, i)),
                  pl.BlockSpec((1, tn), lambda i: (0, i))],
        out_specs=pl.BlockSpec((rows, tn), lambda i: (0, i)),
        compiler_params=_cparams(("arbitrary",)),
        name="modulation",
    )(c_pad, ada_w, ada_b.reshape(1, n))


def _pool_mixer(u, t_in_seq, wp_ref, ps_ref, wa_ref, ext_ref):
    ts = u.shape[0]

    @pl.when(t_in_seq == 0)
    def _():
        ext_ref[0:POOL_HALO, :] = jnp.zeros((POOL_HALO, POOL_WIDTH), F32)

    @pl.when(t_in_seq > 0)
    def _():
        ext_ref[0:POOL_HALO, :] = ext_ref[ts:ts + POOL_HALO, :]

    ext_ref[POOL_HALO:POOL_HALO + ts, :] = u
    t = t_in_seq * ts + lax.broadcasted_iota(I32, (ts, 1), 0)
    outs = []
    for gi, win in enumerate(POOL_WINDOWS):
        lo, hi = gi * POOL_GROUP, (gi + 1) * POOL_GROUP
        ug = u[:, lo:hi]
        total = ug
        for jj in range(1, win):
            total = total + ext_ref[POOL_HALO - jj:POOL_HALO - jj + ts, lo:hi]
        cnt = jnp.minimum(t + 1, win).astype(F32)
        dlt = total / cnt - ug
        outs.append(_bdot(dlt.astype(BF16), wp_ref[gi]))
    y = jnp.concatenate(outs, axis=-1) * ps_ref[...]
    return _bdot(y.astype(BF16), wa_ref[...])


def _inproj_kernel(x_ref, g_ref, sc_ref, sh_ref, qg_ref, kg_ref, cos_ref, sin_ref,
                   wu_ref, wq_ref, wkc_ref, wvc_ref, wkv_ref, wgl_ref, wmg_ref, wp_ref, ps_ref, wa_ref,
                   ya_ref, q_ref, kc_ref, vc_ref, ks_ref, vs_ref, kw_ref, vw_ref, gl_ref, mg_ref,
                   ext_ref, *, tiles_per_seq):
    x = x_ref[...]
    ms = jnp.mean(x * x, axis=-1, keepdims=True)
    h = x * lax.rsqrt(ms + EPS) * g_ref[...]
    h = h * (1.0 + sc_ref[0]) + sh_ref[0]
    hb = h.astype(BF16)
    cos, sin = cos_ref[...], sin_ref[...]

    yq = _bdot(hb, wq_ref[...])
    kv = _bdot(hb, wkv_ref[...])

    for j in range(yq.shape[1] // LANES):
        q2 = _norm_rope_pair(yq[:, j * LANES:(j + 1) * LANES], qg_ref[...], cos, sin)
        q_ref[:, j * LANES:(j + 1) * LANES] = (q2 * QK_SCALE).astype(BF16)

    mg_ref[...] = _bdot(hb, wmg_ref[...]).astype(BF16)
    u = _bdot(hb, wu_ref[...])

    ks = _norm_rope_pair(kv[:, 0:LANES], kg_ref[0:1, :], cos, sin).T.astype(BF16)
    kw = _norm_rope_pair(kv[:, 2 * LANES:3 * LANES], kg_ref[1:2, :], cos, sin).T.astype(BF16)
    for gi in range(N_KV):
        ks_ref[0, gi] = ks[gi * HEAD_DIM:(gi + 1) * HEAD_DIM, :]
        kw_ref[0, gi] = kw[gi * HEAD_DIM:(gi + 1) * HEAD_DIM, :]
        vs_ref[0, gi] = _values_with_ones(kv[:, LANES:2 * LANES], gi)
        vw_ref[0, gi] = _values_with_ones(kv[:, 3 * LANES:4 * LANES], gi)

    kc_ref[...] = _bdot(hb, wkc_ref[...])
    vc_ref[...] = _bdot(hb, wvc_ref[...])
    gl_ref[...] = _bdot(hb, wgl_ref[...])
    ya_ref[...] = _pool_mixer(u, pl.program_id(0) % tiles_per_seq, wp_ref, ps_ref, wa_ref, ext_ref).astype(BF16)


def _in_projection(x2, gain, scale, shift, q_gain2, k_gain2, cos2, sin2, weights, w_pool, pool_scale, w_proj_a,
                   batch, seq):
    n, d = x2.shape
    tm = ROW_TILE
    tps = seq // tm
    w_u, w_q, w_kc, w_vc, w_kv, w_gl, w_mg = weights
    const = lambda a: pl.BlockSpec(a.shape, lambda i: (0,) * a.ndim)
    rows = lambda width: pl.BlockSpec((tm, width), lambda i: (i, 0))
    per_b = pl.BlockSpec((1, 1, d), lambda i: (i // tps, 0, 0))
    table = pl.BlockSpec((tm, LANES), lambda i: (i % tps, 0))
    kt_shape = jax.ShapeDtypeStruct((batch, N_KV, HEAD_DIM, seq), BF16)
    v_shape = jax.ShapeDtypeStruct((batch, N_KV, seq, LANES), BF16)
    kt_spec = pl.BlockSpec((1, N_KV, HEAD_DIM, tm), lambda i: (i // tps, 0, 0, i % tps))
    v_spec = pl.BlockSpec((1, N_KV, tm, LANES), lambda i: (i // tps, 0, i % tps, 0))
    f32_out = lambda width: jax.ShapeDtypeStruct((n, width), F32)
    bf16_out = lambda width: jax.ShapeDtypeStruct((n, width), BF16)
    pool_scale2 = pool_scale.reshape(1, POOL_WIDTH)
    return pl.pallas_call(
        functools.partial(_inproj_kernel, tiles_per_seq=tps),
        out_shape=[bf16_out(w_proj_a.shape[1]), bf16_out(w_q.shape[1]),
                   f32_out(w_kc.shape[1]), f32_out(w_vc.shape[1]), kt_shape, v_shape, kt_shape, v_shape,
                   f32_out(w_gl.shape[1]), bf16_out(w_mg.shape[1])],
        grid=(n // tm,),
        in_specs=[rows(d), const(gain.reshape(1, d)), per_b, per_b, const(q_gain2), const(k_gain2), table, table,
                  const(w_u), const(w_q), const(w_kc), const(w_vc), const(w_kv), const(w_gl), const(w_mg),
                  const(w_pool), const(pool_scale2), const(w_proj_a)],
        out_specs=[rows(w_proj_a.shape[1]), rows(w_q.shape[1]), rows(w_kc.shape[1]), rows(w_vc.shape[1]),
                   kt_spec, v_spec, kt_spec, v_spec, rows(w_gl.shape[1]), rows(w_mg.shape[1])],
        scratch_shapes=[pltpu.VMEM((tm + POOL_HALO, POOL_WIDTH), F32)],
        compiler_params=_cparams(("arbitrary",)),
        name="in_projection",
    )(x2, gain.reshape(1, d), scale[:, None, :], shift[:, None, :], q_gain2, k_gain2, cos2, sin2,
      w_u, w_q, w_kc, w_vc, w_kv, w_gl, w_mg, w_pool, pool_scale2, w_proj_a)


def _rope_tables(pos):
    inv_freq = np.float32(ROPE_THETA) ** (-(np.arange(ROPE_HALF, dtype=np.float32)) / np.float32(ROPE_HALF))
    ang = pos.astype(np.float32)[:, None] * inv_freq.astype(np.float32)[None, :]
    cos, sin = np.cos(ang), np.sin(ang)
    c = np.ones((pos.shape[0], HEAD_DIM), np.float32)
    s = np.zeros((pos.shape[0], HEAD_DIM), np.float32)
    c[:, :ROPE_HALF] = cos
    c[:, ROPE_HALF:ROPE_DIM] = cos
    s[:, :ROPE_HALF] = -sin
    s[:, ROPE_HALF:ROPE_DIM] = sin
    return c, s


def _norm_rope_pair(x, gain, cos, sin):
    rows = x.shape[0]
    lane = lax.broadcasted_iota(I32, (rows, LANES), 1)
    first = lane < HEAD_DIM
    x2 = x * x
    s0 = jnp.sum(jnp.where(first, x2, 0.0), axis=-1, keepdims=True)
    s1 = jnp.sum(jnp.where(first, 0.0, x2), axis=-1, keepdims=True)
    r = lax.rsqrt(jnp.where(first, s0, s1) * (1.0 / HEAD_DIM) + EPS)
    y = x * r * gain
    low = (lane & (HEAD_DIM - 1)) < ROPE_HALF
    sw = jnp.where(low, pltpu.roll(y, LANES - ROPE_HALF, 1), pltpu.roll(y, ROPE_HALF, 1))
    return y * cos + sw * sin


def _values_with_ones(v, gi):
    lane = lax.broadcasted_iota(I32, v.shape, 1)
    vg = v if gi == 0 else pltpu.roll(v, HEAD_DIM, 1)
    return jnp.where(lane < HEAD_DIM, vg, 1.0).astype(BF16)


def _compress_kernel(k_ref, v_ref, pos_ref, wa_ref, wb_ref, b1_ref, w2_ref, gain_ref, cos_ref, sin_ref,
                     kc_ref, vc_ref):
    nch = k_ref.shape[0]

    def mlp(x, idx):
        xa = (x + pos_ref[idx, 0:1, :]).astype(BF16)
        xb = (x + pos_ref[idx, 1:2, :]).astype(BF16)
        a = _bdot(xa, wa_ref[idx])
        b = _bdot(xb, wb_ref[idx])
        b_next = jnp.concatenate([b[1:nch, :], jnp.zeros((1, b.shape[1]), F32)], axis=0)
        pre = a + b_next + b1_ref[idx]
        hid = pre * _sigmoid(pre)
        return _bdot(hid.astype(BF16), w2_ref[idx])

    kc = _norm_rope_pair(mlp(k_ref[...], 0), gain_ref[...], cos_ref[...], sin_ref[...]).T.astype(BF16)
    vc = mlp(v_ref[...], 1)
    for gi in range(N_KV):
        kc_ref[0, gi] = kc[gi * HEAD_DIM:(gi + 1) * HEAD_DIM, :]
        vc_ref[0, gi] = _values_with_ones(vc, gi)


def _compress(kcm, vcm, pos2, wa, wb, b1, w2, gain, cos_c, sin_c, batch, seq):
    nch = seq // CMP_STRIDE
    width = CMP_STRIDE * LANES
    full = lambda a: pl.BlockSpec(a.shape, lambda b: (0,) * a.ndim)
    return pl.pallas_call(
        _compress_kernel,
        out_shape=[jax.ShapeDtypeStruct((batch, N_KV, HEAD_DIM, nch), BF16),
                   jax.ShapeDtypeStruct((batch, N_KV, nch, LANES), BF16)],
        grid=(batch,),
        in_specs=[pl.BlockSpec((nch, width), lambda b: (b, 0)),
                  pl.BlockSpec((nch, width), lambda b: (b, 0)),
                  full(pos2), full(wa), full(wb), full(b1), full(w2), full(gain), full(cos_c), full(sin_c)],
        out_specs=[pl.BlockSpec((1, N_KV, HEAD_DIM, nch), lambda b: (b, 0, 0, 0)),
                   pl.BlockSpec((1, N_KV, nch, LANES), lambda b: (b, 0, 0, 0))],
        compiler_params=_cparams(("arbitrary",)),
        name="compress",
    )(kcm.reshape(batch * nch, width), vcm.reshape(batch * nch, width), pos2, wa, wb, b1, w2, gain, cos_c, sin_c)


def _attn_kernel(q_ref, gl_ref, ks_ref, vs_ref, kw_ref, vw_ref, kc_ref, vc_ref, ovl_ref, eneg_ref,
                 o_ref, qa_scr, pre_scr, m_scr, acc_scr, *, seq):
    H, QB, TK = HEADS_PER_KV, Q_TILE, KV_TILE
    q0 = pl.program_id(2) * QB
    ncp = seq // CMP_STRIDE
    nsel = seq // SEL_LEN
    n_top = min(SEL_TOP, nsel)
    nt_dims = (((1,), (1,)), ((), ()))

    def row_sum(a):
        return a[:, HEAD_DIM:HEAD_DIM + 1]

    def pair_keys(k_t, lead=None):
        z = jnp.zeros_like(k_t)
        head = [] if lead is None else [lead]
        return (jnp.concatenate(head + [k_t, z], axis=0), jnp.concatenate(head + [z, k_t], axis=0))

    q_pairs = [q_ref[:, j * LANES:(j + 1) * LANES] for j in range(H // 2)]
    qpos = q0 + lax.broadcasted_iota(I32, (QB, 1), 0)

    def one_shot_branch(k_t, v_aug, mask, between=None):
        k_var = pair_keys(k_t)
        score = lambda h: _bdot(q_pairs[h // 2], k_var[h % 2])
        probs, accs = [], []
        ahead = 1
        scores = [score(h) for h in range(ahead)]
        for h in range(H):
            s = jnp.where(mask, scores[h], NEG)
            scores[h] = None
            if h + ahead < H:
                scores.append(score(h + ahead))
            p = jnp.exp2(s - jnp.max(s, axis=-1, keepdims=True))
            probs.append(p)
            accs.append(_bdot(p.astype(BF16), v_aug))
            if between is not None:
                between(h, probs, accs)
        return accs

    gs = _sigmoid(gl_ref[...])

    def compressed_branch(width):
        valid = (lax.broadcasted_iota(I32, (QB, width), 1) * CMP_STRIDE + (CMP_LEN - 1)) <= qpos
        has_key = qpos >= CMP_LEN - 1
        imp_state = {"ps": None}

        def finish_cmp_head(h, probs, accs):
            inv = jnp.where(has_key, 1.0 / row_sum(accs[h]), 0.0)
            pre_scr[h] = accs[h] * (gs[:, h:h + 1] * inv)
            term = probs[h] * inv
            imp_state["ps"] = term if imp_state["ps"] is None else imp_state["ps"] + term
            probs[h] = None

        def after_cmp_head(h, probs, accs):
            if h > 0:
                finish_cmp_head(h - 1, probs, accs)
            if h == H - 1:
                finish_cmp_head(h, probs, accs)

        one_shot_branch(kc_ref[0, 0, :, 0:width], vc_ref[0, 0, 0:width, :], valid, between=after_cmp_head)
        ps_hi, ps_lo = _split(imp_state["ps"])
        ovl_t = ovl_ref[:, 0:width]
        return (lax.dot_general(ovl_t, ps_hi, nt_dims, preferred_element_type=F32)
                + lax.dot_general(ovl_t, ps_lo, nt_dims, preferred_element_type=F32))

    imp = compressed_branch(ncp)

    sidx = lax.broadcasted_iota(I32, (nsel, QB), 0)
    cur = (q0 + lax.broadcasted_iota(I32, (nsel, QB), 1)) // SEL_LEN
    causal = sidx <= cur
    forced = (sidx == cur) | (sidx == 0)
    n_forced = 2
    pick_state = {"v": jnp.where(forced, -3e38, jnp.where(causal, imp, -1.0)), "sel": forced}

    def pick_next():
        v = pick_state["v"]
        mx = jnp.max(v, axis=0, keepdims=True)
        first = jnp.min(jnp.where(v == mx, sidx, nsel), axis=0, keepdims=True)
        pick = sidx == first
        pick_state["sel"] = pick_state["sel"] | pick
        pick_state["v"] = jnp.where(pick, -3e38, v)

    def picks_after_head(h, probs, accs):
        probs[h] = None
        n_pick = n_top - n_forced
        for _ in range(n_pick // H + (1 if h < n_pick % H else 0)):
            pick_next()

    span = WINDOW + QB
    base = pl.multiple_of(jnp.maximum(q0 - WINDOW, 0), LANES)
    dist = qpos - (base + lax.broadcasted_iota(I32, (QB, span), 1))
    inwin = (dist >= 0) & (dist < WINDOW)
    accs_w = one_shot_branch(kw_ref[0, 0, :, pl.ds(base, span)], vw_ref[0, 0, pl.ds(base, span), :],
                             inwin, between=picks_after_head)
    for h in range(H):
        pre_scr[h] = pre_scr[h] + accs_w[h] * (gs[:, 2 * H + h:2 * H + h + 1] / row_sum(accs_w[h]))

    notsel = jnp.where(pick_state["sel"] & causal, 0.0, 1.0).T.astype(BF16)
    for j in range(H // 2):
        qa_scr[j] = jnp.concatenate([notsel, q_pairs[j]], axis=1)

    m_scr[...] = jnp.full(m_scr.shape, NEG, F32)
    acc_scr[...] = jnp.zeros(acc_scr.shape, F32)

    def sel_tile(kt, causal_tile):
        off = pl.multiple_of(kt * TK, TK)
        k_var = pair_keys(ks_ref[0, 0, :, pl.ds(off, TK)], lead=eneg_ref[:, pl.ds(off, TK)])
        score = lambda h: _bdot(qa_scr[h // 2], k_var[h % 2])
        v_t = vs_ref[0, 0, pl.ds(off, TK), :]
        if causal_tile:
            keep = (off + lax.broadcasted_iota(I32, (QB, TK), 1)) <= qpos
        ahead = 4
        scores = [score(h) for h in range(ahead)]
        for h in range(H):
            s = scores[h]
            scores[h] = None
            if h + ahead < H:
                scores.append(score(h + ahead))
            if causal_tile:
                s = jnp.where(keep, s, NEG)
            m_old = m_scr[h]
            m_new = jnp.maximum(m_old, jnp.max(s, axis=-1, keepdims=True))
            alpha = jnp.exp2(m_old - m_new)
            pt = jnp.concatenate([jnp.exp2(s[:, c * LANES:(c + 1) * LANES] - m_new).astype(BF16)
                                  for c in range(TK // LANES)], axis=1)
            acc_scr[h] = alpha * acc_scr[h] + _bdot(pt, v_t)
            m_scr[h] = m_new

    n_tiles = (q0 + QB - 1) // TK + 1

    def body(kt, carry):
        sel_tile(kt, False)
        return carry

    lax.fori_loop(0, n_tiles - 1, body, 0)
    sel_tile(n_tiles - 1, True)

    def head_out(h):
        a_s = acc_scr[h]
        return pre_scr[h] + a_s * (gs[:, H + h:H + h + 1] / row_sum(a_s))

    low = lax.broadcasted_iota(I32, (QB, LANES), 1) < HEAD_DIM
    for j in range(H // 2):
        slab = jnp.where(low, head_out(2 * j), pltpu.roll(head_out(2 * j + 1), HEAD_DIM, 1))
        o_ref[:, j * LANES:(j + 1) * LANES] = slab.astype(BF16)


def _attention(q, gl, ks, vs, kw, vw, kc, vc, ovl, eneg, batch, seq):
    n = q.shape[0]
    QB = Q_TILE
    nq = seq // QB
    H = HEADS_PER_KV
    gw = HEADS_PER_KV * HEAD_DIM
    ncp = seq // CMP_STRIDE
    nsel = seq // SEL_LEN
    const2 = lambda a: pl.BlockSpec(a.shape, lambda b, g, j: (0,) * a.ndim)
    return pl.pallas_call(
        functools.partial(_attn_kernel, seq=seq),
        out_shape=jax.ShapeDtypeStruct((n, N_KV * gw), BF16),
        grid=(batch, N_KV, nq),
        in_specs=[pl.BlockSpec((QB, gw), lambda b, g, j: (b * nq + j, g)),
                  pl.BlockSpec((QB, LANES), lambda b, g, j: (b * nq + j, g)),
                  pl.BlockSpec((1, 1, HEAD_DIM, seq), lambda b, g, j: (b, g, 0, 0)),
                  pl.BlockSpec((1, 1, seq, LANES), lambda b, g, j: (b, g, 0, 0)),
                  pl.BlockSpec((1, 1, HEAD_DIM, seq), lambda b, g, j: (b, g, 0, 0)),
                  pl.BlockSpec((1, 1, seq, LANES), lambda b, g, j: (b, g, 0, 0)),
                  pl.BlockSpec((1, 1, HEAD_DIM, ncp), lambda b, g, j: (b, g, 0, 0)),
                  pl.BlockSpec((1, 1, ncp, LANES), lambda b, g, j: (b, g, 0, 0)),
                  const2(ovl), const2(eneg)],
        out_specs=pl.BlockSpec((QB, gw), lambda b, g, j: (b * nq + j, g)),
        scratch_shapes=[pltpu.VMEM((H // 2, QB, nsel + LANES), BF16), pltpu.VMEM((H, QB, LANES), F32),
                        pltpu.VMEM((H, QB, LANES), F32), pltpu.VMEM((H, QB, LANES), F32)],
        compiler_params=_cparams(("arbitrary", "arbitrary", "arbitrary")),
        name="sparse_attention",
    )(q, gl, ks, vs, kw, vw, kc, vc, ovl, eneg)


def _merge_kernel(x_ref, att_ref, ya_ref, mg_ref, g1_ref, wb_ref, wo_ref, o_ref):
    d = x_ref.shape[1]
    y_b = _bdot(att_ref[...], wb_ref[...])
    mg = mg_ref[...].astype(F32)
    merged = _sigmoid(mg[:, 0:d]) * ya_ref[...].astype(F32) + _sigmoid(mg[:, d:2 * d]) * y_b
    o_ref[...] = x_ref[...] + g1_ref[0] * _bdot(merged.astype(BF16), wo_ref[...])


def _merge(x2, att, y_a, mg, gate1, w_proj_b, w_out, seq):
    n, d = x2.shape
    tm = ROW_TILE
    tiles_per_b = seq // tm
    return pl.pallas_call(
        _merge_kernel,
        out_shape=jax.ShapeDtypeStruct((n, d), F32),
        grid=(n // tm,),
        in_specs=[pl.BlockSpec((tm, d), lambda i: (i, 0)),
                  pl.BlockSpec((tm, d), lambda i: (i, 0)),
                  pl.BlockSpec((tm, d), lambda i: (i, 0)),
                  pl.BlockSpec((tm, 2 * d), lambda i: (i, 0)),
                  pl.BlockSpec((1, 1, d), lambda i: (i // tiles_per_b, 0, 0)),
                  pl.BlockSpec(w_proj_b.shape, lambda i: (0, 0)),
                  pl.BlockSpec(w_out.shape, lambda i: (0, 0))],
        out_specs=pl.BlockSpec((tm, d), lambda i: (i, 0)),
        compiler_params=_cparams(("arbitrary",)),
        name="merge_out_projection",
    )(x2, att, y_a, mg, gate1[:, None, :], w_proj_b, w_out)


def _router_kernel(x_ref, g_ref, sc_ref, sh_ref, wh_ref, wl_ref, rb_ref, tri_ref,
                   h_ref, e_ref, gt_ref, pos_ref, cnt_ref, carry_ref):
    i = pl.program_id(0)
    tm, d = x_ref.shape

    @pl.when(i == 0)
    def _():
        carry_ref[...] = jnp.zeros((N_EXPERTS, 1), F32)

    x = x_ref[...]
    ms = jnp.mean(x * x, axis=-1, keepdims=True)
    h = x * lax.rsqrt(ms + EPS) * g_ref[...]
    h = h * (1.0 + sc_ref[0]) + sh_ref[0]
    for s in range(d // LANES):
        h_ref[pl.ds(s, tm, stride=SUBLANES), :] = h[:, s * LANES:(s + 1) * LANES]

    h_hi, h_lo = _split(h)
    nt = (((1,), (1,)), ((), ()))
    logits = (lax.dot_general(wh_ref[...], h_hi, nt, preferred_element_type=F32)
              + lax.dot_general(wh_ref[...], h_lo, nt, preferred_element_type=F32)
              + lax.dot_general(wl_ref[...], h_hi, nt, preferred_element_type=F32)
              + rb_ref[...])
    eidx = lax.broadcasted_iota(I32, (N_EXPERTS, tm), 0)
    onehot = jnp.zeros((N_EXPERTS, tm), F32)
    picks, vals, ids = [], [], []
    for _ in range(TOP_K):
        mx = jnp.max(logits, axis=0, keepdims=True)
        first = jnp.min(jnp.where(logits == mx, eidx, N_EXPERTS), axis=0, keepdims=True)
        pick = eidx == first
        picks.append(pick)
        vals.append(mx)
        ids.append(first)
        onehot = jnp.where(pick, 1.0, onehot)
        logits = jnp.where(pick, -3e38, logits)
    ex = [jnp.exp(vk - vals[0]) for vk in vals]
    den = ex[0] + ex[1] + ex[2] + ex[3]
    before = _bdot(onehot.astype(BF16), tri_ref[...]) + carry_ref[...]
    pad = SUBLANES - TOP_K
    pos = [jnp.sum(jnp.where(pk, before, 0.0), axis=0, keepdims=True) for pk in picks]
    e_ref[...] = jnp.concatenate(ids + [jnp.zeros((pad, tm), I32)], axis=0)
    gt_ref[...] = jnp.concatenate([e / den for e in ex] + [jnp.zeros((pad, tm), F32)], axis=0)
    pos_ref[...] = jnp.concatenate(pos + [jnp.zeros((pad, tm), F32)], axis=0).astype(I32)
    carry_ref[...] = carry_ref[...] + jnp.sum(onehot, axis=1, keepdims=True)
    cnt_ref[...] = carry_ref[...]


def _router(x2, gain, scale, shift, wr_hi, wr_lo, rb, tri, seq):
    n, d = x2.shape
    tm = ROUTE_TILE
    tiles_per_b = seq // tm
    nsub = d // LANES
    return pl.pallas_call(
        _router_kernel,
        out_shape=[jax.ShapeDtypeStruct((n * nsub, LANES), F32),
                   jax.ShapeDtypeStruct((SUBLANES, n), I32),
                   jax.ShapeDtypeStruct((SUBLANES, n), F32),
                   jax.ShapeDtypeStruct((SUBLANES, n), I32),
                   jax.ShapeDtypeStruct((N_EXPERTS, 1), F32)],
        grid=(n // tm,),
        in_specs=[pl.BlockSpec((tm, d), lambda i: (i, 0)),
                  pl.BlockSpec((1, d), lambda i: (0, 0)),
                  pl.BlockSpec((1, 1, d), lambda i: (i // tiles_per_b, 0, 0)),
                  pl.BlockSpec((1, 1, d), lambda i: (i // tiles_per_b, 0, 0)),
                  pl.BlockSpec(wr_hi.shape, lambda i: (0, 0)),
                  pl.BlockSpec(wr_lo.shape, lambda i: (0, 0)),
                  pl.BlockSpec((N_EXPERTS, 1), lambda i: (0, 0)),
                  pl.BlockSpec((tm, tm), lambda i: (0, 0))],
        out_specs=[pl.BlockSpec((tm * nsub, LANES), lambda i: (i, 0)),
                   pl.BlockSpec((SUBLANES, tm), lambda i: (0, i)),
                   pl.BlockSpec((SUBLANES, tm), lambda i: (0, i)),
                   pl.BlockSpec((SUBLANES, tm), lambda i: (0, i)),
                   pl.BlockSpec((N_EXPERTS, 1), lambda i: (0, 0))],
        scratch_shapes=[pltpu.VMEM((N_EXPERTS, 1), F32)],
        compiler_params=_cparams(("arbitrary",)),
        name="router",
    )(x2, gain.reshape(1, d), scale[:, None, :], shift[:, None, :], wr_hi, wr_lo, rb.reshape(N_EXPERTS, 1), tri)


def _dispatch_kernel(dest_ref, fill_ref, h_ref, xs_ref, zero_ref, sem, *, nsub):
    i = pl.program_id(0)
    tt = DISPATCH_TILE

    def row_copy(n, k):
        dst = dest_ref[0, 0, k * tt + n]
        return pltpu.make_async_copy(h_ref.at[pl.ds(n * nsub, nsub), :],
                                     xs_ref.at[pl.ds(dst * nsub, nsub), :], sem)

    def fill_copy(r):
        return pltpu.make_async_copy(zero_ref, xs_ref.at[pl.ds(r * nsub, nsub), :], sem)

    def for_fill_rows(fn):
        def per_range(e, carry):
            lax.fori_loop(fill_ref[2 * e], fill_ref[2 * e + 1], lambda r, c: (fn(fill_copy(r)), c)[1], 0)
            return carry

        lax.fori_loop(0, fill_ref.shape[0] // 2, per_range, 0)

    def for_token_rows(fn):
        def per_token(n, carry):
            for k in range(TOP_K):
                fn(row_copy(n, k))
            return carry

        lax.fori_loop(0, tt, per_token, 0)

    @pl.when(i == 0)
    def _():
        zero_ref[...] = jnp.zeros(zero_ref.shape, F32)
        for_fill_rows(lambda cp: cp.start())

    for_token_rows(lambda cp: cp.start())
    for_token_rows(lambda cp: cp.wait())

    @pl.when(i == 0)
    def _():
        for_fill_rows(lambda cp: cp.wait())


def _tile_indices(dest, tt):
    k, n = dest.shape
    return dest.reshape(k, n // tt, tt).transpose(1, 0, 2).reshape(n // tt, 1, k * tt)


def _dispatch(h_rows, dest, fill, n_rows, nsub):
    n_tok = h_rows.shape[0] // nsub
    tt = DISPATCH_TILE
    dest_tiles = _tile_indices(dest, tt)
    grid_spec = pltpu.PrefetchScalarGridSpec(
        num_scalar_prefetch=0,
        grid=(n_tok // tt,),
        in_specs=[pl.BlockSpec((1, 1, TOP_K * tt), lambda i: (i, 0, 0), memory_space=pltpu.SMEM),
                  pl.BlockSpec(memory_space=pltpu.SMEM),
                  pl.BlockSpec((tt * nsub, LANES), lambda i: (i, 0))],
        out_specs=pl.BlockSpec(memory_space=pl.ANY),
        scratch_shapes=[pltpu.VMEM((nsub, LANES), F32), pltpu.SemaphoreType.DMA(())],
    )
    return pl.pallas_call(
        functools.partial(_dispatch_kernel, nsub=nsub),
        out_shape=jax.ShapeDtypeStruct((n_rows * nsub, LANES), F32),
        grid_spec=grid_spec,
        compiler_params=pltpu.CompilerParams(dimension_semantics=("arbitrary",), vmem_limit_bytes=VMEM_LIMIT,
                                             has_side_effects=True),
        name="dispatch",
    )(dest_tiles, fill, h_rows)


PAIR = 2 * LANES


def _pair_permutation():
    p = np.zeros((PAIR, PAIR), np.float32)
    p[2 * np.arange(LANES), np.arange(LANES)] = 1.0
    p[2 * np.arange(LANES) + 1, LANES + np.arange(LANES)] = 1.0
    return p


def _regroup_kernel(w_ref, p_ref, o_ref):
    w = w_ref[0].astype(BF16)
    for j in range(w.shape[1] // PAIR):
        o_ref[0, :, j * PAIR:(j + 1) * PAIR] = _bdot(w[:, j * PAIR:(j + 1) * PAIR], p_ref[...]).astype(BF16)


def _regroup_gate_up(w_gate_up):
    depth, n_e, d, f2 = w_gate_up.shape
    tm = 512
    perm = jnp.asarray(_pair_permutation(), BF16)
    out = pl.pallas_call(
        _regroup_kernel,
        out_shape=jax.ShapeDtypeStruct((depth * n_e, d, f2), BF16),
        grid=(depth * n_e, d // tm),
        in_specs=[pl.BlockSpec((1, tm, f2), lambda e, i: (e, i, 0)),
                  pl.BlockSpec((PAIR, PAIR), lambda e, i: (0, 0))],
        out_specs=pl.BlockSpec((1, tm, f2), lambda e, i: (e, i, 0)),
        compiler_params=_cparams(("arbitrary", "arbitrary")),
        name="regroup_gate_up",
    )(w_gate_up.reshape(depth * n_e, d, f2), perm)
    return out.reshape(depth, n_e, d, f2)


def _regroup_bias(b):
    lead = b.shape[:-1]
    return b.reshape(*lead, -1, LANES, 2).swapaxes(-1, -2).reshape(*lead, -1)


def _expert_kernel(be_ref, nu_ref, xs_ref, wgu_ref, bgu_ref, wd_ref, bd_ref, ys_ref, *, nsub):
    i = pl.program_id(0)
    rows = EXPERT_ROWS
    f = wd_ref.shape[1]

    @pl.when(i < nu_ref[0])
    def _():
        x = jnp.concatenate([xs_ref[pl.ds(s, rows, stride=nsub), :] for s in range(nsub)], axis=-1)
        gu = _bdot(x.astype(BF16), wgu_ref[0]) + bgu_ref[0]
        acts = []
        for j in range(f // LANES):
            g_lin = jnp.minimum(gu[:, j * PAIR:j * PAIR + LANES], SWIGLU_LIMIT)
            u_lin = jnp.clip(gu[:, j * PAIR + LANES:(j + 1) * PAIR], -SWIGLU_LIMIT, SWIGLU_LIMIT)
            acts.append(((u_lin + 1.0) * g_lin * _sigmoid(g_lin * SWIGLU_ALPHA)).astype(BF16))
        act = jnp.concatenate(acts, axis=-1)
        y = _bdot(act, wd_ref[0].astype(BF16)) + bd_ref[0]
        for s in range(nsub):
            ys_ref[pl.ds(s, rows, stride=nsub), :] = y[:, s * LANES:(s + 1) * LANES]

    @pl.when(i >= nu_ref[0])
    def _():
        ys_ref[...] = jnp.zeros(ys_ref.shape, F32)


def _experts(blk_e, n_used, xs, w_gu, b_gu, w_d, b_d, nsub):
    rows = EXPERT_ROWS
    n_blocks = xs.shape[0] // (rows * nsub)
    d, f2 = w_gu.shape[1], w_gu.shape[2]
    f = w_d.shape[1]
    row_map = lambda i, be, nu: (jnp.minimum(i, nu[0] - 1), 0)
    grid_spec = pltpu.PrefetchScalarGridSpec(
        num_scalar_prefetch=2,
        grid=(n_blocks,),
        in_specs=[pl.BlockSpec((rows * nsub, LANES), row_map),
                  pl.BlockSpec((1, d, f2), lambda i, be, nu: (be[i], 0, 0)),
                  pl.BlockSpec((1, 1, f2), lambda i, be, nu: (be[i], 0, 0)),
                  pl.BlockSpec((1, f, d), lambda i, be, nu: (be[i], 0, 0)),
                  pl.BlockSpec((1, 1, d), lambda i, be, nu: (be[i], 0, 0))],
        out_specs=pl.BlockSpec((rows * nsub, LANES), lambda i, be, nu: (i, 0)),
    )
    return pl.pallas_call(
        functools.partial(_expert_kernel, nsub=nsub),
        out_shape=jax.ShapeDtypeStruct(xs.shape, F32),
        grid_spec=grid_spec,
        compiler_params=_cparams(("arbitrary",)),
        name="experts",
    )(blk_e, n_used, xs, w_gu, b_gu, w_d, b_d)


def _combine_kernel(cur_ref, nxt_ref, ys_ref, x_ref, gt_ref, g2_ref, o_ref, buf_ref, sems, *, nsub, n_steps):
    i = pl.program_id(0)
    tt = COMBINE_TILE
    slot = i % 2

    def for_tile_rows(idx_ref, s, fn):
        def per_token(n, carry):
            for k in range(TOP_K):
                src = idx_ref[0, 0, k * tt + n]
                fn(pltpu.make_async_copy(ys_ref.at[pl.ds(src * nsub, nsub), :],
                                         buf_ref.at[s, k, pl.ds(n * nsub, nsub), :], sems.at[s]))
            return carry

        lax.fori_loop(0, tt, per_token, 0)

    @pl.when(i == 0)
    def _():
        for_tile_rows(cur_ref, 0, lambda cp: cp.start())

    @pl.when(i + 1 < n_steps)
    def _():
        for_tile_rows(nxt_ref, 1 - slot, lambda cp: cp.start())

    for_tile_rows(cur_ref, slot, lambda cp: cp.wait())
    gt = gt_ref[...]
    acc = None
    for k in range(TOP_K):
        yk = jnp.concatenate([buf_ref[slot, k, pl.ds(s, tt, stride=nsub), :] for s in range(nsub)], axis=-1)
        term = gt[:, k:k + 1] * yk
        acc = term if acc is None else acc + term
    o_ref[...] = x_ref[...] + g2_ref[0] * acc


def _combine(dest, ys, x2, gates_t, gate2, seq, nsub):
    n, d = x2.shape
    tt = COMBINE_TILE
    tiles_per_b = seq // tt
    n_steps = n // tt
    dest_tiles = _tile_indices(dest, tt)
    grid_spec = pltpu.PrefetchScalarGridSpec(
        num_scalar_prefetch=0,
        grid=(n_steps,),
        in_specs=[pl.BlockSpec((1, 1, TOP_K * tt), lambda i: (i, 0, 0), memory_space=pltpu.SMEM),
                  pl.BlockSpec((1, 1, TOP_K * tt), lambda i: (jnp.minimum(i + 1, n_steps - 1), 0, 0),
                               memory_space=pltpu.SMEM),
                  pl.BlockSpec(memory_space=pl.ANY),
                  pl.BlockSpec((tt, d), lambda i: (i, 0)),
                  pl.BlockSpec((tt, SUBLANES), lambda i: (i, 0)),
                  pl.BlockSpec((1, 1, d), lambda i: (i // tiles_per_b, 0, 0))],
        out_specs=pl.BlockSpec((tt, d), lambda i: (i, 0)),
        scratch_shapes=[pltpu.VMEM((2, TOP_K, tt * nsub, LANES), F32), pltpu.SemaphoreType.DMA((2,))],
    )
    return pl.pallas_call(
        functools.partial(_combine_kernel, nsub=nsub, n_steps=n_steps),
        out_shape=jax.ShapeDtypeStruct((n, d), F32),
        grid_spec=grid_spec,
        compiler_params=_cparams(("arbitrary",)),
        name="combine",
    )(dest_tiles, dest_tiles, ys, x2, gates_t, gate2[:, None, :])


def _attention_constants(seq):
    nsel = seq // SEL_LEN
    ncp = seq // CMP_STRIDE
    cmp_start = np.arange(ncp) * CMP_STRIDE
    sel_start = np.arange(nsel) * SEL_LEN
    ovl = ((cmp_start[:, None] < sel_start[None, :] + SEL_LEN)
           & (cmp_start[:, None] + CMP_LEN - 1 >= sel_start[None, :])).astype(np.float32)
    ovl[ncp - 1, :] = 0.0
    eneg = np.where(np.arange(seq)[None, :] // SEL_LEN == np.arange(nsel)[:, None], NEG, 0.0).astype(np.float32)
    return jnp.asarray(ovl.T, BF16), jnp.asarray(eneg, BF16)


def _layer(x2, mod, consts, p, batch, seq):
    n, d = x2.shape
    shift1, scale1, gate1, shift2, scale2, gate2 = [mod[:, k * d:(k + 1) * d] for k in range(6)]
    (cos2, sin2, cos_c, sin_c, ovl, eneg, tri) = consts

    y_a, q, kcm, vcm, ks, vs, kw, vw, gl, mg = _in_projection(
        x2, p["norm1"], scale1, shift1, p["q_gain2"], p["k_gain2"], cos2, sin2, p["w_in_parts"],
        p["w_pool"], p["pool_scale"], p["w_proj_a"], batch, seq)
    kc, vc = _compress(kcm, vcm, p["cmp_pos2"], p["cmp_wa"], p["cmp_wb"], p["cmp_b1"], p["cmp_w2"],
                       p["kc_gain"], cos_c, sin_c, batch, seq)
    att = _attention(q, gl, ks, vs, kw, vw, kc, vc, ovl, eneg, batch, seq)
    x2 = _merge(x2, att, y_a, mg, gate1, p["w_proj_b"], p["w_out"], seq)

    nsub = d // LANES
    h_rows, top_e, gates, pos, counts = _router(x2, p["norm2"], scale2, shift2, p["wr_hi"], p["wr_lo"],
                                                p["router_b"], tri, seq)
    rows = EXPERT_ROWS
    n_blocks = -(-n * TOP_K // rows) + N_EXPERTS
    cnt = counts[:, 0].astype(I32)
    padded = (cnt + rows - 1) // rows * rows
    pad_end = jnp.cumsum(padded)
    pad_start = pad_end - padded
    e_ids = jnp.arange(N_EXPERTS, dtype=I32)
    dest = jnp.sum(jnp.where(top_e[:TOP_K, :, None] == e_ids, pad_start, 0), axis=-1) + pos[:TOP_K]
    fill_lo = jnp.concatenate([pad_start + cnt, pad_end[-1:]])
    fill_hi = jnp.concatenate([pad_end, jnp.full((1,), n_blocks * rows, I32)])
    fill = jnp.stack([fill_lo, fill_hi], axis=1).reshape(-1).astype(I32)
    blk_first = jnp.arange(n_blocks, dtype=I32)[:, None] * rows
    blk_e = jnp.minimum(jnp.sum((pad_end[None, :] <= blk_first).astype(I32), axis=1), N_EXPERTS - 1)
    n_used = (pad_end[-1:] // rows).astype(I32)
    xs = _dispatch(h_rows, dest, fill, n_blocks * rows, nsub)
    ys = _experts(blk_e, n_used, xs, p["w_gu"], p["b_gu"], p["w_d"], p["b_d"], nsub)
    gates_t = gates.T
    return _combine(dest, ys, x2, gates_t, gate2, seq, nsub)


def _prep_layer(l, norm1, norm2, w_in, w_pool, pool_scale, q_norm, k_norm, cmp_pos, cmp_w1, cmp_b1, cmp_w2,
                w_proj_a, w_proj_b, w_out, router_w, router_b, w_gate_up, b_gate_up, w_down, b_down):
    d = w_in.shape[1]
    w = w_in[l]
    o_q = POOL_WIDTH
    o_kv = o_q + N_KV * HEADS_PER_KV * HEAD_DIM
    o_g = o_kv + 6 * LANES
    n_gate = 3 * N_KV * HEADS_PER_KV
    o_m = o_g + n_gate
    wg = w[:, o_g:o_m].reshape(d, 3, N_KV, HEADS_PER_KV).transpose(0, 2, 1, 3).reshape(d, N_KV, 3 * HEADS_PER_KV)
    wg = jnp.pad(wg, ((0, 0), (0, 0), (0, LANES - 3 * HEADS_PER_KV))).reshape(d, N_KV * LANES)
    parts = [w[:, 0:o_q], w[:, o_q:o_kv], w[:, o_kv:o_kv + LANES], w[:, o_kv + LANES:o_kv + 2 * LANES],
             w[:, o_kv + 2 * LANES:o_g], wg, w[:, o_m:]]
    w_in_parts = [a.astype(BF16) for a in parts]

    half = CMP_LEN // 2
    w1 = cmp_w1[l].reshape(2, CMP_LEN, HEAD_DIM, CMP_HIDDEN)

    def chunk_weight(wh):
        z = jnp.zeros_like(wh)
        g0 = jnp.stack([wh, z], axis=2)
        g1 = jnp.stack([z, wh], axis=2)
        return jnp.concatenate([g0, g1], axis=-1).reshape(2, half * LANES, N_KV * CMP_HIDDEN)

    cmp_wa = chunk_weight(w1[:, :half]).astype(BF16)
    cmp_wb = chunk_weight(w1[:, half:]).astype(BF16)
    pos = cmp_pos[l]
    pos_t = jnp.broadcast_to(pos[:, :, None, :], (2, CMP_LEN, N_KV, HEAD_DIM))
    cmp_pos2 = jnp.stack([pos_t[:, :half].reshape(2, half * LANES), pos_t[:, half:].reshape(2, half * LANES)], axis=1)
    b1 = jnp.tile(cmp_b1[l], (1, N_KV))[:, None, :]
    w2 = cmp_w2[l]
    z2 = jnp.zeros_like(w2)
    cmp_w2b = jnp.concatenate([jnp.concatenate([w2, z2], axis=-1), jnp.concatenate([z2, w2], axis=-1)],
                              axis=1).astype(BF16)
    wr = router_w[l].T
    wr_hi = wr.astype(BF16)
    wr_lo = (wr - wr_hi.astype(F32)).astype(BF16)
    w_gu = w_gate_up[l]
    b_gu = _regroup_bias(b_gate_up[l])[:, None, :]
    return dict(
        norm1=norm1[l], norm2=norm2[l], w_in_parts=w_in_parts,
        w_pool=w_pool[l].astype(BF16), pool_scale=pool_scale[l], w_proj_a=w_proj_a[l].astype(BF16),
        k_gain2=jnp.stack([jnp.tile(k_norm[l, 1], N_KV), jnp.tile(k_norm[l, 2], N_KV)], axis=0),
        kc_gain=jnp.tile(k_norm[l, 0], N_KV)[None, :], q_gain2=jnp.tile(q_norm[l], 2)[None, :],
        cmp_pos2=cmp_pos2, cmp_wa=cmp_wa, cmp_wb=cmp_wb, cmp_b1=b1, cmp_w2=cmp_w2b,
        w_proj_b=w_proj_b[l].astype(BF16), w_out=w_out[l].astype(BF16),
        wr_hi=wr_hi, wr_lo=wr_lo, router_b=router_b[l],
        w_gu=w_gu, b_gu=b_gu, w_d=w_down[l], b_d=b_down[l][:, None, :],
    )


def kernel(x, c, norm1, norm2, ada_w, ada_b, w_in, w_pool, pool_scale, q_norm, k_norm, cmp_pos, cmp_w1, cmp_b1,
           cmp_w2, w_proj_a, w_proj_b, w_out, router_w, router_b, w_gate_up, b_gate_up, w_down, b_down):
    batch, seq, d = x.shape
    depth = norm1.shape[0]
    assert seq % KV_TILE == 0 and seq % ROUTE_TILE == 0 and seq >= WINDOW + Q_TILE and d % LANES == 0
    assert (batch * seq * TOP_K) % EXPERT_ROWS == 0 and (batch * seq) % DISPATCH_TILE == 0
    assert seq % COMBINE_TILE == 0 and seq % Q_TILE == 0 and seq % ROW_TILE == 0

    pos = np.arange(seq)
    cq, sq = _rope_tables(pos)
    cc, sc = _rope_tables(np.arange(seq // CMP_STRIDE) * CMP_STRIDE + CMP_LEN - 1)
    tile2 = lambda a: jnp.asarray(np.concatenate([a, a], axis=1))
    tri = jnp.asarray(np.triu(np.ones((ROUTE_TILE, ROUTE_TILE), np.float32), 1), BF16)
    consts = (tile2(cq), tile2(sq), tile2(cc), tile2(sc), *_attention_constants(seq), tri)

    c_pad = jnp.pad(c, ((0, SUBLANES - batch % SUBLANES if batch % SUBLANES else 0), (0, 0)))
    x2 = x.reshape(batch * seq, d)
    w_gate_up = _regroup_gate_up(w_gate_up)
    for l in range(depth):
        mod = _modulation(c_pad, ada_w[l], ada_b[l])[:batch]
        p = _prep_layer(l, norm1, norm2, w_in, w_pool, pool_scale, q_norm, k_norm, cmp_pos, cmp_w1, cmp_b1, cmp_w2,
                        w_proj_a, w_proj_b, w_out, router_w, router_b, w_gate_up, b_gate_up, w_down, b_down)
        x2 = _layer(x2, mod, consts, p, batch, seq)
    return x2.reshape(batch, seq, d)
```

```python
import functools

import numpy as np
import jax
import jax.numpy as jnp
from jax import lax
from jax.experimental import pallas as pl
from jax.experimental.pallas import tpu as pltpu

F32 = jnp.float32
BF16 = jnp.bfloat16
I32 = jnp.int32

POOL_WINDOWS = (2, 4, 8, 16)
POOL_GROUP = 128
POOL_WIDTH = 512
HEAD_DIM = 64
N_KV = 2
HEADS_PER_KV = 8
ROPE_DIM = 16
ROPE_HALF = 8
ROPE_THETA = 500000.0
CMP_LEN = 32
CMP_STRIDE = 16
CMP_HIDDEN = 256
SEL_LEN = 64
SEL_TOP = 16
WINDOW = 512
N_EXPERTS = 32
TOP_K = 4
SWIGLU_LIMIT = 7.0
SWIGLU_ALPHA = 1.702
EPS = 1e-6
NEG = -1e30
QK_SCALE = HEAD_DIM ** -0.5 * 1.4426950408889634

LANES = 128
SUBLANES = 8
VMEM_LIMIT = 56 * 1024 * 1024

ROW_TILE = 512
POOL_HALO = 16
Q_TILE = 256
KV_TILE = 512
LOOP_AHEAD = 4
ROUTE_TILE = 512
EXPERT_ROWS = 512
DISPATCH_TILE = 1024
COMBINE_TILE = 256


def _cparams(sem):
    return pltpu.CompilerParams(dimension_semantics=sem, vmem_limit_bytes=VMEM_LIMIT)


def _bdot(a, b):
    return jnp.dot(a, b, preferred_element_type=F32)


def _split(a):
    hi = a.astype(BF16)
    lo = (a - hi.astype(F32)).astype(BF16)
    return hi, lo


def _sigmoid(x):
    return 1.0 / (1.0 + jnp.exp(-x))


def _mod_kernel(c_ref, w_ref, b_ref, o_ref):
    c = c_ref[...]
    a = c * _sigmoid(c)
    a_hi, a_lo = _split(a)
    w_hi, w_lo = _split(w_ref[...])
    o_ref[...] = _bdot(a_hi, w_hi) + _bdot(a_lo, w_hi) + _bdot(a_hi, w_lo) + b_ref[...]


def _modulation(c_pad, ada_w, ada_b):
    rows, d = c_pad.shape
    n = ada_w.shape[1]
    tn = 512
    return pl.pallas_call(
        _mod_kernel,
        out_shape=jax.ShapeDtypeStruct((rows, n), F32),
        grid=(n // tn,),
        in_specs=[pl.BlockSpec((rows, d), lambda i: (0, 0)),
                  pl.BlockSpec((d, tn), lambda i: (0, i)),
                  pl.BlockSpec((1, tn), lambda i: (0, i))],
        out_specs=pl.BlockSpec((rows, tn), lambda i: (0, i)),
        compiler_params=_cparams(("arbitrary",)),
        name="modulation",
    )(c_pad, ada_w, ada_b.reshape(1, n))


def _pool_mixer(u, t_in_seq, wp_ref, ps_ref, wa_ref, ext_ref):
    ts = u.shape[0]

    @pl.when(t_in_seq == 0)
    def _():
        ext_ref[0:POOL_HALO, :] = jnp.zeros((POOL_HALO, POOL_WIDTH), F32)

    @pl.when(t_in_seq > 0)
    def _():
        ext_ref[0:POOL_HALO, :] = ext_ref[ts:ts + POOL_HALO, :]

    ext_ref[POOL_HALO:POOL_HALO + ts, :] = u
    t = t_in_seq * ts + lax.broadcasted_iota(I32, (ts, 1), 0)
    outs = []
    for gi, win in enumerate(POOL_WINDOWS):
        lo, hi = gi * POOL_GROUP, (gi + 1) * POOL_GROUP
        ug = u[:, lo:hi]
        total = ug
        for jj in range(1, win):
            total = total + ext_ref[POOL_HALO - jj:POOL_HALO - jj + ts, lo:hi]
        cnt = jnp.minimum(t + 1, win).astype(F32)
        dlt = total / cnt - ug
        outs.append(_bdot(dlt.astype(BF16), wp_ref[gi]))
    y = jnp.concatenate(outs, axis=-1) * ps_ref[...]
    return _bdot(y.astype(BF16), wa_ref[...])


def _inproj_kernel(x_ref, g_ref, sc_ref, sh_ref, qg_ref, kg_ref, cos_ref, sin_ref,
                   wu_ref, wq_ref, wkc_ref, wvc_ref, wkv_ref, wgl_ref, wmg_ref, wp_ref, ps_ref, wa_ref,
                   ya_ref, q_ref, kc_ref, vc_ref, ks_ref, vs_ref, kw_ref, vw_ref, gl_ref, mg_ref,
                   ext_ref, *, tiles_per_seq):
    x = x_ref[...]
    ms = jnp.mean(x * x, axis=-1, keepdims=True)
    h = x * lax.rsqrt(ms + EPS) * g_ref[...]
    h = h * (1.0 + sc_ref[0]) + sh_ref[0]
    hb = h.astype(BF16)
    cos, sin = cos_ref[...], sin_ref[...]

    yq = _bdot(hb, wq_ref[...])
    kv = _bdot(hb, wkv_ref[...])

    for j in range(yq.shape[1] // LANES):
        q2 = _norm_rope_pair(yq[:, j * LANES:(j + 1) * LANES], qg_ref[...], cos, sin)
        q_ref[:, j * LANES:(j + 1) * LANES] = (q2 * QK_SCALE).astype(BF16)

    mg_ref[...] = _bdot(hb, wmg_ref[...]).astype(BF16)
    u = _bdot(hb, wu_ref[...])

    ks = _norm_rope_pair(kv[:, 0:LANES], kg_ref[0:1, :], cos, sin).T.astype(BF16)
    kw = _norm_rope_pair(kv[:, 2 * LANES:3 * LANES], kg_ref[1:2, :], cos, sin).T.astype(BF16)
    for gi in range(N_KV):
        ks_ref[0, gi] = ks[gi * HEAD_DIM:(gi + 1) * HEAD_DIM, :]
        kw_ref[0, gi] = kw[gi * HEAD_DIM:(gi + 1) * HEAD_DIM, :]
        vs_ref[0, gi] = _values_with_ones(kv[:, LANES:2 * LANES], gi)
        vw_ref[0, gi] = _values_with_ones(kv[:, 3 * LANES:4 * LANES], gi)

    kc_ref[...] = _bdot(hb, wkc_ref[...])
    vc_ref[...] = _bdot(hb, wvc_ref[...])
    gl_ref[...] = _bdot(hb, wgl_ref[...])
    ya_ref[...] = _pool_mixer(u, pl.program_id(0) % tiles_per_seq, wp_ref, ps_ref, wa_ref, ext_ref).astype(BF16)


def _in_projection(x2, gain, scale, shift, q_gain2, k_gain2, cos2, sin2, weights, w_pool, pool_scale, w_proj_a,
                   batch, seq):
    n, d = x2.shape
    tm = ROW_TILE
    tps = seq // tm
    w_u, w_q, w_kc, w_vc, w_kv, w_gl, w_mg = weights
    const = lambda a: pl.BlockSpec(a.shape, lambda i: (0,) * a.ndim)
    rows = lambda width: pl.BlockSpec((tm, width), lambda i: (i, 0))
    per_b = pl.BlockSpec((1, 1, d), lambda i: (i // tps, 0, 0))
    table = pl.BlockSpec((tm, LANES), lambda i: (i % tps, 0))
    kt_shape = jax.ShapeDtypeStruct((batch, N_KV, HEAD_DIM, seq), BF16)
    v_shape = jax.ShapeDtypeStruct((batch, N_KV, seq, LANES), BF16)
    kt_spec = pl.BlockSpec((1, N_KV, HEAD_DIM, tm), lambda i: (i // tps, 0, 0, i % tps))
    v_spec = pl.BlockSpec((1, N_KV, tm, LANES), lambda i: (i // tps, 0, i % tps, 0))
    f32_out = lambda width: jax.ShapeDtypeStruct((n, width), F32)
    bf16_out = lambda width: jax.ShapeDtypeStruct((n, width), BF16)
    pool_scale2 = pool_scale.reshape(1, POOL_WIDTH)
    return pl.pallas_call(
        functools.partial(_inproj_kernel, tiles_per_seq=tps),
        out_shape=[bf16_out(w_proj_a.shape[1]), bf16_out(w_q.shape[1]),
                   f32_out(w_kc.shape[1]), f32_out(w_vc.shape[1]), kt_shape, v_shape, kt_shape, v_shape,
                   f32_out(w_gl.shape[1]), bf16_out(w_mg.shape[1])],
        grid=(n // tm,),
        in_specs=[rows(d), const(gain.reshape(1, d)), per_b, per_b, const(q_gain2), const(k_gain2), table, table,
                  const(w_u), const(w_q), const(w_kc), const(w_vc), const(w_kv), const(w_gl), const(w_mg),
                  const(w_pool), const(pool_scale2), const(w_proj_a)],
        out_specs=[rows(w_proj_a.shape[1]), rows(w_q.shape[1]), rows(w_kc.shape[1]), rows(w_vc.shape[1]),
                   kt_spec, v_spec, kt_spec, v_spec, rows(w_gl.shape[1]), rows(w_mg.shape[1])],
        scratch_shapes=[pltpu.VMEM((tm + POOL_HALO, POOL_WIDTH), F32)],
        compiler_params=_cparams(("arbitrary",)),
        name="in_projection",
    )(x2, gain.reshape(1, d), scale[:, None, :], shift[:, None, :], q_gain2, k_gain2, cos2, sin2,
      w_u, w_q, w_kc, w_vc, w_kv, w_gl, w_mg, w_pool, pool_scale2, w_proj_a)


def _rope_tables(pos):
    inv_freq = np.float32(ROPE_THETA) ** (-(np.arange(ROPE_HALF, dtype=np.float32)) / np.float32(ROPE_HALF))
    ang = pos.astype(np.float32)[:, None] * inv_freq.astype(np.float32)[None, :]
    cos, sin = np.cos(ang), np.sin(ang)
    c = np.ones((pos.shape[0], HEAD_DIM), np.float32)
    s = np.zeros((pos.shape[0], HEAD_DIM), np.float32)
    c[:, :ROPE_HALF] = cos
    c[:, ROPE_HALF:ROPE_DIM] = cos
    s[:, :ROPE_HALF] = -sin
    s[:, ROPE_HALF:ROPE_DIM] = sin
    return c, s


def _norm_rope_pair(x, gain, cos, sin):
    rows = x.shape[0]
    lane = lax.broadcasted_iota(I32, (rows, LANES), 1)
    first = lane < HEAD_DIM
    x2 = x * x
    s0 = jnp.sum(jnp.where(first, x2, 0.0), axis=-1, keepdims=True)
    s1 = jnp.sum(jnp.where(first, 0.0, x2), axis=-1, keepdims=True)
    r = lax.rsqrt(jnp.where(first, s0, s1) * (1.0 / HEAD_DIM) + EPS)
    y = x * r * gain
    low = (lane & (HEAD_DIM - 1)) < ROPE_HALF
    sw = jnp.where(low, pltpu.roll(y, LANES - ROPE_HALF, 1), pltpu.roll(y, ROPE_HALF, 1))
    return y * cos + sw * sin


def _values_with_ones(v, gi):
    lane = lax.broadcasted_iota(I32, v.shape, 1)
    vg = v if gi == 0 else pltpu.roll(v, HEAD_DIM, 1)
    return jnp.where(lane < HEAD_DIM, vg, 1.0).astype(BF16)


def _compress_kernel(k_ref, v_ref, pos_ref, wa_ref, wb_ref, b1_ref, w2_ref, gain_ref, cos_ref, sin_ref,
                     kc_ref, vc_ref):
    nch = k_ref.shape[0]

    def mlp(x, idx):
        xa = (x + pos_ref[idx, 0:1, :]).astype(BF16)
        xb = (x + pos_ref[idx, 1:2, :]).astype(BF16)
        a = _bdot(xa, wa_ref[idx])
        b = _bdot(xb, wb_ref[idx])
        b_next = jnp.concatenate([b[1:nch, :], jnp.zeros((1, b.shape[1]), F32)], axis=0)
        pre = a + b_next + b1_ref[idx]
        hid = pre * _sigmoid(pre)
        return _bdot(hid.astype(BF16), w2_ref[idx])

    kc = _norm_rope_pair(mlp(k_ref[...], 0), gain_ref[...], cos_ref[...], sin_ref[...]).T.astype(BF16)
    vc = mlp(v_ref[...], 1)
    for gi in range(N_KV):
        kc_ref[0, gi] = kc[gi * HEAD_DIM:(gi + 1) * HEAD_DIM, :]
        vc_ref[0, gi] = _values_with_ones(vc, gi)


def _compress(kcm, vcm, pos2, wa, wb, b1, w2, gain, cos_c, sin_c, batch, seq):
    nch = seq // CMP_STRIDE
    width = CMP_STRIDE * LANES
    full = lambda a: pl.BlockSpec(a.shape, lambda b: (0,) * a.ndim)
    return pl.pallas_call(
        _compress_kernel,
        out_shape=[jax.ShapeDtypeStruct((batch, N_KV, HEAD_DIM, nch), BF16),
                   jax.ShapeDtypeStruct((batch, N_KV, nch, LANES), BF16)],
        grid=(batch,),
        in_specs=[pl.BlockSpec((nch, width), lambda b: (b, 0)),
                  pl.BlockSpec((nch, width), lambda b: (b, 0)),
                  full(pos2), full(wa), full(wb), full(b1), full(w2), full(gain), full(cos_c), full(sin_c)],
        out_specs=[pl.BlockSpec((1, N_KV, HEAD_DIM, nch), lambda b: (b, 0, 0, 0)),
                   pl.BlockSpec((1, N_KV, nch, LANES), lambda b: (b, 0, 0, 0))],
        compiler_params=_cparams(("arbitrary",)),
        name="compress",
    )(kcm.reshape(batch * nch, width), vcm.reshape(batch * nch, width), pos2, wa, wb, b1, w2, gain, cos_c, sin_c)


def _attn_kernel(q_ref, gl_ref, ks_ref, vs_ref, kw_ref, vw_ref, kc_ref, vc_ref, ovl_ref, eneg_ref,
                 o_ref, qa_scr, pre_scr, m_scr, acc_scr, *, seq):
    H, QB, TK = HEADS_PER_KV, Q_TILE, KV_TILE
    q0 = pl.program_id(2) * QB
    ncp = seq // CMP_STRIDE
    nsel = seq // SEL_LEN
    n_top = min(SEL_TOP, nsel)
    nt_dims = (((1,), (1,)), ((), ()))

    def row_sum(a):
        return a[:, HEAD_DIM:HEAD_DIM + 1]

    def pair_keys(k_t, lead=None):
        z = jnp.zeros_like(k_t)
        head = [] if lead is None else [lead]
        return (jnp.concatenate(head + [k_t, z], axis=0), jnp.concatenate(head + [z, k_t], axis=0))

    q_pairs = [q_ref[:, j * LANES:(j + 1) * LANES] for j in range(H // 2)]
    qpos = q0 + lax.broadcasted_iota(I32, (QB, 1), 0)

    def one_shot_branch(k_t, v_aug, mask, between=None):
        k_var = pair_keys(k_t)
        scores = [_bdot(q_pairs[h // 2], k_var[h % 2]) for h in range(H)]
        probs, accs = [], []
        for h in range(H):
            s = jnp.where(mask, scores[h], NEG)
            scores[h] = None
            p = jnp.exp2(s - jnp.max(s, axis=-1, keepdims=True))
            probs.append(p)
            accs.append(_bdot(p.astype(BF16), v_aug))
            if between is not None:
                between(h, probs, accs)
        return accs

    gs = _sigmoid(gl_ref[...])

    def compressed_branch(width):
        valid = (lax.broadcasted_iota(I32, (QB, width), 1) * CMP_STRIDE + (CMP_LEN - 1)) <= qpos
        has_key = qpos >= CMP_LEN - 1
        imp_state = {"ps": None}

        def finish_cmp_head(h, probs, accs):
            inv = jnp.where(has_key, 1.0 / row_sum(accs[h]), 0.0)
            pre_scr[h] = accs[h] * (gs[:, h:h + 1] * inv)
            term = probs[h] * inv
            imp_state["ps"] = term if imp_state["ps"] is None else imp_state["ps"] + term
            probs[h] = None

        def after_cmp_head(h, probs, accs):
            if h > 0:
                finish_cmp_head(h - 1, probs, accs)
            if h == H - 1:
                finish_cmp_head(h, probs, accs)

        one_shot_branch(kc_ref[0, 0, :, 0:width], vc_ref[0, 0, 0:width, :], valid, between=after_cmp_head)
        ps_hi, ps_lo = _split(imp_state["ps"])
        ovl_t = ovl_ref[:, 0:width]
        return (lax.dot_general(ovl_t, ps_hi, nt_dims, preferred_element_type=F32)
                + lax.dot_general(ovl_t, ps_lo, nt_dims, preferred_element_type=F32))

    imp = compressed_branch(ncp)

    sidx = lax.broadcasted_iota(I32, (nsel, QB), 0)
    cur = (q0 + lax.broadcasted_iota(I32, (nsel, QB), 1)) // SEL_LEN
    causal = sidx <= cur
    forced = (sidx == cur) | (sidx == 0)
    n_forced = 2
    pick_state = {"v": jnp.where(forced, -3e38, jnp.where(causal, imp, -1.0)), "sel": forced}

    def pick_next():
        v = pick_state["v"]
        mx = jnp.max(v, axis=0, keepdims=True)
        first = jnp.min(jnp.where(v == mx, sidx, nsel), axis=0, keepdims=True)
        pick = sidx == first
        pick_state["sel"] = pick_state["sel"] | pick
        pick_state["v"] = jnp.where(pick, -3e38, v)

    def picks_after_head(h, probs, accs):
        probs[h] = None
        n_pick = n_top - n_forced
        for _ in range(n_pick // H + (1 if h < n_pick % H else 0)):
            pick_next()

    span = WINDOW + QB
    base = pl.multiple_of(jnp.maximum(q0 - WINDOW, 0), LANES)
    dist = qpos - (base + lax.broadcasted_iota(I32, (QB, span), 1))
    inwin = (dist >= 0) & (dist < WINDOW)
    accs_w = one_shot_branch(kw_ref[0, 0, :, pl.ds(base, span)], vw_ref[0, 0, pl.ds(base, span), :],
                             inwin, between=picks_after_head)
    for h in range(H):
        pre_scr[h] = pre_scr[h] + accs_w[h] * (gs[:, 2 * H + h:2 * H + h + 1] / row_sum(accs_w[h]))

    notsel = jnp.where(pick_state["sel"] & causal, 0.0, 1.0).T.astype(BF16)
    for j in range(H // 2):
        qa_scr[j] = jnp.concatenate([notsel, q_pairs[j]], axis=1)

    m_scr[...] = jnp.full(m_scr.shape, NEG, F32)
    acc_scr[...] = jnp.zeros(acc_scr.shape, F32)

    def sel_tile(kt, causal_tile):
        off = pl.multiple_of(kt * TK, TK)
        k_var = pair_keys(ks_ref[0, 0, :, pl.ds(off, TK)], lead=eneg_ref[:, pl.ds(off, TK)])
        score = lambda h: _bdot(qa_scr[h // 2], k_var[h % 2])
        v_t = vs_ref[0, 0, pl.ds(off, TK), :]
        if causal_tile:
            keep = (off + lax.broadcasted_iota(I32, (QB, TK), 1)) <= qpos
        ahead = H if causal_tile else LOOP_AHEAD
        scores = [score(h) for h in range(ahead)]
        for h in range(H):
            s = scores[h]
            scores[h] = None
            if h + ahead < H:
                scores.append(score(h + ahead))
            if causal_tile:
                s = jnp.where(keep, s, NEG)
            m_old = m_scr[h]
            m_new = jnp.maximum(m_old, jnp.max(s, axis=-1, keepdims=True))
            alpha = jnp.exp2(m_old - m_new)
            pt = jnp.concatenate([jnp.exp2(s[:, c * LANES:(c + 1) * LANES] - m_new).astype(BF16)
                                  for c in range(TK // LANES)], axis=1)
            acc_scr[h] = alpha * acc_scr[h] + _bdot(pt, v_t)
            m_scr[h] = m_new

    n_tiles = (q0 + QB - 1) // TK + 1

    def body(kt, carry):
        sel_tile(kt, False)
        return carry

    lax.fori_loop(0, n_tiles - 1, body, 0)
    sel_tile(n_tiles - 1, True)

    def head_out(h):
        a_s = acc_scr[h]
        return pre_scr[h] + a_s * (gs[:, H + h:H + h + 1] / row_sum(a_s))

    low = lax.broadcasted_iota(I32, (QB, LANES), 1) < HEAD_DIM
    for j in range(H // 2):
        slab = jnp.where(low, head_out(2 * j), pltpu.roll(head_out(2 * j + 1), HEAD_DIM, 1))
        o_ref[:, j * LANES:(j + 1) * LANES] = slab.astype(BF16)


def _attention(q, gl, ks, vs, kw, vw, kc, vc, ovl, eneg, batch, seq):
    n = q.shape[0]
    QB = Q_TILE
    nq = seq // QB
    H = HEADS_PER_KV
    gw = HEADS_PER_KV * HEAD_DIM
    ncp = seq // CMP_STRIDE
    nsel = seq // SEL_LEN
    const2 = lambda a: pl.BlockSpec(a.shape, lambda b, g, j: (0,) * a.ndim)
    return pl.pallas_call(
        functools.partial(_attn_kernel, seq=seq),
        out_shape=jax.ShapeDtypeStruct((n, N_KV * gw), BF16),
        grid=(batch, N_KV, nq),
        in_specs=[pl.BlockSpec((QB, gw), lambda b, g, j: (b * nq + j, g)),
                  pl.BlockSpec((QB, LANES), lambda b, g, j: (b * nq + j, g)),
                  pl.BlockSpec((1, 1, HEAD_DIM, seq), lambda b, g, j: (b, g, 0, 0)),
                  pl.BlockSpec((1, 1, seq, LANES), lambda b, g, j: (b, g, 0, 0)),
                  pl.BlockSpec((1, 1, HEAD_DIM, seq), lambda b, g, j: (b, g, 0, 0)),
                  pl.BlockSpec((1, 1, seq, LANES), lambda b, g, j: (b, g, 0, 0)),
                  pl.BlockSpec((1, 1, HEAD_DIM, ncp), lambda b, g, j: (b, g, 0, 0)),
                  pl.BlockSpec((1, 1, ncp, LANES), lambda b, g, j: (b, g, 0, 0)),
                  const2(ovl), const2(eneg)],
        out_specs=pl.BlockSpec((QB, gw), lambda b, g, j: (b * nq + j, g)),
        scratch_shapes=[pltpu.VMEM((H // 2, QB, nsel + LANES), BF16), pltpu.VMEM((H, QB, LANES), F32),
                        pltpu.VMEM((H, QB, LANES), F32), pltpu.VMEM((H, QB, LANES), F32)],
        compiler_params=_cparams(("arbitrary", "arbitrary", "arbitrary")),
        name="sparse_attention",
    )(q, gl, ks, vs, kw, vw, kc, vc, ovl, eneg)


def _merge_kernel(x_ref, att_ref, ya_ref, mg_ref, g1_ref, wb_ref, wo_ref, o_ref):
    d = x_ref.shape[1]
    y_b = _bdot(att_ref[...], wb_ref[...])
    mg = mg_ref[...].astype(F32)
    merged = _sigmoid(mg[:, 0:d]) * ya_ref[...].astype(F32) + _sigmoid(mg[:, d:2 * d]) * y_b
    o_ref[...] = x_ref[...] + g1_ref[0] * _bdot(merged.astype(BF16), wo_ref[...])


def _merge(x2, att, y_a, mg, gate1, w_proj_b, w_out, seq):
    n, d = x2.shape
    tm = ROW_TILE
    tiles_per_b = seq // tm
    return pl.pallas_call(
        _merge_kernel,
        out_shape=jax.ShapeDtypeStruct((n, d), F32),
        grid=(n // tm,),
        in_specs=[pl.BlockSpec((tm, d), lambda i: (i, 0)),
                  pl.BlockSpec((tm, d), lambda i: (i, 0)),
                  pl.BlockSpec((tm, d), lambda i: (i, 0)),
                  pl.BlockSpec((tm, 2 * d), lambda i: (i, 0)),
                  pl.BlockSpec((1, 1, d), lambda i: (i // tiles_per_b, 0, 0)),
                  pl.BlockSpec(w_proj_b.shape, lambda i: (0, 0)),
                  pl.BlockSpec(w_out.shape, lambda i: (0, 0))],
        out_specs=pl.BlockSpec((tm, d), lambda i: (i, 0)),
        compiler_params=_cparams(("arbitrary",)),
        name="merge_out_projection",
    )(x2, att, y_a, mg, gate1[:, None, :], w_proj_b, w_out)


def _router_kernel(x_ref, g_ref, sc_ref, sh_ref, wh_ref, wl_ref, rb_ref, tri_ref,
                   h_ref, e_ref, gt_ref, pos_ref, cnt_ref, carry_ref):
    i = pl.program_id(0)
    tm, d = x_ref.shape

    @pl.when(i == 0)
    def _():
        carry_ref[...] = jnp.zeros((N_EXPERTS, 1), F32)

    x = x_ref[...]
    ms = jnp.mean(x * x, axis=-1, keepdims=True)
    h = x * lax.rsqrt(ms + EPS) * g_ref[...]
    h = h * (1.0 + sc_ref[0]) + sh_ref[0]
    for s in range(d // LANES):
        h_ref[pl.ds(s, tm, stride=SUBLANES), :] = h[:, s * LANES:(s + 1) * LANES]

    h_hi, h_lo = _split(h)
    nt = (((1,), (1,)), ((), ()))
    logits = (lax.dot_general(wh_ref[...], h_hi, nt, preferred_element_type=F32)
              + lax.dot_general(wh_ref[...], h_lo, nt, preferred_element_type=F32)
              + lax.dot_general(wl_ref[...], h_hi, nt, preferred_element_type=F32)
              + rb_ref[...])
    eidx = lax.broadcasted_iota(I32, (N_EXPERTS, tm), 0)
    onehot = jnp.zeros((N_EXPERTS, tm), F32)
    picks, vals, ids = [], [], []
    for _ in range(TOP_K):
        mx = jnp.max(logits, axis=0, keepdims=True)
        first = jnp.min(jnp.where(logits == mx, eidx, N_EXPERTS), axis=0, keepdims=True)
        pick = eidx == first
        picks.append(pick)
        vals.append(mx)
        ids.append(first)
        onehot = jnp.where(pick, 1.0, onehot)
        logits = jnp.where(pick, -3e38, logits)
    ex = [jnp.exp(vk - vals[0]) for vk in vals]
    den = ex[0] + ex[1] + ex[2] + ex[3]
    before = _bdot(onehot.astype(BF16), tri_ref[...]) + carry_ref[...]
    pad = SUBLANES - TOP_K
    pos = [jnp.sum(jnp.where(pk, before, 0.0), axis=0, keepdims=True) for pk in picks]
    e_ref[...] = jnp.concatenate(ids + [jnp.zeros((pad, tm), I32)], axis=0)
    gt_ref[...] = jnp.concatenate([e / den for e in ex] + [jnp.zeros((pad, tm), F32)], axis=0)
    pos_ref[...] = jnp.concatenate(pos + [jnp.zeros((pad, tm), F32)], axis=0).astype(I32)
    carry_ref[...] = carry_ref[...] + jnp.sum(onehot, axis=1, keepdims=True)
    cnt_ref[...] = carry_ref[...]


def _router(x2, gain, scale, shift, wr_hi, wr_lo, rb, tri, seq):
    n, d = x2.shape
    tm = ROUTE_TILE
    tiles_per_b = seq // tm
    nsub = d // LANES
    return pl.pallas_call(
        _router_kernel,
        out_shape=[jax.ShapeDtypeStruct((n * nsub, LANES), F32),
                   jax.ShapeDtypeStruct((SUBLANES, n), I32),
                   jax.ShapeDtypeStruct((SUBLANES, n), F32),
                   jax.ShapeDtypeStruct((SUBLANES, n), I32),
                   jax.ShapeDtypeStruct((N_EXPERTS, 1), F32)],
        grid=(n // tm,),
        in_specs=[pl.BlockSpec((tm, d), lambda i: (i, 0)),
                  pl.BlockSpec((1, d), lambda i: (0, 0)),
                  pl.BlockSpec((1, 1, d), lambda i: (i // tiles_per_b, 0, 0)),
                  pl.BlockSpec((1, 1, d), lambda i: (i // tiles_per_b, 0, 0)),
                  pl.BlockSpec(wr_hi.shape, lambda i: (0, 0)),
                  pl.BlockSpec(wr_lo.shape, lambda i: (0, 0)),
                  pl.BlockSpec((N_EXPERTS, 1), lambda i: (0, 0)),
                  pl.BlockSpec((tm, tm), lambda i: (0, 0))],
        out_specs=[pl.BlockSpec((tm * nsub, LANES), lambda i: (i, 0)),
                   pl.BlockSpec((SUBLANES, tm), lambda i: (0, i)),
                   pl.BlockSpec((SUBLANES, tm), lambda i: (0, i)),
                   pl.BlockSpec((SUBLANES, tm), lambda i: (0, i)),
                   pl.BlockSpec((N_EXPERTS, 1), lambda i: (0, 0))],
        scratch_shapes=[pltpu.VMEM((N_EXPERTS, 1), F32)],
        compiler_params=_cparams(("arbitrary",)),
        name="router",
    )(x2, gain.reshape(1, d), scale[:, None, :], shift[:, None, :], wr_hi, wr_lo, rb.reshape(N_EXPERTS, 1), tri)


def _dispatch_kernel(dest_ref, fill_ref, h_ref, xs_ref, zero_ref, sem, *, nsub):
    i = pl.program_id(0)
    tt = DISPATCH_TILE

    def row_copy(n, k):
        dst = dest_ref[0, 0, k * tt + n]
        return pltpu.make_async_copy(h_ref.at[pl.ds(n * nsub, nsub), :],
                                     xs_ref.at[pl.ds(dst * nsub, nsub), :], sem)

    def fill_copy(r):
        return pltpu.make_async_copy(zero_ref, xs_ref.at[pl.ds(r * nsub, nsub), :], sem)

    def for_fill_rows(fn):
        def per_range(e, carry):
            lax.fori_loop(fill_ref[2 * e], fill_ref[2 * e + 1], lambda r, c: (fn(fill_copy(r)), c)[1], 0)
            return carry

        lax.fori_loop(0, fill_ref.shape[0] // 2, per_range, 0)

    def for_token_rows(fn):
        def per_token(n, carry):
            for k in range(TOP_K):
                fn(row_copy(n, k))
            return carry

        lax.fori_loop(0, tt, per_token, 0)

    @pl.when(i == 0)
    def _():
        zero_ref[...] = jnp.zeros(zero_ref.shape, F32)
        for_fill_rows(lambda cp: cp.start())

    for_token_rows(lambda cp: cp.start())
    for_token_rows(lambda cp: cp.wait())

    @pl.when(i == 0)
    def _():
        for_fill_rows(lambda cp: cp.wait())


def _tile_indices(dest, tt):
    k, n = dest.shape
    return dest.reshape(k, n // tt, tt).transpose(1, 0, 2).reshape(n // tt, 1, k * tt)


def _dispatch(h_rows, dest, fill, n_rows, nsub):
    n_tok = h_rows.shape[0] // nsub
    tt = DISPATCH_TILE
    dest_tiles = _tile_indices(dest, tt)
    grid_spec = pltpu.PrefetchScalarGridSpec(
        num_scalar_prefetch=0,
        grid=(n_tok // tt,),
        in_specs=[pl.BlockSpec((1, 1, TOP_K * tt), lambda i: (i, 0, 0), memory_space=pltpu.SMEM),
                  pl.BlockSpec(memory_space=pltpu.SMEM),
                  pl.BlockSpec((tt * nsub, LANES), lambda i: (i, 0))],
        out_specs=pl.BlockSpec(memory_space=pl.ANY),
        scratch_shapes=[pltpu.VMEM((nsub, LANES), F32), pltpu.SemaphoreType.DMA(())],
    )
    return pl.pallas_call(
        functools.partial(_dispatch_kernel, nsub=nsub),
        out_shape=jax.ShapeDtypeStruct((n_rows * nsub, LANES), F32),
        grid_spec=grid_spec,
        compiler_params=pltpu.CompilerParams(dimension_semantics=("arbitrary",), vmem_limit_bytes=VMEM_LIMIT,
                                             has_side_effects=True),
        name="dispatch",
    )(dest_tiles, fill, h_rows)


PAIR = 2 * LANES


def _pair_permutation():
    p = np.zeros((PAIR, PAIR), np.float32)
    p[2 * np.arange(LANES), np.arange(LANES)] = 1.0
    p[2 * np.arange(LANES) + 1, LANES + np.arange(LANES)] = 1.0
    return p


def _regroup_kernel(w_ref, p_ref, o_ref):
    w = w_ref[0].astype(BF16)
    for j in range(w.shape[1] // PAIR):
        o_ref[0, :, j * PAIR:(j + 1) * PAIR] = _bdot(w[:, j * PAIR:(j + 1) * PAIR], p_ref[...]).astype(BF16)


def _regroup_gate_up(w_gate_up):
    depth, n_e, d, f2 = w_gate_up.shape
    tm = 512
    perm = jnp.asarray(_pair_permutation(), BF16)
    out = pl.pallas_call(
        _regroup_kernel,
        out_shape=jax.ShapeDtypeStruct((depth * n_e, d, f2), BF16),
        grid=(depth * n_e, d // tm),
        in_specs=[pl.BlockSpec((1, tm, f2), lambda e, i: (e, i, 0)),
                  pl.BlockSpec((PAIR, PAIR), lambda e, i: (0, 0))],
        out_specs=pl.BlockSpec((1, tm, f2), lambda e, i: (e, i, 0)),
        compiler_params=_cparams(("arbitrary", "arbitrary")),
        name="regroup_gate_up",
    )(w_gate_up.reshape(depth * n_e, d, f2), perm)
    return out.reshape(depth, n_e, d, f2)


def _regroup_bias(b):
    lead = b.shape[:-1]
    return b.reshape(*lead, -1, LANES, 2).swapaxes(-1, -2).reshape(*lead, -1)


def _expert_kernel(be_ref, nu_ref, xs_ref, wgu_ref, bgu_ref, wd_ref, bd_ref, ys_ref, *, nsub):
    i = pl.program_id(0)
    rows = EXPERT_ROWS
    f = wd_ref.shape[1]

    @pl.when(i < nu_ref[0])
    def _():
        x = jnp.concatenate([xs_ref[pl.ds(s, rows, stride=nsub), :] for s in range(nsub)], axis=-1)
        gu = _bdot(x.astype(BF16), wgu_ref[0]) + bgu_ref[0]
        acts = []
        for j in range(f // LANES):
            g_lin = jnp.minimum(gu[:, j * PAIR:j * PAIR + LANES], SWIGLU_LIMIT)
            u_lin = jnp.clip(gu[:, j * PAIR + LANES:(j + 1) * PAIR], -SWIGLU_LIMIT, SWIGLU_LIMIT)
            acts.append(((u_lin + 1.0) * g_lin * _sigmoid(g_lin * SWIGLU_ALPHA)).astype(BF16))
        act = jnp.concatenate(acts, axis=-1)
        y = _bdot(act, wd_ref[0].astype(BF16)) + bd_ref[0]
        for s in range(nsub):
            ys_ref[pl.ds(s, rows, stride=nsub), :] = y[:, s * LANES:(s + 1) * LANES]

    @pl.when(i >= nu_ref[0])
    def _():
        ys_ref[...] = jnp.zeros(ys_ref.shape, F32)


def _experts(blk_e, n_used, xs, w_gu, b_gu, w_d, b_d, nsub):
    rows = EXPERT_ROWS
    n_blocks = xs.shape[0] // (rows * nsub)
    d, f2 = w_gu.shape[1], w_gu.shape[2]
    f = w_d.shape[1]
    row_map = lambda i, be, nu: (jnp.minimum(i, nu[0] - 1), 0)
    grid_spec = pltpu.PrefetchScalarGridSpec(
        num_scalar_prefetch=2,
        grid=(n_blocks,),
        in_specs=[pl.BlockSpec((rows * nsub, LANES), row_map),
                  pl.BlockSpec((1, d, f2), lambda i, be, nu: (be[i], 0, 0)),
                  pl.BlockSpec((1, 1, f2), lambda i, be, nu: (be[i], 0, 0)),
                  pl.BlockSpec((1, f, d), lambda i, be, nu: (be[i], 0, 0)),
                  pl.BlockSpec((1, 1, d), lambda i, be, nu: (be[i], 0, 0))],
        out_specs=pl.BlockSpec((rows * nsub, LANES), lambda i, be, nu: (i, 0)),
    )
    return pl.pallas_call(
        functools.partial(_expert_kernel, nsub=nsub),
        out_shape=jax.ShapeDtypeStruct(xs.shape, F32),
        grid_spec=grid_spec,
        compiler_params=_cparams(("arbitrary",)),
        name="experts",
    )(blk_e, n_used, xs, w_gu, b_gu, w_d, b_d)


def _combine_kernel(cur_ref, nxt_ref, ys_ref, x_ref, gt_ref, g2_ref, o_ref, buf_ref, sems, *, nsub, n_steps):
    i = pl.program_id(0)
    tt = COMBINE_TILE
    slot = i % 2

    def for_tile_rows(idx_ref, s, fn):
        def per_token(n, carry):
            for k in range(TOP_K):
                src = idx_ref[0, 0, k * tt + n]
                fn(pltpu.make_async_copy(ys_ref.at[pl.ds(src * nsub, nsub), :],
                                         buf_ref.at[s, k, pl.ds(n * nsub, nsub), :], sems.at[s]))
            return carry

        lax.fori_loop(0, tt, per_token, 0)

    @pl.when(i == 0)
    def _():
        for_tile_rows(cur_ref, 0, lambda cp: cp.start())

    @pl.when(i + 1 < n_steps)
    def _():
        for_tile_rows(nxt_ref, 1 - slot, lambda cp: cp.start())

    for_tile_rows(cur_ref, slot, lambda cp: cp.wait())
    gt = gt_ref[...]
    acc = None
    for k in range(TOP_K):
        yk = jnp.concatenate([buf_ref[slot, k, pl.ds(s, tt, stride=nsub), :] for s in range(nsub)], axis=-1)
        term = gt[:, k:k + 1] * yk
        acc = term if acc is None else acc + term
    o_ref[...] = x_ref[...] + g2_ref[0] * acc


def _combine(dest, ys, x2, gates_t, gate2, seq, nsub):
    n, d = x2.shape
    tt = COMBINE_TILE
    tiles_per_b = seq // tt
    n_steps = n // tt
    dest_tiles = _tile_indices(dest, tt)
    grid_spec = pltpu.PrefetchScalarGridSpec(
        num_scalar_prefetch=0,
        grid=(n_steps,),
        in_specs=[pl.BlockSpec((1, 1, TOP_K * tt), lambda i: (i, 0, 0), memory_space=pltpu.SMEM),
                  pl.BlockSpec((1, 1, TOP_K * tt), lambda i: (jnp.minimum(i + 1, n_steps - 1), 0, 0),
                               memory_space=pltpu.SMEM),
                  pl.BlockSpec(memory_space=pl.ANY),
                  pl.BlockSpec((tt, d), lambda i: (i, 0)),
                  pl.BlockSpec((tt, SUBLANES), lambda i: (i, 0)),
                  pl.BlockSpec((1, 1, d), lambda i: (i // tiles_per_b, 0, 0))],
        out_specs=pl.BlockSpec((tt, d), lambda i: (i, 0)),
        scratch_shapes=[pltpu.VMEM((2, TOP_K, tt * nsub, LANES), F32), pltpu.SemaphoreType.DMA((2,))],
    )
    return pl.pallas_call(
        functools.partial(_combine_kernel, nsub=nsub, n_steps=n_steps),
        out_shape=jax.ShapeDtypeStruct((n, d), F32),
        grid_spec=grid_spec,
        compiler_params=_cparams(("arbitrary",)),
        name="combine",
    )(dest_tiles, dest_tiles, ys, x2, gates_t, gate2[:, None, :])


def _attention_constants(seq):
    nsel = seq // SEL_LEN
    ncp = seq // CMP_STRIDE
    cmp_start = np.arange(ncp) * CMP_STRIDE
    sel_start = np.arange(nsel) * SEL_LEN
    ovl = ((cmp_start[:, None] < sel_start[None, :] + SEL_LEN)
           & (cmp_start[:, None] + CMP_LEN - 1 >= sel_start[None, :])).astype(np.float32)
    ovl[ncp - 1, :] = 0.0
    eneg = np.where(np.arange(seq)[None, :] // SEL_LEN == np.arange(nsel)[:, None], NEG, 0.0).astype(np.float32)
    return jnp.asarray(ovl.T, BF16), jnp.asarray(eneg, BF16)


def _layer(x2, mod, consts, p, batch, seq):
    n, d = x2.shape
    shift1, scale1, gate1, shift2, scale2, gate2 = [mod[:, k * d:(k + 1) * d] for k in range(6)]
    (cos2, sin2, cos_c, sin_c, ovl, eneg, tri) = consts

    y_a, q, kcm, vcm, ks, vs, kw, vw, gl, mg = _in_projection(
        x2, p["norm1"], scale1, shift1, p["q_gain2"], p["k_gain2"], cos2, sin2, p["w_in_parts"],
        p["w_pool"], p["pool_scale"], p["w_proj_a"], batch, seq)
    kc, vc = _compress(kcm, vcm, p["cmp_pos2"], p["cmp_wa"], p["cmp_wb"], p["cmp_b1"], p["cmp_w2"],
                       p["kc_gain"], cos_c, sin_c, batch, seq)
    att = _attention(q, gl, ks, vs, kw, vw, kc, vc, ovl, eneg, batch, seq)
    x2 = _merge(x2, att, y_a, mg, gate1, p["w_proj_b"], p["w_out"], seq)

    nsub = d // LANES
    h_rows, top_e, gates, pos, counts = _router(x2, p["norm2"], scale2, shift2, p["wr_hi"], p["wr_lo"],
                                                p["router_b"], tri, seq)
    rows = EXPERT_ROWS
    n_blocks = -(-n * TOP_K // rows) + N_EXPERTS
    cnt = counts[:, 0].astype(I32)
    padded = (cnt + rows - 1) // rows * rows
    pad_end = jnp.cumsum(padded)
    pad_start = pad_end - padded
    e_ids = jnp.arange(N_EXPERTS, dtype=I32)
    dest = jnp.sum(jnp.where(top_e[:TOP_K, :, None] == e_ids, pad_start, 0), axis=-1) + pos[:TOP_K]
    fill_lo = jnp.concatenate([pad_start + cnt, pad_end[-1:]])
    fill_hi = jnp.concatenate([pad_end, jnp.full((1,), n_blocks * rows, I32)])
    fill = jnp.stack([fill_lo, fill_hi], axis=1).reshape(-1).astype(I32)
    blk_first = jnp.arange(n_blocks, dtype=I32)[:, None] * rows
    blk_e = jnp.minimum(jnp.sum((pad_end[None, :] <= blk_first).astype(I32), axis=1), N_EXPERTS - 1)
    n_used = (pad_end[-1:] // rows).astype(I32)
    xs = _dispatch(h_rows, dest, fill, n_blocks * rows, nsub)
    ys = _experts(blk_e, n_used, xs, p["w_gu"], p["b_gu"], p["w_d"], p["b_d"], nsub)
    gates_t = gates.T
    return _combine(dest, ys, x2, gates_t, gate2, seq, nsub)


def _prep_layer(l, norm1, norm2, w_in, w_pool, pool_scale, q_norm, k_norm, cmp_pos, cmp_w1, cmp_b1, cmp_w2,
                w_proj_a, w_proj_b, w_out, router_w, router_b, w_gate_up, b_gate_up, w_down, b_down):
    d = w_in.shape[1]
    w = w_in[l]
    o_q = POOL_WIDTH
    o_kv = o_q + N_KV * HEADS_PER_KV * HEAD_DIM
    o_g = o_kv + 6 * LANES
    n_gate = 3 * N_KV * HEADS_PER_KV
    o_m = o_g + n_gate
    wg = w[:, o_g:o_m].reshape(d, 3, N_KV, HEADS_PER_KV).transpose(0, 2, 1, 3).reshape(d, N_KV, 3 * HEADS_PER_KV)
    wg = jnp.pad(wg, ((0, 0), (0, 0), (0, LANES - 3 * HEADS_PER_KV))).reshape(d, N_KV * LANES)
    parts = [w[:, 0:o_q], w[:, o_q:o_kv], w[:, o_kv:o_kv + LANES], w[:, o_kv + LANES:o_kv + 2 * LANES],
             w[:, o_kv + 2 * LANES:o_g], wg, w[:, o_m:]]
    w_in_parts = [a.astype(BF16) for a in parts]

    half = CMP_LEN // 2
    w1 = cmp_w1[l].reshape(2, CMP_LEN, HEAD_DIM, CMP_HIDDEN)

    def chunk_weight(wh):
        z = jnp.zeros_like(wh)
        g0 = jnp.stack([wh, z], axis=2)
        g1 = jnp.stack([z, wh], axis=2)
        return jnp.concatenate([g0, g1], axis=-1).reshape(2, half * LANES, N_KV * CMP_HIDDEN)

    cmp_wa = chunk_weight(w1[:, :half]).astype(BF16)
    cmp_wb = chunk_weight(w1[:, half:]).astype(BF16)
    pos = cmp_pos[l]
    pos_t = jnp.broadcast_to(pos[:, :, None, :], (2, CMP_LEN, N_KV, HEAD_DIM))
    cmp_pos2 = jnp.stack([pos_t[:, :half].reshape(2, half * LANES), pos_t[:, half:].reshape(2, half * LANES)], axis=1)
    b1 = jnp.tile(cmp_b1[l], (1, N_KV))[:, None, :]
    w2 = cmp_w2[l]
    z2 = jnp.zeros_like(w2)
    cmp_w2b = jnp.concatenate([jnp.concatenate([w2, z2], axis=-1), jnp.concatenate([z2, w2], axis=-1)],
                              axis=1).astype(BF16)
    wr = router_w[l].T
    wr_hi = wr.astype(BF16)
    wr_lo = (wr - wr_hi.astype(F32)).astype(BF16)
    w_gu = w_gate_up[l]
    b_gu = _regroup_bias(b_gate_up[l])[:, None, :]
    return dict(
        norm1=norm1[l], norm2=norm2[l], w_in_parts=w_in_parts,
        w_pool=w_pool[l].astype(BF16), pool_scale=pool_scale[l], w_proj_a=w_proj_a[l].astype(BF16),
        k_gain2=jnp.stack([jnp.tile(k_norm[l, 1], N_KV), jnp.tile(k_norm[l, 2], N_KV)], axis=0),
        kc_gain=jnp.tile(k_norm[l, 0], N_KV)[None, :], q_gain2=jnp.tile(q_norm[l], 2)[None, :],
        cmp_pos2=cmp_pos2, cmp_wa=cmp_wa, cmp_wb=cmp_wb, cmp_b1=b1, cmp_w2=cmp_w2b,
        w_proj_b=w_proj_b[l].astype(BF16), w_out=w_out[l].astype(BF16),
        wr_hi=wr_hi, wr_lo=wr_lo, router_b=router_b[l],
        w_gu=w_gu, b_gu=b_gu, w_d=w_down[l], b_d=b_down[l][:, None, :],
    )


def kernel(x, c, norm1, norm2, ada_w, ada_b, w_in, w_pool, pool_scale, q_norm, k_norm, cmp_pos, cmp_w1, cmp_b1,
           cmp_w2, w_proj_a, w_proj_b, w_out, router_w, router_b, w_gate_up, b_gate_up, w_down, b_down):
    batch, seq, d = x.shape
    depth = norm1.shape[0]
    assert seq % KV_TILE == 0 and seq % ROUTE_TILE == 0 and seq >= WINDOW + Q_TILE and d % LANES == 0
    assert (batch * seq * TOP_K) % EXPERT_ROWS == 0 and (batch * seq) % DISPATCH_TILE == 0
    assert seq % COMBINE_TILE == 0 and seq % Q_TILE == 0 and seq % ROW_TILE == 0

    pos = np.arange(seq)
    cq, sq = _rope_tables(pos)
    cc, sc = _rope_tables(np.arange(seq // CMP_STRIDE) * CMP_STRIDE + CMP_LEN - 1)
    tile2 = lambda a: jnp.asarray(np.concatenate([a, a], axis=1))
    tri = jnp.asarray(np.triu(np.ones((ROUTE_TILE, ROUTE_TILE), np.float32), 1), BF16)
    consts = (tile2(cq), tile2(sq), tile2(cc), tile2(sc), *_attention_constants(seq), tri)

    c_pad = jnp.pad(c, ((0, SUBLANES - batch % SUBLANES if batch % SUBLANES else 0), (0, 0)))
    x2 = x.reshape(batch * seq, d)
    w_gate_up = _regroup_gate_up(w_gate_up)
    for l in range(depth):
        mod = _modulation(c_pad, ada_w[l], ada_b[l])[:batch]
        p = _prep_layer(l, norm1, norm2, w_in, w_pool, pool_scale, q_norm, k_norm, cmp_pos, cmp_w1, cmp_b1, cmp_w2,
                        w_proj_a, w_proj_b, w_out, router_w, router_b, w_gate_up, b_gate_up, w_down, b_down)
        x2 = _layer(x2, mod, consts, p, batch, seq)
    return x2.reshape(batch, seq, d)
```

```python
import functools

import numpy as np
import jax
import jax.numpy as jnp
from jax import lax
from jax.experimental import pallas as pl
from jax.experimental.pallas import tpu as pltpu

F32 = jnp.float32
BF16 = jnp.bfloat16
I32 = jnp.int32

POOL_WINDOWS = (2, 4, 8, 16)
POOL_GROUP = 128
POOL_WIDTH = 512
HEAD_DIM = 64
N_KV = 2
HEADS_PER_KV = 8
ROPE_DIM = 16
ROPE_HALF = 8
ROPE_THETA = 500000.0
CMP_LEN = 32
CMP_STRIDE = 16
CMP_HIDDEN = 256
SEL_LEN = 64
SEL_TOP = 16
WINDOW = 512
N_EXPERTS = 32
TOP_K = 4
SWIGLU_LIMIT = 7.0
SWIGLU_ALPHA = 1.702
EPS = 1e-6
NEG = -1e30
QK_SCALE = HEAD_DIM ** -0.5 * 1.4426950408889634

LANES = 128
SUBLANES = 8
VMEM_LIMIT = 56 * 1024 * 1024

ROW_TILE = 512
POOL_HALO = 16
Q_TILE = 512
KV_TILE = 512
LOOP_AHEAD = 4
ROUTE_TILE = 512
EXPERT_ROWS = 512
DISPATCH_TILE = 1024
COMBINE_TILE = 256


def _cparams(sem):
    return pltpu.CompilerParams(dimension_semantics=sem, vmem_limit_bytes=VMEM_LIMIT)


def _bdot(a, b):
    return jnp.dot(a, b, preferred_element_type=F32)


def _split(a):
    hi = a.astype(BF16)
    lo = (a - hi.astype(F32)).astype(BF16)
    return hi, lo


def _sigmoid(x):
    return 1.0 / (1.0 + jnp.exp(-x))


def _mod_kernel(c_ref, w_ref, b_ref, o_ref):
    c = c_ref[...]
    a = c * _sigmoid(c)
    a_hi, a_lo = _split(a)
    w_hi, w_lo = _split(w_ref[...])
    o_ref[...] = _bdot(a_hi, w_hi) + _bdot(a_lo, w_hi) + _bdot(a_hi, w_lo) + b_ref[...]


def _modulation(c_pad, ada_w, ada_b):
    rows, d = c_pad.shape
    n = ada_w.shape[1]
    tn = 512
    return pl.pallas_call(
        _mod_kernel,
        out_shape=jax.ShapeDtypeStruct((rows, n), F32),
        grid=(n // tn,),
        in_specs=[pl.BlockSpec((rows, d), lambda i: (0, 0)),
                  pl.BlockSpec((d, tn), lambda i: (0, i)),
                  pl.BlockSpec((1, tn), lambda i: (0, i))],
        out_specs=pl.BlockSpec((rows, tn), lambda i: (0, i)),
        compiler_params=_cparams(("arbitrary",)),
        name="modulation",
    )(c_pad, ada_w, ada_b.reshape(1, n))


def _pool_mixer(u, t_in_seq, wp_ref, ps_ref, wa_ref, ext_ref):
    ts = u.shape[0]

    @pl.when(t_in_seq == 0)
    def _():
        ext_ref[0:POOL_HALO, :] = jnp.zeros((POOL_HALO, POOL_WIDTH), F32)

    @pl.when(t_in_seq > 0)
    def _():
        ext_ref[0:POOL_HALO, :] = ext_ref[ts:ts + POOL_HALO, :]

    ext_ref[POOL_HALO:POOL_HALO + ts, :] = u
    t = t_in_seq * ts + lax.broadcasted_iota(I32, (ts, 1), 0)
    outs = []
    for gi, win in enumerate(POOL_WINDOWS):
        lo, hi = gi * POOL_GROUP, (gi + 1) * POOL_GROUP
        ug = u[:, lo:hi]
        total = ug
        for jj in range(1, win):
            total = total + ext_ref[POOL_HALO - jj:POOL_HALO - jj + ts, lo:hi]
        cnt = jnp.minimum(t + 1, win).astype(F32)
        dlt = total / cnt - ug
        outs.append(_bdot(dlt.astype(BF16), wp_ref[gi]))
    y = jnp.concatenate(outs, axis=-1) * ps_ref[...]
    return _bdot(y.astype(BF16), wa_ref[...])


def _inproj_kernel(x_ref, g_ref, sc_ref, sh_ref, qg_ref, kg_ref, cos_ref, sin_ref,
                   wu_ref, wq_ref, wkc_ref, wvc_ref, wkv_ref, wgl_ref, wmg_ref, wp_ref, ps_ref, wa_ref,
                   ya_ref, q_ref, kc_ref, vc_ref, ks_ref, vs_ref, kw_ref, vw_ref, gl_ref, mg_ref,
                   ext_ref, *, tiles_per_seq):
    x = x_ref[...]
    ms = jnp.mean(x * x, axis=-1, keepdims=True)
    h = x * lax.rsqrt(ms + EPS) * g_ref[...]
    h = h * (1.0 + sc_ref[0]) + sh_ref[0]
    hb = h.astype(BF16)
    cos, sin = cos_ref[...], sin_ref[...]

    yq = _bdot(hb, wq_ref[...])
    kv = _bdot(hb, wkv_ref[...])

    for j in range(yq.shape[1] // LANES):
        q2 = _norm_rope_pair(yq[:, j * LANES:(j + 1) * LANES], qg_ref[...], cos, sin)
        q_ref[:, j * LANES:(j + 1) * LANES] = (q2 * QK_SCALE).astype(BF16)

    mg_ref[...] = _bdot(hb, wmg_ref[...]).astype(BF16)
    u = _bdot(hb, wu_ref[...])

    ks = _norm_rope_pair(kv[:, 0:LANES], kg_ref[0:1, :], cos, sin).T.astype(BF16)
    kw = _norm_rope_pair(kv[:, 2 * LANES:3 * LANES], kg_ref[1:2, :], cos, sin).T.astype(BF16)
    for gi in range(N_KV):
        ks_ref[0, gi] = ks[gi * HEAD_DIM:(gi + 1) * HEAD_DIM, :]
        kw_ref[0, gi] = kw[gi * HEAD_DIM:(gi + 1) * HEAD_DIM, :]
        vs_ref[0, gi] = _values_with_ones(kv[:, LANES:2 * LANES], gi)
        vw_ref[0, gi] = _values_with_ones(kv[:, 3 * LANES:4 * LANES], gi)

    kc_ref[...] = _bdot(hb, wkc_ref[...])
    vc_ref[...] = _bdot(hb, wvc_ref[...])
    gl_ref[...] = _bdot(hb, wgl_ref[...])
    ya_ref[...] = _pool_mixer(u, pl.program_id(0) % tiles_per_seq, wp_ref, ps_ref, wa_ref, ext_ref).astype(BF16)


def _in_projection(x2, gain, scale, shift, q_gain2, k_gain2, cos2, sin2, weights, w_pool, pool_scale, w_proj_a,
                   batch, seq):
    n, d = x2.shape
    tm = ROW_TILE
    tps = seq // tm
    w_u, w_q, w_kc, w_vc, w_kv, w_gl, w_mg = weights
    const = lambda a: pl.BlockSpec(a.shape, lambda i: (0,) * a.ndim)
    rows = lambda width: pl.BlockSpec((tm, width), lambda i: (i, 0))
    per_b = pl.BlockSpec((1, 1, d), lambda i: (i // tps, 0, 0))
    table = pl.BlockSpec((tm, LANES), lambda i: (i % tps, 0))
    kt_shape = jax.ShapeDtypeStruct((batch, N_KV, HEAD_DIM, seq), BF16)
    v_shape = jax.ShapeDtypeStruct((batch, N_KV, seq, LANES), BF16)
    kt_spec = pl.BlockSpec((1, N_KV, HEAD_DIM, tm), lambda i: (i // tps, 0, 0, i % tps))
    v_spec = pl.BlockSpec((1, N_KV, tm, LANES), lambda i: (i // tps, 0, i % tps, 0))
    f32_out = lambda width: jax.ShapeDtypeStruct((n, width), F32)
    bf16_out = lambda width: jax.ShapeDtypeStruct((n, width), BF16)
    pool_scale2 = pool_scale.reshape(1, POOL_WIDTH)
    return pl.pallas_call(
        functools.partial(_inproj_kernel, tiles_per_seq=tps),
        out_shape=[bf16_out(w_proj_a.shape[1]), bf16_out(w_q.shape[1]),
                   f32_out(w_kc.shape[1]), f32_out(w_vc.shape[1]), kt_shape, v_shape, kt_shape, v_shape,
                   f32_out(w_gl.shape[1]), bf16_out(w_mg.shape[1])],
        grid=(n // tm,),
        in_specs=[rows(d), const(gain.reshape(1, d)), per_b, per_b, const(q_gain2), const(k_gain2), table, table,
                  const(w_u), const(w_q), const(w_kc), const(w_vc), const(w_kv), const(w_gl), const(w_mg),
                  const(w_pool), const(pool_scale2), const(w_proj_a)],
        out_specs=[rows(w_proj_a.shape[1]), rows(w_q.shape[1]), rows(w_kc.shape[1]), rows(w_vc.shape[1]),
                   kt_spec, v_spec, kt_spec, v_spec, rows(w_gl.shape[1]), rows(w_mg.shape[1])],
        scratch_shapes=[pltpu.VMEM((tm + POOL_HALO, POOL_WIDTH), F32)],
        compiler_params=_cparams(("arbitrary",)),
        name="in_projection",
    )(x2, gain.reshape(1, d), scale[:, None, :], shift[:, None, :], q_gain2, k_gain2, cos2, sin2,
      w_u, w_q, w_kc, w_vc, w_kv, w_gl, w_mg, w_pool, pool_scale2, w_proj_a)


def _rope_tables(pos):
    inv_freq = np.float32(ROPE_THETA) ** (-(np.arange(ROPE_HALF, dtype=np.float32)) / np.float32(ROPE_HALF))
    ang = pos.astype(np.float32)[:, None] * inv_freq.astype(np.float32)[None, :]
    cos, sin = np.cos(ang), np.sin(ang)
    c = np.ones((pos.shape[0], HEAD_DIM), np.float32)
    s = np.zeros((pos.shape[0], HEAD_DIM), np.float32)
    c[:, :ROPE_HALF] = cos
    c[:, ROPE_HALF:ROPE_DIM] = cos
    s[:, :ROPE_HALF] = -sin
    s[:, ROPE_HALF:ROPE_DIM] = sin
    return c, s


def _norm_rope_pair(x, gain, cos, sin):
    rows = x.shape[0]
    lane = lax.broadcasted_iota(I32, (rows, LANES), 1)
    first = lane < HEAD_DIM
    x2 = x * x
    s0 = jnp.sum(jnp.where(first, x2, 0.0), axis=-1, keepdims=True)
    s1 = jnp.sum(jnp.where(first, 0.0, x2), axis=-1, keepdims=True)
    r = lax.rsqrt(jnp.where(first, s0, s1) * (1.0 / HEAD_DIM) + EPS)
    y = x * r * gain
    low = (lane & (HEAD_DIM - 1)) < ROPE_HALF
    sw = jnp.where(low, pltpu.roll(y, LANES - ROPE_HALF, 1), pltpu.roll(y, ROPE_HALF, 1))
    return y * cos + sw * sin


def _values_with_ones(v, gi):
    lane = lax.broadcasted_iota(I32, v.shape, 1)
    vg = v if gi == 0 else pltpu.roll(v, HEAD_DIM, 1)
    return jnp.where(lane < HEAD_DIM, vg, 1.0).astype(BF16)


def _compress_kernel(k_ref, v_ref, pos_ref, wa_ref, wb_ref, b1_ref, w2_ref, gain_ref, cos_ref, sin_ref,
                     kc_ref, vc_ref):
    nch = k_ref.shape[0]

    def mlp(x, idx):
        xa = (x + pos_ref[idx, 0:1, :]).astype(BF16)
        xb = (x + pos_ref[idx, 1:2, :]).astype(BF16)
        a = _bdot(xa, wa_ref[idx])
        b = _bdot(xb, wb_ref[idx])
        b_next = jnp.concatenate([b[1:nch, :], jnp.zeros((1, b.shape[1]), F32)], axis=0)
        pre = a + b_next + b1_ref[idx]
        hid = pre * _sigmoid(pre)
        return _bdot(hid.astype(BF16), w2_ref[idx])

    kc = _norm_rope_pair(mlp(k_ref[...], 0), gain_ref[...], cos_ref[...], sin_ref[...]).T.astype(BF16)
    vc = mlp(v_ref[...], 1)
    for gi in range(N_KV):
        kc_ref[0, gi] = kc[gi * HEAD_DIM:(gi + 1) * HEAD_DIM, :]
        vc_ref[0, gi] = _values_with_ones(vc, gi)


def _compress(kcm, vcm, pos2, wa, wb, b1, w2, gain, cos_c, sin_c, batch, seq):
    nch = seq // CMP_STRIDE
    width = CMP_STRIDE * LANES
    full = lambda a: pl.BlockSpec(a.shape, lambda b: (0,) * a.ndim)
    return pl.pallas_call(
        _compress_kernel,
        out_shape=[jax.ShapeDtypeStruct((batch, N_KV, HEAD_DIM, nch), BF16),
                   jax.ShapeDtypeStruct((batch, N_KV, nch, LANES), BF16)],
        grid=(batch,),
        in_specs=[pl.BlockSpec((nch, width), lambda b: (b, 0)),
                  pl.BlockSpec((nch, width), lambda b: (b, 0)),
                  full(pos2), full(wa), full(wb), full(b1), full(w2), full(gain), full(cos_c), full(sin_c)],
        out_specs=[pl.BlockSpec((1, N_KV, HEAD_DIM, nch), lambda b: (b, 0, 0, 0)),
                   pl.BlockSpec((1, N_KV, nch, LANES), lambda b: (b, 0, 0, 0))],
        compiler_params=_cparams(("arbitrary",)),
        name="compress",
    )(kcm.reshape(batch * nch, width), vcm.reshape(batch * nch, width), pos2, wa, wb, b1, w2, gain, cos_c, sin_c)


def _attn_kernel(q_ref, gl_ref, ks_ref, vs_ref, kw_ref, vw_ref, kc_ref, vc_ref, ovl_ref, eneg_ref,
                 o_ref, qa_scr, pre_scr, m_scr, acc_scr, *, seq):
    H, QB, TK = HEADS_PER_KV, Q_TILE, KV_TILE
    q0 = pl.program_id(2) * QB
    ncp = seq // CMP_STRIDE
    nsel = seq // SEL_LEN
    n_top = min(SEL_TOP, nsel)
    nt_dims = (((1,), (1,)), ((), ()))

    def row_sum(a):
        return a[:, HEAD_DIM:HEAD_DIM + 1]

    def pair_keys(k_t, lead=None):
        z = jnp.zeros_like(k_t)
        head = [] if lead is None else [lead]
        return (jnp.concatenate(head + [k_t, z], axis=0), jnp.concatenate(head + [z, k_t], axis=0))

    q_pairs = [q_ref[:, j * LANES:(j + 1) * LANES] for j in range(H // 2)]
    qpos = q0 + lax.broadcasted_iota(I32, (QB, 1), 0)

    def one_shot_branch(k_t, v_aug, mask, between=None):
        k_var = pair_keys(k_t)
        scores = [_bdot(q_pairs[h // 2], k_var[h % 2]) for h in range(H)]
        probs, accs = [], []
        for h in range(H):
            s = jnp.where(mask, scores[h], NEG)
            scores[h] = None
            p = jnp.exp2(s - jnp.max(s, axis=-1, keepdims=True))
            probs.append(p)
            accs.append(_bdot(p.astype(BF16), v_aug))
            if between is not None:
                between(h, probs, accs)
        return accs

    gs = _sigmoid(gl_ref[...])

    def compressed_branch(width):
        valid = (lax.broadcasted_iota(I32, (QB, width), 1) * CMP_STRIDE + (CMP_LEN - 1)) <= qpos
        has_key = qpos >= CMP_LEN - 1
        imp_state = {"ps": None}

        def finish_cmp_head(h, probs, accs):
            inv = jnp.where(has_key, 1.0 / row_sum(accs[h]), 0.0)
            pre_scr[h] = accs[h] * (gs[:, h:h + 1] * inv)
            term = probs[h] * inv
            imp_state["ps"] = term if imp_state["ps"] is None else imp_state["ps"] + term
            probs[h] = None

        def after_cmp_head(h, probs, accs):
            if h > 0:
                finish_cmp_head(h - 1, probs, accs)
            if h == H - 1:
                finish_cmp_head(h, probs, accs)

        one_shot_branch(kc_ref[0, 0, :, 0:width], vc_ref[0, 0, 0:width, :], valid, between=after_cmp_head)
        ps_hi, ps_lo = _split(imp_state["ps"])
        ovl_t = ovl_ref[:, 0:width]
        return (lax.dot_general(ovl_t, ps_hi, nt_dims, preferred_element_type=F32)
                + lax.dot_general(ovl_t, ps_lo, nt_dims, preferred_element_type=F32))

    imp = compressed_branch(ncp)

    sidx = lax.broadcasted_iota(I32, (nsel, QB), 0)
    cur = (q0 + lax.broadcasted_iota(I32, (nsel, QB), 1)) // SEL_LEN
    causal = sidx <= cur
    forced = (sidx == cur) | (sidx == 0)
    n_forced = 2
    pick_state = {"v": jnp.where(forced, -3e38, jnp.where(causal, imp, -1.0)), "sel": forced}

    def pick_next():
        v = pick_state["v"]
        mx = jnp.max(v, axis=0, keepdims=True)
        first = jnp.min(jnp.where(v == mx, sidx, nsel), axis=0, keepdims=True)
        pick = sidx == first
        pick_state["sel"] = pick_state["sel"] | pick
        pick_state["v"] = jnp.where(pick, -3e38, v)

    def picks_after_head(h, probs, accs):
        probs[h] = None
        n_pick = n_top - n_forced
        for _ in range(n_pick // H + (1 if h < n_pick % H else 0)):
            pick_next()

    span = WINDOW + QB
    base = pl.multiple_of(jnp.maximum(q0 - WINDOW, 0), LANES)
    dist = qpos - (base + lax.broadcasted_iota(I32, (QB, span), 1))
    inwin = (dist >= 0) & (dist < WINDOW)
    accs_w = one_shot_branch(kw_ref[0, 0, :, pl.ds(base, span)], vw_ref[0, 0, pl.ds(base, span), :],
                             inwin, between=picks_after_head)
    for h in range(H):
        pre_scr[h] = pre_scr[h] + accs_w[h] * (gs[:, 2 * H + h:2 * H + h + 1] / row_sum(accs_w[h]))

    notsel = jnp.where(pick_state["sel"] & causal, 0.0, 1.0).T.astype(BF16)
    for j in range(H // 2):
        qa_scr[j] = jnp.concatenate([notsel, q_pairs[j]], axis=1)

    m_scr[...] = jnp.full(m_scr.shape, NEG, F32)
    acc_scr[...] = jnp.zeros(acc_scr.shape, F32)

    def sel_tile(kt, causal_tile):
        off = pl.multiple_of(kt * TK, TK)
        k_var = pair_keys(ks_ref[0, 0, :, pl.ds(off, TK)], lead=eneg_ref[:, pl.ds(off, TK)])
        score = lambda h: _bdot(qa_scr[h // 2], k_var[h % 2])
        v_t = vs_ref[0, 0, pl.ds(off, TK), :]
        if causal_tile:
            keep = (off + lax.broadcasted_iota(I32, (QB, TK), 1)) <= qpos
        ahead = H if causal_tile else LOOP_AHEAD
        scores = [score(h) for h in range(ahead)]
        for h in range(H):
            s = scores[h]
            scores[h] = None
            if h + ahead < H:
                scores.append(score(h + ahead))
            if causal_tile:
                s = jnp.where(keep, s, NEG)
            m_old = m_scr[h]
            m_new = jnp.maximum(m_old, jnp.max(s, axis=-1, keepdims=True))
            alpha = jnp.exp2(m_old - m_new)
            pt = jnp.concatenate([jnp.exp2(s[:, c * LANES:(c + 1) * LANES] - m_new).astype(BF16)
                                  for c in range(TK // LANES)], axis=1)
            acc_scr[h] = alpha * acc_scr[h] + _bdot(pt, v_t)
            m_scr[h] = m_new

    n_tiles = (q0 + QB - 1) // TK + 1

    def body(kt, carry):
        sel_tile(kt, False)
        return carry

    lax.fori_loop(0, n_tiles - 1, body, 0)
    sel_tile(n_tiles - 1, True)

    def head_out(h):
        a_s = acc_scr[h]
        return pre_scr[h] + a_s * (gs[:, H + h:H + h + 1] / row_sum(a_s))

    low = lax.broadcasted_iota(I32, (QB, LANES), 1) < HEAD_DIM
    for j in range(H // 2):
        slab = jnp.where(low, head_out(2 * j), pltpu.roll(head_out(2 * j + 1), HEAD_DIM, 1))
        o_ref[:, j * LANES:(j + 1) * LANES] = slab.astype(BF16)


def _attention(q, gl, ks, vs, kw, vw, kc, vc, ovl, eneg, batch, seq):
    n = q.shape[0]
    QB = Q_TILE
    nq = seq // QB
    H = HEADS_PER_KV
    gw = HEADS_PER_KV * HEAD_DIM
    ncp = seq // CMP_STRIDE
    nsel = seq // SEL_LEN
    const2 = lambda a: pl.BlockSpec(a.shape, lambda b, g, j: (0,) * a.ndim)
    return pl.pallas_call(
        functools.partial(_attn_kernel, seq=seq),
        out_shape=jax.ShapeDtypeStruct((n, N_KV * gw), BF16),
        grid=(batch, N_KV, nq),
        in_specs=[pl.BlockSpec((QB, gw), lambda b, g, j: (b * nq + j, g)),
                  pl.BlockSpec((QB, LANES), lambda b, g, j: (b * nq + j, g)),
                  pl.BlockSpec((1, 1, HEAD_DIM, seq), lambda b, g, j: (b, g, 0, 0)),
                  pl.BlockSpec((1, 1, seq, LANES), lambda b, g, j: (b, g, 0, 0)),
                  pl.BlockSpec((1, 1, HEAD_DIM, seq), lambda b, g, j: (b, g, 0, 0)),
                  pl.BlockSpec((1, 1, seq, LANES), lambda b, g, j: (b, g, 0, 0)),
                  pl.BlockSpec((1, 1, HEAD_DIM, ncp), lambda b, g, j: (b, g, 0, 0)),
                  pl.BlockSpec((1, 1, ncp, LANES), lambda b, g, j: (b, g, 0, 0)),
                  const2(ovl), const2(eneg)],
        out_specs=pl.BlockSpec((QB, gw), lambda b, g, j: (b * nq + j, g)),
        scratch_shapes=[pltpu.VMEM((H // 2, QB, nsel + LANES), BF16), pltpu.VMEM((H, QB, LANES), F32),
                        pltpu.VMEM((H, QB, LANES), F32), pltpu.VMEM((H, QB, LANES), F32)],
        compiler_params=_cparams(("arbitrary", "arbitrary", "arbitrary")),
        name="sparse_attention",
    )(q, gl, ks, vs, kw, vw, kc, vc, ovl, eneg)


def _merge_kernel(x_ref, att_ref, ya_ref, mg_ref, g1_ref, wb_ref, wo_ref, o_ref):
    d = x_ref.shape[1]
    y_b = _bdot(att_ref[...], wb_ref[...])
    mg = mg_ref[...].astype(F32)
    merged = _sigmoid(mg[:, 0:d]) * ya_ref[...].astype(F32) + _sigmoid(mg[:, d:2 * d]) * y_b
    o_ref[...] = x_ref[...] + g1_ref[0] * _bdot(merged.astype(BF16), wo_ref[...])


def _merge(x2, att, y_a, mg, gate1, w_proj_b, w_out, seq):
    n, d = x2.shape
    tm = ROW_TILE
    tiles_per_b = seq // tm
    return pl.pallas_call(
        _merge_kernel,
        out_shape=jax.ShapeDtypeStruct((n, d), F32),
        grid=(n // tm,),
        in_specs=[pl.BlockSpec((tm, d), lambda i: (i, 0)),
                  pl.BlockSpec((tm, d), lambda i: (i, 0)),
                  pl.BlockSpec((tm, d), lambda i: (i, 0)),
                  pl.BlockSpec((tm, 2 * d), lambda i: (i, 0)),
                  pl.BlockSpec((1, 1, d), lambda i: (i // tiles_per_b, 0, 0)),
                  pl.BlockSpec(w_proj_b.shape, lambda i: (0, 0)),
                  pl.BlockSpec(w_out.shape, lambda i: (0, 0))],
        out_specs=pl.BlockSpec((tm, d), lambda i: (i, 0)),
        compiler_params=_cparams(("arbitrary",)),
        name="merge_out_projection",
    )(x2, att, y_a, mg, gate1[:, None, :], w_proj_b, w_out)


def _router_kernel(x_ref, g_ref, sc_ref, sh_ref, wh_ref, wl_ref, rb_ref, tri_ref,
                   h_ref, e_ref, gt_ref, pos_ref, cnt_ref, carry_ref):
    i = pl.program_id(0)
    tm, d = x_ref.shape

    @pl.when(i == 0)
    def _():
        carry_ref[...] = jnp.zeros((N_EXPERTS, 1), F32)

    x = x_ref[...]
    ms = jnp.mean(x * x, axis=-1, keepdims=True)
    h = x * lax.rsqrt(ms + EPS) * g_ref[...]
    h = h * (1.0 + sc_ref[0]) + sh_ref[0]
    for s in range(d // LANES):
        h_ref[pl.ds(s, tm, stride=SUBLANES), :] = h[:, s * LANES:(s + 1) * LANES]

    h_hi, h_lo = _split(h)
    nt = (((1,), (1,)), ((), ()))
    logits = (lax.dot_general(wh_ref[...], h_hi, nt, preferred_element_type=F32)
              + lax.dot_general(wh_ref[...], h_lo, nt, preferred_element_type=F32)
              + lax.dot_general(wl_ref[...], h_hi, nt, preferred_element_type=F32)
              + rb_ref[...])
    eidx = lax.broadcasted_iota(I32, (N_EXPERTS, tm), 0)
    onehot = jnp.zeros((N_EXPERTS, tm), F32)
    picks, vals, ids = [], [], []
    for _ in range(TOP_K):
        mx = jnp.max(logits, axis=0, keepdims=True)
        first = jnp.min(jnp.where(logits == mx, eidx, N_EXPERTS), axis=0, keepdims=True)
        pick = eidx == first
        picks.append(pick)
        vals.append(mx)
        ids.append(first)
        onehot = jnp.where(pick, 1.0, onehot)
        logits = jnp.where(pick, -3e38, logits)
    ex = [jnp.exp(vk - vals[0]) for vk in vals]
    den = ex[0] + ex[1] + ex[2] + ex[3]
    before = _bdot(onehot.astype(BF16), tri_ref[...]) + carry_ref[...]
    pad = SUBLANES - TOP_K
    pos = [jnp.sum(jnp.where(pk, before, 0.0), axis=0, keepdims=True) for pk in picks]
    e_ref[...] = jnp.concatenate(ids + [jnp.zeros((pad, tm), I32)], axis=0)
    gt_ref[...] = jnp.concatenate([e / den for e in ex] + [jnp.zeros((pad, tm), F32)], axis=0)
    pos_ref[...] = jnp.concatenate(pos + [jnp.zeros((pad, tm), F32)], axis=0).astype(I32)
    carry_ref[...] = carry_ref[...] + jnp.sum(onehot, axis=1, keepdims=True)
    cnt_ref[...] = carry_ref[...]


def _router(x2, gain, scale, shift, wr_hi, wr_lo, rb, tri, seq):
    n, d = x2.shape
    tm = ROUTE_TILE
    tiles_per_b = seq // tm
    nsub = d // LANES
    return pl.pallas_call(
        _router_kernel,
        out_shape=[jax.ShapeDtypeStruct((n * nsub, LANES), F32),
                   jax.ShapeDtypeStruct((SUBLANES, n), I32),
                   jax.ShapeDtypeStruct((SUBLANES, n), F32),
                   jax.ShapeDtypeStruct((SUBLANES, n), I32),
                   jax.ShapeDtypeStruct((N_EXPERTS, 1), F32)],
        grid=(n // tm,),
        in_specs=[pl.BlockSpec((tm, d), lambda i: (i, 0)),
                  pl.BlockSpec((1, d), lambda i: (0, 0)),
                  pl.BlockSpec((1, 1, d), lambda i: (i // tiles_per_b, 0, 0)),
                  pl.BlockSpec((1, 1, d), lambda i: (i // tiles_per_b, 0, 0)),
                  pl.BlockSpec(wr_hi.shape, lambda i: (0, 0)),
                  pl.BlockSpec(wr_lo.shape, lambda i: (0, 0)),
                  pl.BlockSpec((N_EXPERTS, 1), lambda i: (0, 0)),
                  pl.BlockSpec((tm, tm), lambda i: (0, 0))],
        out_specs=[pl.BlockSpec((tm * nsub, LANES), lambda i: (i, 0)),
                   pl.BlockSpec((SUBLANES, tm), lambda i: (0, i)),
                   pl.BlockSpec((SUBLANES, tm), lambda i: (0, i)),
                   pl.BlockSpec((SUBLANES, tm), lambda i: (0, i)),
                   pl.BlockSpec((N_EXPERTS, 1), lambda i: (0, 0))],
        scratch_shapes=[pltpu.VMEM((N_EXPERTS, 1), F32)],
        compiler_params=_cparams(("arbitrary",)),
        name="router",
    )(x2, gain.reshape(1, d), scale[:, None, :], shift[:, None, :], wr_hi, wr_lo, rb.reshape(N_EXPERTS, 1), tri)


def _dispatch_kernel(dest_ref, fill_ref, h_ref, xs_ref, zero_ref, sem, *, nsub):
    i = pl.program_id(0)
    tt = DISPATCH_TILE

    def row_copy(n, k):
        dst = dest_ref[0, 0, k * tt + n]
        return pltpu.make_async_copy(h_ref.at[pl.ds(n * nsub, nsub), :],
                                     xs_ref.at[pl.ds(dst * nsub, nsub), :], sem)

    def fill_copy(r):
        return pltpu.make_async_copy(zero_ref, xs_ref.at[pl.ds(r * nsub, nsub), :], sem)

    def for_fill_rows(fn):
        def per_range(e, carry):
            lax.fori_loop(fill_ref[2 * e], fill_ref[2 * e + 1], lambda r, c: (fn(fill_copy(r)), c)[1], 0)
            return carry

        lax.fori_loop(0, fill_ref.shape[0] // 2, per_range, 0)

    def for_token_rows(fn):
        def per_token(n, carry):
            for k in range(TOP_K):
                fn(row_copy(n, k))
            return carry

        lax.fori_loop(0, tt, per_token, 0)

    @pl.when(i == 0)
    def _():
        zero_ref[...] = jnp.zeros(zero_ref.shape, F32)
        for_fill_rows(lambda cp: cp.start())

    for_token_rows(lambda cp: cp.start())
    for_token_rows(lambda cp: cp.wait())

    @pl.when(i == 0)
    def _():
        for_fill_rows(lambda cp: cp.wait())


def _tile_indices(dest, tt):
    k, n = dest.shape
    return dest.reshape(k, n // tt, tt).transpose(1, 0, 2).reshape(n // tt, 1, k * tt)


def _dispatch(h_rows, dest, fill, n_rows, nsub):
    n_tok = h_rows.shape[0] // nsub
    tt = DISPATCH_TILE
    dest_tiles = _tile_indices(dest, tt)
    grid_spec = pltpu.PrefetchScalarGridSpec(
        num_scalar_prefetch=0,
        grid=(n_tok // tt,),
        in_specs=[pl.BlockSpec((1, 1, TOP_K * tt), lambda i: (i, 0, 0), memory_space=pltpu.SMEM),
                  pl.BlockSpec(memory_space=pltpu.SMEM),
                  pl.BlockSpec((tt * nsub, LANES), lambda i: (i, 0))],
        out_specs=pl.BlockSpec(memory_space=pl.ANY),
        scratch_shapes=[pltpu.VMEM((nsub, LANES), F32), pltpu.SemaphoreType.DMA(())],
    )
    return pl.pallas_call(
        functools.partial(_dispatch_kernel, nsub=nsub),
        out_shape=jax.ShapeDtypeStruct((n_rows * nsub, LANES), F32),
        grid_spec=grid_spec,
        compiler_params=pltpu.CompilerParams(dimension_semantics=("arbitrary",), vmem_limit_bytes=VMEM_LIMIT,
                                             has_side_effects=True),
        name="dispatch",
    )(dest_tiles, fill, h_rows)


PAIR = 2 * LANES


def _pair_permutation():
    p = np.zeros((PAIR, PAIR), np.float32)
    p[2 * np.arange(LANES), np.arange(LANES)] = 1.0
    p[2 * np.arange(LANES) + 1, LANES + np.arange(LANES)] = 1.0
    return p


def _regroup_kernel(w_ref, p_ref, o_ref):
    w = w_ref[0].astype(BF16)
    for j in range(w.shape[1] // PAIR):
        o_ref[0, :, j * PAIR:(j + 1) * PAIR] = _bdot(w[:, j * PAIR:(j + 1) * PAIR], p_ref[...]).astype(BF16)


def _regroup_gate_up(w_gate_up):
    depth, n_e, d, f2 = w_gate_up.shape
    tm = 512
    perm = jnp.asarray(_pair_permutation(), BF16)
    out = pl.pallas_call(
        _regroup_kernel,
        out_shape=jax.ShapeDtypeStruct((depth * n_e, d, f2), BF16),
        grid=(depth * n_e, d // tm),
        in_specs=[pl.BlockSpec((1, tm, f2), lambda e, i: (e, i, 0)),
                  pl.BlockSpec((PAIR, PAIR), lambda e, i: (0, 0))],
        out_specs=pl.BlockSpec((1, tm, f2), lambda e, i: (e, i, 0)),
        compiler_params=_cparams(("arbitrary", "arbitrary")),
        name="regroup_gate_up",
    )(w_gate_up.reshape(depth * n_e, d, f2), perm)
    return out.reshape(depth, n_e, d, f2)


def _regroup_bias(b):
    lead = b.shape[:-1]
    return b.reshape(*lead, -1, LANES, 2).swapaxes(-1, -2).reshape(*lead, -1)


def _expert_kernel(be_ref, nu_ref, xs_ref, wgu_ref, bgu_ref, wd_ref, bd_ref, ys_ref, *, nsub):
    i = pl.program_id(0)
    rows = EXPERT_ROWS
    f = wd_ref.shape[1]

    @pl.when(i < nu_ref[0])
    def _():
        x = jnp.concatenate([xs_ref[pl.ds(s, rows, stride=nsub), :] for s in range(nsub)], axis=-1)
        gu = _bdot(x.astype(BF16), wgu_ref[0]) + bgu_ref[0]
        acts = []
        for j in range(f // LANES):
            g_lin = jnp.minimum(gu[:, j * PAIR:j * PAIR + LANES], SWIGLU_LIMIT)
            u_lin = jnp.clip(gu[:, j * PAIR + LANES:(j + 1) * PAIR], -SWIGLU_LIMIT, SWIGLU_LIMIT)
            acts.append(((u_lin + 1.0) * g_lin * _sigmoid(g_lin * SWIGLU_ALPHA)).astype(BF16))
        act = jnp.concatenate(acts, axis=-1)
        y = _bdot(act, wd_ref[0].astype(BF16)) + bd_ref[0]
        for s in range(nsub):
            ys_ref[pl.ds(s, rows, stride=nsub), :] = y[:, s * LANES:(s + 1) * LANES]

    @pl.when(i >= nu_ref[0])
    def _():
        ys_ref[...] = jnp.zeros(ys_ref.shape, F32)


def _experts(blk_e, n_used, xs, w_gu, b_gu, w_d, b_d, nsub):
    rows = EXPERT_ROWS
    n_blocks = xs.shape[0] // (rows * nsub)
    d, f2 = w_gu.shape[1], w_gu.shape[2]
    f = w_d.shape[1]
    row_map = lambda i, be, nu: (jnp.minimum(i, nu[0] - 1), 0)
    grid_spec = pltpu.PrefetchScalarGridSpec(
        num_scalar_prefetch=2,
        grid=(n_blocks,),
        in_specs=[pl.BlockSpec((rows * nsub, LANES), row_map),
                  pl.BlockSpec((1, d, f2), lambda i, be, nu: (be[i], 0, 0)),
                  pl.BlockSpec((1, 1, f2), lambda i, be, nu: (be[i], 0, 0)),
                  pl.BlockSpec((1, f, d), lambda i, be, nu: (be[i], 0, 0)),
                  pl.BlockSpec((1, 1, d), lambda i, be, nu: (be[i], 0, 0))],
        out_specs=pl.BlockSpec((rows * nsub, LANES), lambda i, be, nu: (i, 0)),
    )
    return pl.pallas_call(
        functools.partial(_expert_kernel, nsub=nsub),
        out_shape=jax.ShapeDtypeStruct(xs.shape, F32),
        grid_spec=grid_spec,
        compiler_params=_cparams(("arbitrary",)),
        name="experts",
    )(blk_e, n_used, xs, w_gu, b_gu, w_d, b_d)


def _combine_kernel(cur_ref, nxt_ref, ys_ref, x_ref, gt_ref, g2_ref, o_ref, buf_ref, sems, *, nsub, n_steps):
    i = pl.program_id(0)
    tt = COMBINE_TILE
    slot = i % 2

    def for_tile_rows(idx_ref, s, fn):
        def per_token(n, carry):
            for k in range(TOP_K):
                src = idx_ref[0, 0, k * tt + n]
                fn(pltpu.make_async_copy(ys_ref.at[pl.ds(src * nsub, nsub), :],
                                         buf_ref.at[s, k, pl.ds(n * nsub, nsub), :], sems.at[s]))
            return carry

        lax.fori_loop(0, tt, per_token, 0)

    @pl.when(i == 0)
    def _():
        for_tile_rows(cur_ref, 0, lambda cp: cp.start())

    @pl.when(i + 1 < n_steps)
    def _():
        for_tile_rows(nxt_ref, 1 - slot, lambda cp: cp.start())

    for_tile_rows(cur_ref, slot, lambda cp: cp.wait())
    gt = gt_ref[...]
    acc = None
    for k in range(TOP_K):
        yk = jnp.concatenate([buf_ref[slot, k, pl.ds(s, tt, stride=nsub), :] for s in range(nsub)], axis=-1)
        term = gt[:, k:k + 1] * yk
        acc = term if acc is None else acc + term
    o_ref[...] = x_ref[...] + g2_ref[0] * acc


def _combine(dest, ys, x2, gates_t, gate2, seq, nsub):
    n, d = x2.shape
    tt = COMBINE_TILE
    tiles_per_b = seq // tt
    n_steps = n // tt
    dest_tiles = _tile_indices(dest, tt)
    grid_spec = pltpu.PrefetchScalarGridSpec(
        num_scalar_prefetch=0,
        grid=(n_steps,),
        in_specs=[pl.BlockSpec((1, 1, TOP_K * tt), lambda i: (i, 0, 0), memory_space=pltpu.SMEM),
                  pl.BlockSpec((1, 1, TOP_K * tt), lambda i: (jnp.minimum(i + 1, n_steps - 1), 0, 0),
                               memory_space=pltpu.SMEM),
                  pl.BlockSpec(memory_space=pl.ANY),
                  pl.BlockSpec((tt, d), lambda i: (i, 0)),
                  pl.BlockSpec((tt, SUBLANES), lambda i: (i, 0)),
                  pl.BlockSpec((1, 1, d), lambda i: (i // tiles_per_b, 0, 0))],
        out_specs=pl.BlockSpec((tt, d), lambda i: (i, 0)),
        scratch_shapes=[pltpu.VMEM((2, TOP_K, tt * nsub, LANES), F32), pltpu.SemaphoreType.DMA((2,))],
    )
    return pl.pallas_call(
        functools.partial(_combine_kernel, nsub=nsub, n_steps=n_steps),
        out_shape=jax.ShapeDtypeStruct((n, d), F32),
        grid_spec=grid_spec,
        compiler_params=_cparams(("arbitrary",)),
        name="combine",
    )(dest_tiles, dest_tiles, ys, x2, gates_t, gate2[:, None, :])


def _attention_constants(seq):
    nsel = seq // SEL_LEN
    ncp = seq // CMP_STRIDE
    cmp_start = np.arange(ncp) * CMP_STRIDE
    sel_start = np.arange(nsel) * SEL_LEN
    ovl = ((cmp_start[:, None] < sel_start[None, :] + SEL_LEN)
           & (cmp_start[:, None] + CMP_LEN - 1 >= sel_start[None, :])).astype(np.float32)
    ovl[ncp - 1, :] = 0.0
    eneg = np.where(np.arange(seq)[None, :] // SEL_LEN == np.arange(nsel)[:, None], NEG, 0.0).astype(np.float32)
    return jnp.asarray(ovl.T, BF16), jnp.asarray(eneg, BF16)


def _layer(x2, mod, consts, p, batch, seq):
    n, d = x2.shape
    shift1, scale1, gate1, shift2, scale2, gate2 = [mod[:, k * d:(k + 1) * d] for k in range(6)]
    (cos2, sin2, cos_c, sin_c, ovl, eneg, tri) = consts

    y_a, q, kcm, vcm, ks, vs, kw, vw, gl, mg = _in_projection(
        x2, p["norm1"], scale1, shift1, p["q_gain2"], p["k_gain2"], cos2, sin2, p["w_in_parts"],
        p["w_pool"], p["pool_scale"], p["w_proj_a"], batch, seq)
    kc, vc = _compress(kcm, vcm, p["cmp_pos2"], p["cmp_wa"], p["cmp_wb"], p["cmp_b1"], p["cmp_w2"],
                       p["kc_gain"], cos_c, sin_c, batch, seq)
    att = _attention(q, gl, ks, vs, kw, vw, kc, vc, ovl, eneg, batch, seq)
    x2 = _merge(x2, att, y_a, mg, gate1, p["w_proj_b"], p["w_out"], seq)

    nsub = d // LANES
    h_rows, top_e, gates, pos, counts = _router(x2, p["norm2"], scale2, shift2, p["wr_hi"], p["wr_lo"],
                                                p["router_b"], tri, seq)
    rows = EXPERT_ROWS
    n_blocks = -(-n * TOP_K // rows) + N_EXPERTS
    cnt = counts[:, 0].astype(I32)
    padded = (cnt + rows - 1) // rows * rows
    pad_end = jnp.cumsum(padded)
    pad_start = pad_end - padded
    e_ids = jnp.arange(N_EXPERTS, dtype=I32)
    dest = jnp.sum(jnp.where(top_e[:TOP_K, :, None] == e_ids, pad_start, 0), axis=-1) + pos[:TOP_K]
    fill_lo = jnp.concatenate([pad_start + cnt, pad_end[-1:]])
    fill_hi = jnp.concatenate([pad_end, jnp.full((1,), n_blocks * rows, I32)])
    fill = jnp.stack([fill_lo, fill_hi], axis=1).reshape(-1).astype(I32)
    blk_first = jnp.arange(n_blocks, dtype=I32)[:, None] * rows
    blk_e = jnp.minimum(jnp.sum((pad_end[None, :] <= blk_first).astype(I32), axis=1), N_EXPERTS - 1)
    n_used = (pad_end[-1:] // rows).astype(I32)
    xs = _dispatch(h_rows, dest, fill, n_blocks * rows, nsub)
    ys = _experts(blk_e, n_used, xs, p["w_gu"], p["b_gu"], p["w_d"], p["b_d"], nsub)
    gates_t = gates.T
    return _combine(dest, ys, x2, gates_t, gate2, seq, nsub)


def _prep_layer(l, norm1, norm2, w_in, w_pool, pool_scale, q_norm, k_norm, cmp_pos, cmp_w1, cmp_b1, cmp_w2,
                w_proj_a, w_proj_b, w_out, router_w, router_b, w_gate_up, b_gate_up, w_down, b_down):
    d = w_in.shape[1]
    w = w_in[l]
    o_q = POOL_WIDTH
    o_kv = o_q + N_KV * HEADS_PER_KV * HEAD_DIM
    o_g = o_kv + 6 * LANES
    n_gate = 3 * N_KV * HEADS_PER_KV
    o_m = o_g + n_gate
    wg = w[:, o_g:o_m].reshape(d, 3, N_KV, HEADS_PER_KV).transpose(0, 2, 1, 3).reshape(d, N_KV, 3 * HEADS_PER_KV)
    wg = jnp.pad(wg, ((0, 0), (0, 0), (0, LANES - 3 * HEADS_PER_KV))).reshape(d, N_KV * LANES)
    parts = [w[:, 0:o_q], w[:, o_q:o_kv], w[:, o_kv:o_kv + LANES], w[:, o_kv + LANES:o_kv + 2 * LANES],
             w[:, o_kv + 2 * LANES:o_g], wg, w[:, o_m:]]
    w_in_parts = [a.astype(BF16) for a in parts]

    half = CMP_LEN // 2
    w1 = cmp_w1[l].reshape(2, CMP_LEN, HEAD_DIM, CMP_HIDDEN)

    def chunk_weight(wh):
        z = jnp.zeros_like(wh)
        g0 = jnp.stack([wh, z], axis=2)
        g1 = jnp.stack([z, wh], axis=2)
        return jnp.concatenate([g0, g1], axis=-1).reshape(2, half * LANES, N_KV * CMP_HIDDEN)

    cmp_wa = chunk_weight(w1[:, :half]).astype(BF16)
    cmp_wb = chunk_weight(w1[:, half:]).astype(BF16)
    pos = cmp_pos[l]
    pos_t = jnp.broadcast_to(pos[:, :, None, :], (2, CMP_LEN, N_KV, HEAD_DIM))
    cmp_pos2 = jnp.stack([pos_t[:, :half].reshape(2, half * LANES), pos_t[:, half:].reshape(2, half * LANES)], axis=1)
    b1 = jnp.tile(cmp_b1[l], (1, N_KV))[:, None, :]
    w2 = cmp_w2[l]
    z2 = jnp.zeros_like(w2)
    cmp_w2b = jnp.concatenate([jnp.concatenate([w2, z2], axis=-1), jnp.concatenate([z2, w2], axis=-1)],
                              axis=1).astype(BF16)
    wr = router_w[l].T
    wr_hi = wr.astype(BF16)
    wr_lo = (wr - wr_hi.astype(F32)).astype(BF16)
    w_gu = w_gate_up[l]
    b_gu = _regroup_bias(b_gate_up[l])[:, None, :]
    return dict(
        norm1=norm1[l], norm2=norm2[l], w_in_parts=w_in_parts,
        w_pool=w_pool[l].astype(BF16), pool_scale=pool_scale[l], w_proj_a=w_proj_a[l].astype(BF16),
        k_gain2=jnp.stack([jnp.tile(k_norm[l, 1], N_KV), jnp.tile(k_norm[l, 2], N_KV)], axis=0),
        kc_gain=jnp.tile(k_norm[l, 0], N_KV)[None, :], q_gain2=jnp.tile(q_norm[l], 2)[None, :],
        cmp_pos2=cmp_pos2, cmp_wa=cmp_wa, cmp_wb=cmp_wb, cmp_b1=b1, cmp_w2=cmp_w2b,
        w_proj_b=w_proj_b[l].astype(BF16), w_out=w_out[l].astype(BF16),
        wr_hi=wr_hi, wr_lo=wr_lo, router_b=router_b[l],
        w_gu=w_gu, b_gu=b_gu, w_d=w_down[l], b_d=b_down[l][:, None, :],
    )


def kernel(x, c, norm1, norm2, ada_w, ada_b, w_in, w_pool, pool_scale, q_norm, k_norm, cmp_pos, cmp_w1, cmp_b1,
           cmp_w2, w_proj_a, w_proj_b, w_out, router_w, router_b, w_gate_up, b_gate_up, w_down, b_down):
    batch, seq, d = x.shape
    depth = norm1.shape[0]
    assert seq % KV_TILE == 0 and seq % ROUTE_TILE == 0 and seq >= WINDOW + Q_TILE and d % LANES == 0
    assert (batch * seq * TOP_K) % EXPERT_ROWS == 0 and (batch * seq) % DISPATCH_TILE == 0
    assert seq % COMBINE_TILE == 0 and seq % Q_TILE == 0 and seq % ROW_TILE == 0

    pos = np.arange(seq)
    cq, sq = _rope_tables(pos)
    cc, sc = _rope_tables(np.arange(seq // CMP_STRIDE) * CMP_STRIDE + CMP_LEN - 1)
    tile2 = lambda a: jnp.asarray(np.concatenate([a, a], axis=1))
    tri = jnp.asarray(np.triu(np.ones((ROUTE_TILE, ROUTE_TILE), np.float32), 1), BF16)
    consts = (tile2(cq), tile2(sq), tile2(cc), tile2(sc), *_attention_constants(seq), tri)

    c_pad = jnp.pad(c, ((0, SUBLANES - batch % SUBLANES if batch % SUBLANES else 0), (0, 0)))
    x2 = x.reshape(batch * seq, d)
    w_gate_up = _regroup_gate_up(w_gate_up)
    for l in range(depth):
        mod = _modulation(c_pad, ada_w[l], ada_b[l])[:batch]
        p = _prep_layer(l, norm1, norm2, w_in, w_pool, pool_scale, q_norm, k_norm, cmp_pos, cmp_w1, cmp_b1, cmp_w2,
                        w_proj_a, w_proj_b, w_out, router_w, router_b, w_gate_up, b_gate_up, w_down, b_down)
        x2 = _layer(x2, mod, consts, p, batch, seq)
    return x2.reshape(batch, seq, d)
```
